```python
import math
import jax, jax.numpy as jnp
from jax import lax
import numpy as np

D_MODEL = 1024
BATCH = 8
SEQ = 2048
DEPTH = 4
DEC_BATCH = 128
DEC_SEQ = 8
PAST_LEN = 16384
PAGE_SIZE = 128

N_MIXERS = 2
N_HGRN = (DEPTH + 1) // 2
N_RWKV = DEPTH // 2
MIX_W = D_MODEL
HG_HEADS = 8
HG_DK = MIX_W // HG_HEADS
HG_DV = MIX_W // HG_HEADS
HG_CHUNK = 64
RW_N = 64
RW_HEADS = MIX_W // RW_N
RW_DECAY_LORA = 64
RW_ICL_LORA = 64
RW_VRES_LORA = 32
RW_LNX_EPS = 64e-5
MEM_LEN = 256
X_HEADS = 4
X_DH = 128
CROSS_W = X_HEADS * X_DH
BRANCH_W = MIX_W + CROSS_W
IN_COLS = 3 * MIX_W + CROSS_W + BRANCH_W
NORM_EPS = 1e-6
LOG_FLOOR = 1e-30

kernel_name = 'hgrn2_rwkv7_memxattn_decode_step'

F32 = jnp.float32


def rmsnorm(x, g):
    xf = x.astype(F32)
    xf = xf * lax.rsqrt(jnp.mean(xf * xf, axis=-1, keepdims=True) + NORM_EPS)
    return (xf * g.astype(F32)).astype(x.dtype)


def memory_kv(mem, g, w):
    kv = rmsnorm(mem, g) @ w
    k, v = jnp.split(kv, 2, axis=-1)
    B, M = mem.shape[0], mem.shape[1]
    return k.reshape(B, M, X_HEADS, X_DH), v.reshape(B, M, X_HEADS, X_DH)


def cross_attend(q, mk, mv):
    B, T = q.shape[0], q.shape[1]
    q = q.reshape(B, T, X_HEADS, X_DH)
    s = jnp.einsum('bthd,bmhd->bhtm', q, mk.astype(q.dtype)).astype(F32) * (X_DH ** -0.5)
    p = jax.nn.softmax(s, axis=-1).astype(q.dtype)
    o = jnp.einsum('bhtm,bmhd->bthd', p, mv.astype(q.dtype))
    return o.reshape(B, T, CROSS_W)


def hgrn2_scan(q, k, v, logf, s0):
    B, T, H, DK = q.shape
    C = math.gcd(T, HG_CHUNK)
    n = T // C

    def to_chunks(a):
        return a.reshape(B, n, C, H, a.shape[-1]).transpose(1, 0, 3, 2, 4)

    causal = jnp.tril(jnp.ones((C, C), dtype=bool))[:, :, None]

    def step(S, inp):
        qc, kc, vc, gc = inp
        b = jnp.cumsum(gc, axis=2)
        diff = b[:, :, :, None, :] - b[:, :, None, :, :]
        decay = jnp.where(causal, jnp.exp(jnp.where(causal, diff, 0.0)), 0.0)
        att = jnp.einsum('bhtsd,bhsd->bhts', qc[:, :, :, None, :] * decay, kc)
        o = jnp.einsum('bhts,bhse->bhte', att, vc) + jnp.einsum('bhtd,bhde->bhte', qc * jnp.exp(b), S)
        b_last = b[:, :, -1:, :]
        S = jnp.exp(b_last[:, :, 0, :])[..., None] * S + jnp.einsum('bhsd,bhse->bhde', kc * jnp.exp(b_last - b), vc)
        return S, o

    S, o = lax.scan(step, s0, (to_chunks(q), to_chunks(k), to_chunks(v), to_chunks(logf)))
    o = o.transpose(1, 0, 3, 2, 4).reshape(B, T, H, v.shape[-1])
    return o, S


def hgrn2_mixer(h, w_in_l, lb, onorm_g, s0):
    B, T, _ = h.shape
    proj = h @ w_in_l
    qz = proj[..., :MIX_W].astype(F32)
    fz = proj[..., MIX_W:2 * MIX_W].astype(F32)
    iv = proj[..., 2 * MIX_W:3 * MIX_W].astype(F32)
    rest = proj[..., 3 * MIX_W:]
    q = jax.nn.silu(qz)
    f = lb + (1.0 - lb) * jax.nn.sigmoid(fz)
    logf = jnp.log(jnp.maximum(f, LOG_FLOOR))
    k = (1.0 - lb) * jax.nn.sigmoid(-fz)
    hd = lambda t: t.reshape(B, T, HG_HEADS, -1)
    o, S = hgrn2_scan(hd(q), hd(k), hd(iv), hd(logf), s0.astype(F32))
    o = rmsnorm(o, onorm_g)
    return o.reshape(B, T, MIX_W).astype(h.dtype), rest, S.astype(s0.dtype)


def rwkv7_scan(r, w, k, v, a, b, s0):
    def step(S, inp):
        r_t, w_t, k_t, v_t, a_t, b_t = inp
        sa = jnp.einsum('bhij,bhj->bhi', S, a_t)
        S = S * w_t[:, :, None, :] + sa[..., None] * b_t[:, :, None, :] + v_t[..., None] * k_t[:, :, None, :]
        y = jnp.einsum('bhij,bhj->bhi', S, r_t)
        return S, y

    tm = lambda t: t.transpose(1, 0, 2, 3)
    S, y = lax.scan(step, s0, (tm(r), tm(w), tm(k), tm(v), tm(a), tm(b)))
    return tm(y), S


def rwkv7_mixer(h, prev, w_in_l, j, p, v_first, s0):
    B, T, D = h.shape
    h_prev = jnp.concatenate([prev[:, None, :].astype(h.dtype), h[:, :-1]], axis=1)
    dx = h_prev - h
    mu = p['rw_mu'][j].astype(h.dtype)
    xr = h + dx * mu[0]
    xw = h + dx * mu[1]
    xk = h + dx * mu[2]
    xv = h + dx * mu[3]
    xa = h + dx * mu[4]
    r = (xr @ w_in_l[:, :MIX_W]).astype(F32)
    k = (xk @ w_in_l[:, MIX_W:2 * MIX_W]).astype(F32)
    v = (xv @ w_in_l[:, 2 * MIX_W:3 * MIX_W]).astype(F32)
    rest = h @ w_in_l[:, 3 * MIX_W:]
    w = -jax.nn.softplus(-(p['rw_w0'][j] + jnp.tanh(xw @ p['rw_w1'][j]) @ p['rw_w2'][j]).astype(F32)) - 0.5
    decay = jnp.exp(-jnp.exp(w))
    if v_first is None:
        v_first = v
    else:
        m = j - 1
        v = v + (v_first - v) * jax.nn.sigmoid((p['rw_v0'][m] + (xv @ p['rw_v1'][m]) @ p['rw_v2'][m]).astype(F32))
    a = jax.nn.sigmoid((p['rw_a0'][j] + (xa @ p['rw_a1'][j]) @ p['rw_a2'][j]).astype(F32))
    hd = lambda t: t.reshape(B, T, RW_HEADS, RW_N)
    kk = hd(k * p['rw_kk'][j].astype(F32))
    kk = kk / jnp.maximum(jnp.sqrt(jnp.sum(kk * kk, axis=-1, keepdims=True)), 1e-12)
    k = k * (1.0 + (a - 1.0) * p['rw_ka'][j].astype(F32))
    rh, kh, vh = hd(r), hd(k), hd(v)
    y, S = rwkv7_scan(rh, hd(decay), kh, vh, -kk, kk * hd(a), s0.astype(F32))
    mean = jnp.mean(y, axis=-1, keepdims=True)
    var = jnp.mean(jnp.square(y - mean), axis=-1, keepdims=True)
    y = ((y - mean) * lax.rsqrt(var + RW_LNX_EPS)).reshape(B, T, D)
    y = y * p['rw_lnx_g'][j].astype(F32) + p['rw_lnx_b'][j].astype(F32)
    bonus = jnp.sum(rh * kh * p['rw_rk'][j].astype(F32), axis=-1, keepdims=True) * vh
    y = y + bonus.reshape(B, T, D)
    return y.astype(h.dtype), rest, S.astype(s0.dtype), h[:, -1], v_first


def trunk(x, mem_k, mem_v, s_hgrn, s_rwkv, s_shift, p):
    lbs = jax.nn.softmax(p['hg_lb'].astype(F32), axis=0)
    lbs = jnp.cumsum(lbs, axis=0) - lbs[0]
    new_h, new_r, new_s = [], [], []
    v_first = None
    for i in range(DEPTH):
        h = rmsnorm(x, p['norm_g'][i])
        j = i // N_MIXERS
        if i % N_MIXERS == 0:
            mix, rest, S = hgrn2_mixer(h, p['w_in'][i], lbs[j], p['hg_onorm_g'][j], s_hgrn[j])
            new_h.append(S)
        else:
            mix, rest, S, last, v_first = rwkv7_mixer(h, s_shift[j], p['w_in'][i], j, p, v_first, s_rwkv[j])
            new_r.append(S)
            new_s.append(last.astype(s_shift.dtype))
        xq = rest[..., :CROSS_W]
        gate = rest[..., CROSS_W:]
        xo = cross_attend(xq, mem_k[i], mem_v[i])
        branch = jnp.concatenate([mix, xo], axis=-1) * jax.nn.silu(gate)
        x = x + branch @ p['w_out'][i]
    y = rmsnorm(x, p['final_g'])
    return y, jnp.stack(new_h), jnp.stack(new_r), jnp.stack(new_s)


def setup_inputs(seed: int = 0) -> dict:
    key = jax.random.key(seed)
    ks = iter(jax.random.split(key, 40))
    nrm = lambda shape, scale: jax.random.normal(next(ks), shape, F32) * scale
    uni = lambda shape, lo, hi: jax.random.uniform(next(ks), shape, F32, lo, hi)
    return {
        'x_prompt': nrm((BATCH, SEQ, D_MODEL), 1.0),
        'x_sample': nrm((DEC_BATCH, DEC_SEQ, D_MODEL), 1.0),
        'mem_prompt': nrm((BATCH, MEM_LEN, D_MODEL), 1.0),
        'state_hgrn': nrm((N_HGRN, DEC_BATCH, HG_HEADS, HG_DK, HG_DV), 0.5),
        'state_rwkv': nrm((N_RWKV, DEC_BATCH, RW_HEADS, RW_N, RW_N), 0.3),
        'state_shift': nrm((N_RWKV, DEC_BATCH, D_MODEL), 1.0),
        'cache_mem_k': nrm((DEPTH, DEC_BATCH, MEM_LEN, X_HEADS, X_DH), 1.0),
        'cache_mem_v': nrm((DEPTH, DEC_BATCH, MEM_LEN, X_HEADS, X_DH), 1.0),
        'norm_g': 1.0 + nrm((DEPTH, D_MODEL), 0.02),
        'w_in': nrm((DEPTH, D_MODEL, IN_COLS), D_MODEL ** -0.5),
        'w_out': nrm((DEPTH, BRANCH_W, D_MODEL), BRANCH_W ** -0.5),
        'mem_norm_g': 1.0 + nrm((DEPTH, D_MODEL), 0.02),
        'w_mem_kv': nrm((DEPTH, D_MODEL, 2 * CROSS_W), D_MODEL ** -0.5),
        'hg_lb': nrm((N_HGRN, MIX_W), 0.1),
        'hg_onorm_g': 1.0 + nrm((N_HGRN, HG_DV), 0.02),
        'rw_mu': uni((N_RWKV, 5, D_MODEL), 0.0, 1.0),
        'rw_w0': uni((N_RWKV, D_MODEL), -4.0, 0.0),
        'rw_w1': nrm((N_RWKV, D_MODEL, RW_DECAY_LORA), D_MODEL ** -0.5),
        'rw_w2': nrm((N_RWKV, RW_DECAY_LORA, D_MODEL), 0.5 * RW_DECAY_LORA ** -0.5),
        'rw_a0': nrm((N_RWKV, D_MODEL), 0.1),
        'rw_a1': nrm((N_RWKV, D_MODEL, RW_ICL_LORA), D_MODEL ** -0.5),
        'rw_a2': nrm((N_RWKV, RW_ICL_LORA, D_MODEL), 0.5 * RW_ICL_LORA ** -0.5),
        'rw_v0': nrm((N_RWKV - 1, D_MODEL), 0.1),
        'rw_v1': nrm((N_RWKV - 1, D_MODEL, RW_VRES_LORA), D_MODEL ** -0.5),
        'rw_v2': nrm((N_RWKV - 1, RW_VRES_LORA, D_MODEL), 0.5 * RW_VRES_LORA ** -0.5),
        'rw_kk': 0.85 + nrm((N_RWKV, D_MODEL), 0.02),
        'rw_ka': 1.0 + nrm((N_RWKV, D_MODEL), 0.02),
        'rw_rk': nrm((N_RWKV, RW_HEADS, RW_N), 0.1),
        'rw_lnx_g': 1.0 + nrm((N_RWKV, D_MODEL), 0.02),
        'rw_lnx_b': nrm((N_RWKV, D_MODEL), 0.02),
        'final_g': 1.0 + nrm((D_MODEL,), 0.02),
    }


def reference(x_prompt, x_sample, mem_prompt, state_hgrn, state_rwkv, state_shift, cache_mem_k, cache_mem_v,
              norm_g, w_in, w_out, mem_norm_g, w_mem_kv, hg_lb, hg_onorm_g, rw_mu, rw_w0, rw_w1, rw_w2,
              rw_a0, rw_a1, rw_a2, rw_v0, rw_v1, rw_v2, rw_kk, rw_ka, rw_rk, rw_lnx_g, rw_lnx_b, final_g):
    p = dict(norm_g=norm_g, w_in=w_in, w_out=w_out, hg_lb=hg_lb, hg_onorm_g=hg_onorm_g, rw_mu=rw_mu,
             rw_w0=rw_w0, rw_w1=rw_w1, rw_w2=rw_w2, rw_a0=rw_a0, rw_a1=rw_a1, rw_a2=rw_a2,
             rw_v0=rw_v0, rw_v1=rw_v1, rw_v2=rw_v2, rw_kk=rw_kk, rw_ka=rw_ka, rw_rk=rw_rk,
             rw_lnx_g=rw_lnx_g, rw_lnx_b=rw_lnx_b, final_g=final_g)
    mk, mv = [], []
    for i in range(DEPTH):
        k_i, v_i = memory_kv(mem_prompt, mem_norm_g[i], w_mem_kv[i])
        mk.append(k_i)
        mv.append(v_i)
    mem_k_p = jnp.stack(mk)
    mem_v_p = jnp.stack(mv)
    B = x_prompt.shape[0]
    dt = x_prompt.dtype
    z_h = jnp.zeros((N_HGRN, B, HG_HEADS, HG_DK, HG_DV), dt)
    z_r = jnp.zeros((N_RWKV, B, RW_HEADS, RW_N, RW_N), dt)
    z_s = jnp.zeros((N_RWKV, B, D_MODEL), dt)
    y_prompt, sh_p, sr_p, ss_p = trunk(x_prompt, mem_k_p, mem_v_p, z_h, z_r, z_s, p)
    y_sample, sh_s, sr_s, ss_s = trunk(x_sample, cache_mem_k, cache_mem_v, state_hgrn, state_rwkv, state_shift, p)
    return (y_prompt, y_sample, sh_p, sr_p, ss_p, mem_k_p, mem_v_p, sh_s, sr_s, ss_s)
```

```python
import functools
import math

import jax
import jax.numpy as jnp
from jax import lax
from jax.experimental import pallas as pl
from jax.experimental.pallas import tpu as pltpu

F32 = jnp.float32
BF16 = jnp.bfloat16
HIGHEST = lax.Precision.HIGHEST

D_MODEL = 1024
DEPTH = 4
MIX_W = D_MODEL
HG_HEADS = 8
HG_D = MIX_W // HG_HEADS
RW_N = 64
RW_HEADS = MIX_W // RW_N
RW_PAIRS = RW_HEADS // 2
RW_LNX_EPS = 64e-5
MEM_LEN = 256
X_HEADS = 4
X_DH = 128
CROSS_W = X_HEADS * X_DH
BRANCH_W = MIX_W + CROSS_W
IN_COLS = 3 * MIX_W + CROSS_W + BRANCH_W
NORM_EPS = 1e-6
LOG_FLOOR = 1e-30
NEG_BIG = -1e30

LANES = 128
CHUNK = 64
HG_SUB = 16
COL_CHUNK = 512
VMEM_LIMIT = 56 * 1024 * 1024

NT = (((1,), (1,)), ((), ()))
TN = (((0,), (0,)), ((), ()))


def _params(*sem):
    return pltpu.CompilerParams(dimension_semantics=sem, vmem_limit_bytes=VMEM_LIMIT)


def _resident(shape, index_map):
    return pl.BlockSpec(shape, index_map, pipeline_mode=pl.Buffered(1))


def _rms(x, g):
    return x * lax.rsqrt(jnp.mean(x * x, axis=-1, keepdims=True) + NORM_EPS) * g


def _sigmoid(z):
    return 1.0 / (1.0 + jnp.exp(-z))


def _bdot(a, b):
    return jnp.dot(a.astype(BF16), b.astype(BF16), preferred_element_type=F32)


def _hdot(a, b):
    return jnp.dot(a, b, precision=HIGHEST, preferred_element_type=F32)


def _narrow(rows):
    if rows % 16 == 0:
        return lambda a: a.astype(BF16)
    return lambda a: a.astype(BF16).astype(F32)


def _log2(n):
    l = int(math.log2(n))
    assert 1 << l == n, n
    return l


def _memkv_kernel(x_ref, g_ref, w_ref, k_ref, v_ref):
    hb = _rms(x_ref[...], g_ref[...]).astype(BF16)
    kv = jnp.dot(hb, w_ref[...], preferred_element_type=F32)
    k_ref[...] = kv[:, :CROSS_W]
    v_ref[...] = kv[:, CROSS_W:]


def _memory_kv(mem2d, g, w):
    n = mem2d.shape[0]
    tm = min(512, n)
    out = jax.ShapeDtypeStruct((DEPTH, n, CROSS_W), F32)
    return pl.pallas_call(
        _memkv_kernel,
        grid=(DEPTH, n // tm),
        in_specs=[
            pl.BlockSpec((tm, D_MODEL), lambda l, i: (i, 0)),
            pl.BlockSpec((None, 1, D_MODEL), lambda l, i: (l, 0, 0)),
            pl.BlockSpec((None, D_MODEL, 2 * CROSS_W), lambda l, i: (l, 0, 0)),
        ],
        out_specs=[
            pl.BlockSpec((None, tm, CROSS_W), lambda l, i: (l, i, 0)),
            pl.BlockSpec((None, tm, CROSS_W), lambda l, i: (l, i, 0)),
        ],
        out_shape=[out, out],
        compiler_params=_params("arbitrary", "arbitrary"),
        name="memory_kv",
    )(mem2d, g.reshape(DEPTH, 1, D_MODEL), w)


def _hgrn_proj_kernel(x_ref, g_ref, w_ref, lb_ref, q_ref, k_ref, gl_ref, v_ref, xq_ref, sg_ref):
    hb = _rms(x_ref[...], g_ref[...]).astype(BF16)
    for j in range(IN_COLS // COL_CHUNK):
        c0 = j * COL_CHUNK
        z = jnp.dot(hb, w_ref[:, c0:c0 + COL_CHUNK], preferred_element_type=F32)
        if c0 < MIX_W:
            q_ref[:, c0:c0 + COL_CHUNK] = (z * _sigmoid(z)).astype(q_ref.dtype)
        elif c0 < 2 * MIX_W:
            d0 = c0 - MIX_W
            lb = lb_ref[:, d0:d0 + COL_CHUNK]
            f = lb + (1.0 - lb) * _sigmoid(z)
            gl_ref[:, d0:d0 + COL_CHUNK] = jnp.log(jnp.maximum(f, LOG_FLOOR))
            k_ref[:, d0:d0 + COL_CHUNK] = ((1.0 - lb) * _sigmoid(-z)).astype(k_ref.dtype)
        elif c0 < 3 * MIX_W:
            d0 = c0 - 2 * MIX_W
            v_ref[:, d0:d0 + COL_CHUNK] = z.astype(v_ref.dtype)
        elif c0 < 3 * MIX_W + CROSS_W:
            d0 = c0 - 3 * MIX_W
            xq_ref[:, d0:d0 + COL_CHUNK] = z
        else:
            d0 = c0 - 3 * MIX_W - CROSS_W
            sg_ref[:, d0:d0 + COL_CHUNK] = (z * _sigmoid(z)).astype(sg_ref.dtype)


def _hgrn_proj(x2, g, w, lb):
    n = x2.shape[0]
    tm = min(512, n)
    row = lambda width: pl.BlockSpec((tm, width), lambda i: (i, 0))
    sd = lambda width, dt: jax.ShapeDtypeStruct((n, width), dt)
    return pl.pallas_call(
        _hgrn_proj_kernel,
        grid=(n // tm,),
        in_specs=[
            row(D_MODEL),
            _resident((1, D_MODEL), lambda i: (0, 0)),
            _resident((D_MODEL, IN_COLS), lambda i: (0, 0)),
            _resident((1, MIX_W), lambda i: (0, 0)),
        ],
        out_specs=[row(MIX_W), row(MIX_W), row(MIX_W), row(MIX_W), row(CROSS_W), row(BRANCH_W)],
        out_shape=[sd(MIX_W, BF16), sd(MIX_W, BF16), sd(MIX_W, F32), sd(MIX_W, BF16),
                   sd(CROSS_W, F32), sd(BRANCH_W, BF16)],
        compiler_params=_params("arbitrary"),
        name="hgrn_proj",
    )(x2, g.reshape(1, D_MODEL), w, lb.reshape(1, MIX_W))


def _hgrn_scan_kernel(q_ref, k_ref, v_ref, g_ref, s0_ref, og_ref, o_ref, sout_ref, st_ref, *, rb, tseq):
    tc = min(tseq, CHUNK)
    gseq = CHUNK // tc
    sub = min(tc, HG_SUB)
    narrow = _narrow(tc)
    n_chunks = rb // CHUNK
    steps_per_seq = max(1, tseq // rb)
    n_states = st_ref.shape[0]
    i = pl.program_id(1)

    @pl.when(i % steps_per_seq == 0)
    def _():
        def load(s, c):
            st_ref[s] = s0_ref[s, 0].T
            return c
        lax.fori_loop(0, n_states, load, 0)

    t_i = lax.broadcasted_iota(jnp.int32, (CHUNK, CHUNK), 0)
    s_i = lax.broadcasted_iota(jnp.int32, (CHUNK, CHUNK), 1)
    same = (t_i >> _log2(tc)) == (s_i >> _log2(tc))
    tri = jnp.where(same & (s_i <= t_i), 1.0, 0.0).astype(F32)
    ones_seq = jnp.where(same, 1.0, 0.0).astype(F32)
    row_in_sub = lax.broadcasted_iota(jnp.int32, (CHUNK, 1), 0) & (sub - 1)
    og = og_ref[...]

    def chunk(j, carry):
        r0 = pl.multiple_of(j * CHUNK, CHUNK)
        rows = pl.ds(r0, CHUNK)
        q = q_ref[rows, :].astype(F32)
        k = k_ref[rows, :].astype(F32)
        v = v_ref[rows, :].astype(F32)
        g = g_ref[rows, :]
        b = _hdot(tri, g)
        b_end = _hdot(ones_seq, g)
        q_in = q * jnp.exp(b)
        k_dec = k * jnp.exp(b_end - b)
        vb = v.astype(BF16)

        o_parts = []
        for gi in range(gseq):
            sl = slice(gi * tc, (gi + 1) * tc)
            sidx = j * gseq + gi if n_states > 1 else 0
            st = st_ref[sidx]
            o_parts.append(lax.dot_general(narrow(q_in[sl]), narrow(st), NT, preferred_element_type=F32))
            st_ref[sidx] = st * jnp.exp(b_end[gi * tc:gi * tc + 1, :]) + lax.dot_general(
                narrow(v[sl]), narrow(k_dec[sl]), TN, preferred_element_type=F32)
        o = o_parts[0] if gseq == 1 else jnp.concatenate(o_parts, axis=0)

        o = o + jnp.sum(q * k, axis=-1, keepdims=True) * v
        for dist in range(1, sub):
            valid = row_in_sub >= dist
            k_s = pltpu.roll(k, dist, 0)
            v_s = pltpu.roll(v, dist, 0)
            b_s = pltpu.roll(b, dist, 0)
            e = jnp.exp(jnp.where(valid, b - b_s, NEG_BIG))
            o = o + jnp.sum(q * k_s * e, axis=-1, keepdims=True) * v_s

        if tc > sub:
            parts = [jnp.zeros((sub, HG_D), F32)]
            for bi in range(1, tc // sub):
                lo = bi * sub
                beta = b[lo - 1:lo, :]
                q_b = (q[lo:lo + sub] * jnp.exp(b[lo:lo + sub] - beta)).astype(BF16)
                k_b = (k[0:lo] * jnp.exp(beta - b[0:lo])).astype(BF16)
                att = lax.dot_general(q_b, k_b, NT, preferred_element_type=F32)
                parts.append(jnp.dot(att.astype(BF16), vb[0:lo], preferred_element_type=F32))
            o = o + jnp.concatenate(parts, axis=0)

        o = o * lax.rsqrt(jnp.mean(o * o, axis=-1, keepdims=True) + NORM_EPS) * og
        o_ref[rows, :] = o.astype(o_ref.dtype)
        return carry

    lax.fori_loop(0, n_chunks, chunk, 0)

    @pl.when(i % steps_per_seq == steps_per_seq - 1)
    def _():
        def store(s, c):
            sout_ref[s, 0] = st_ref[s].T
            return c
        lax.fori_loop(0, n_states, store, 0)


def _scan_geometry(n, tseq):
    rb = min(512, n)
    assert rb % CHUNK == 0 and n % rb == 0
    assert (tseq % rb == 0) or (rb % tseq == 0 and CHUNK % tseq == 0)
    n_states = max(1, rb // tseq)
    steps_per_seq = max(1, tseq // rb)
    return rb, n_states, steps_per_seq


def _hgrn_scan(q, k, v, gl, s0, og, tseq):
    n = q.shape[0]
    rb, n_states, steps_per_seq = _scan_geometry(n, tseq)
    tok = pl.BlockSpec((rb, HG_D), lambda h, i: (i, h))
    state = pl.BlockSpec((n_states, 1, HG_D, HG_D), lambda h, i: (i // steps_per_seq, h, 0, 0))
    return pl.pallas_call(
        functools.partial(_hgrn_scan_kernel, rb=rb, tseq=tseq),
        grid=(HG_HEADS, n // rb),
        in_specs=[tok, tok, tok, tok, state, _resident((1, HG_D), lambda h, i: (0, 0))],
        out_specs=[tok, state],
        out_shape=[jax.ShapeDtypeStruct((n, MIX_W), BF16), jax.ShapeDtypeStruct(s0.shape, F32)],
        scratch_shapes=[pltpu.VMEM((n_states, HG_D, HG_D), F32)],
        compiler_params=_params("arbitrary", "arbitrary"),
        name="hgrn_scan",
    )(q, k, v, gl, s0, og.reshape(1, HG_D))


def _rwkv_proj_kernel(*refs, tseq, long_seq, has_vres):
    it = iter(refs)
    x_ref, prev_ref, ng_ref, mu_ref, w_ref = (next(it) for _ in range(5))
    w0_ref, w1_ref, w2_ref, a0_ref, a1_ref, a2_ref = (next(it) for _ in range(6))
    if has_vres:
        v0_ref, v1_ref, v2_ref, vf_ref = (next(it) for _ in range(4))
    r_ref, k_ref, v_ref, lw_ref, as_ref, xq_ref, sg_ref, h_ref = (next(it) for _ in range(8))
    carry_ref = next(it) if long_seq else None

    x = x_ref[0] if long_seq else x_ref[...]
    tm = x.shape[0]
    h = _rms(x, ng_ref[...])
    shifted = pltpu.roll(h, 1, 0)
    row = lax.broadcasted_iota(jnp.int32, (tm, 1), 0)
    if long_seq:
        tb = pl.program_id(1)

        @pl.when(tb == 0)
        def _():
            carry_ref[0:1, :] = prev_ref[0]

        h_prev = jnp.where(row == 0, carry_ref[0:1, :], shifted)
        carry_ref[0:1, :] = h[tm - 1:tm, :]

        @pl.when(tb == pl.num_programs(1) - 1)
        def _():
            h_ref[0] = h[tm - 1:tm, :]
    else:
        h_prev = jnp.where((row & (tseq - 1)) == 0, prev_ref[...], shifted)
        h_ref[...] = h

    dx = h_prev - h
    hb = h.astype(BF16)
    xr = (h + dx * mu_ref[0:1, :]).astype(BF16)
    xw = (h + dx * mu_ref[1:2, :]).astype(BF16)
    xk = (h + dx * mu_ref[2:3, :]).astype(BF16)
    xv = (h + dx * mu_ref[3:4, :]).astype(BF16)
    xa = (h + dx * mu_ref[4:5, :]).astype(BF16)

    w_mid = jnp.tanh(jnp.dot(xw, w1_ref[...], preferred_element_type=F32)).astype(BF16)
    a_mid = jnp.dot(xa, a1_ref[...], preferred_element_type=F32).astype(BF16)
    if has_vres:
        v_mid = jnp.dot(xv, v1_ref[...], preferred_element_type=F32).astype(BF16)

    for j in range(MIX_W // COL_CHUNK):
        c0 = j * COL_CHUNK
        cols = slice(c0, c0 + COL_CHUNK)
        r_ref[:, cols] = jnp.dot(xr, w_ref[:, c0:c0 + COL_CHUNK], preferred_element_type=F32)
        k_ref[:, cols] = jnp.dot(xk, w_ref[:, MIX_W + c0:MIX_W + c0 + COL_CHUNK], preferred_element_type=F32)
        vz = jnp.dot(xv, w_ref[:, 2 * MIX_W + c0:2 * MIX_W + c0 + COL_CHUNK], preferred_element_type=F32)
        if has_vres:
            gate = _sigmoid(v0_ref[:, cols] + jnp.dot(v_mid, v2_ref[:, cols], preferred_element_type=F32))
            vz = vz + (vf_ref[:, cols] - vz) * gate
        v_ref[:, cols] = vz
        u = w0_ref[:, cols] + jnp.dot(w_mid, w2_ref[:, cols], preferred_element_type=F32)
        softplus = jnp.maximum(-u, 0.0) + jnp.log(1.0 + jnp.exp(-jnp.abs(u)))
        lw_ref[:, cols] = -jnp.exp(-softplus - 0.5)
        as_ref[:, cols] = _sigmoid(a0_ref[:, cols] + jnp.dot(a_mid, a2_ref[:, cols], preferred_element_type=F32))

    base = 3 * MIX_W
    for j in range(CROSS_W // COL_CHUNK):
        c0 = j * COL_CHUNK
        xq_ref[:, c0:c0 + COL_CHUNK] = jnp.dot(hb, w_ref[:, base + c0:base + c0 + COL_CHUNK],
                                                preferred_element_type=F32)
    base = 3 * MIX_W + CROSS_W
    for j in range(BRANCH_W // COL_CHUNK):
        c0 = j * COL_CHUNK
        z = jnp.dot(hb, w_ref[:, base + c0:base + c0 + COL_CHUNK], preferred_element_type=F32)
        sg_ref[:, c0:c0 + COL_CHUNK] = (z * _sigmoid(z)).astype(sg_ref.dtype)


def _rwkv_proj(x2, shift, tseq, ng, mu, w, w0, w1, w2, a0, a1, a2, vres):
    n = x2.shape[0]
    nb = n // tseq
    tm = min(256, n)
    long_seq = tseq >= tm
    has_vres = vres is not None
    vec = lambda a: a.reshape(1, -1)
    mu8 = jnp.zeros((8, D_MODEL), F32).at[:5].set(mu)
    consts = [vec(ng), mu8, w, vec(w0), w1, w2, vec(a0), a1, a2]
    if has_vres:
        v0, v1, v2, v_first = vres
        consts += [vec(v0), v1, v2]
    if long_seq:
        assert tseq % tm == 0
        steps = tseq // tm
        grid = (nb, steps)
        cmap = lambda b, t: (0, 0)
        row = lambda width: pl.BlockSpec((tm, width), lambda b, t: (b * steps + t, 0))
        x_in = x2.reshape(nb, tseq, D_MODEL)
        x_spec = pl.BlockSpec((1, tm, D_MODEL), lambda b, t: (b, t, 0))
        prev_in = shift.reshape(nb, 1, D_MODEL)
        prev_spec = pl.BlockSpec((1, 1, D_MODEL), lambda b, t: (b, 0, 0))
        h_shape = jax.ShapeDtypeStruct((nb, 1, D_MODEL), F32)
        h_spec = pl.BlockSpec((1, 1, D_MODEL), lambda b, t: (b, 0, 0))
        scratch = [pltpu.VMEM((8, D_MODEL), F32)]
        sem = ("arbitrary", "arbitrary")
    else:
        assert tm % tseq == 0 and tseq & (tseq - 1) == 0
        grid = (n // tm,)
        cmap = lambda i: (0, 0)
        row = lambda width: pl.BlockSpec((tm, width), lambda i: (i, 0))
        x_in, x_spec = x2, row(D_MODEL)
        prev_in = jnp.zeros((nb, tseq, D_MODEL), F32).at[:, 0].set(shift).reshape(n, D_MODEL)
        prev_spec = row(D_MODEL)
        h_shape = jax.ShapeDtypeStruct((n, D_MODEL), F32)
        h_spec = row(D_MODEL)
        scratch = []
        sem = ("arbitrary",)
    in_specs = [x_spec, prev_spec] + [_resident(c.shape, cmap) for c in consts]
    args = [x_in, prev_in] + consts
    if has_vres:
        in_specs.append(row(MIX_W))
        args.append(v_first)
    sd = lambda width, dt: jax.ShapeDtypeStruct((n, width), dt)
    outs = pl.pallas_call(
        functools.partial(_rwkv_proj_kernel, tseq=tseq, long_seq=long_seq, has_vres=has_vres),
        grid=grid,
        in_specs=in_specs,
        out_specs=[row(MIX_W)] * 5 + [row(CROSS_W), row(BRANCH_W), h_spec],
        out_shape=[sd(MIX_W, F32)] * 5 + [sd(CROSS_W, F32), sd(BRANCH_W, BF16), h_shape],
        scratch_shapes=scratch,
        compiler_params=_params(*sem),
        name="rwkv_proj",
    )(*args)
    h_out = outs[7]
    new_shift = h_out.reshape(nb, D_MODEL) if long_seq else h_out.reshape(nb, tseq, D_MODEL)[:, -1]
    return list(outs[:7]) + [new_shift]


def _rwkv_scan_kernel(r_ref, k_ref, v_ref, lw_ref, as_ref, par_ref, s0_ref, y_ref, sout_ref, st_ref, *, rb, tseq):
    tc = min(tseq, CHUNK)
    gseq = CHUNK // tc
    n_chunks = rb // CHUNK
    steps_per_seq = max(1, tseq // rb)
    n_states = st_ref.shape[0]
    n_levels = _log2(tc)
    narrow = _narrow(tc)
    two = 2 * CHUNK
    i = pl.program_id(1)

    @pl.when(i % steps_per_seq == 0)
    def _():
        st_ref[...] = s0_ref[:, 0]

    t_i = lax.broadcasted_iota(jnp.int32, (CHUNK, CHUNK), 0)
    s_i = lax.broadcasted_iota(jnp.int32, (CHUNK, CHUNK), 1)
    same = (t_i >> _log2(tc)) == (s_i >> _log2(tc))
    tri = jnp.where(same & (s_i <= t_i), 1.0, 0.0).astype(F32)
    ones_seq = jnp.where(same, 1.0, 0.0).astype(F32)
    n_i = lax.broadcasted_iota(jnp.int32, (two, two), 0) & (CHUNK - 1)
    m_i = lax.broadcasted_iota(jnp.int32, (two, two), 1) & (CHUNK - 1)
    same2 = (n_i >> _log2(tc)) == (m_i >> _log2(tc))
    strict2 = jnp.where(same2 & (m_i < n_i), 1.0, 0.0).astype(F32)
    incl2 = jnp.where(same2 & (m_i <= n_i), 1.0, 0.0).astype(F32)
    lane = lax.broadcasted_iota(jnp.int32, (1, LANES), 1)
    head0 = jnp.where(lane < RW_N, 1.0, 0.0).astype(F32)
    head1 = 1.0 - head0
    ones_head = jnp.where((lax.broadcasted_iota(jnp.int32, (LANES, LANES), 0) >> _log2(RW_N))
                          == (lax.broadcasted_iota(jnp.int32, (LANES, LANES), 1) >> _log2(RW_N)),
                          1.0, 0.0).astype(F32)
    kk_w = par_ref[0:1, :]
    ka = par_ref[1:2, :]
    rk = par_ref[2:3, :]
    ln_g = par_ref[3:4, :]
    ln_b = par_ref[4:5, :]

    def stack(a):
        return jnp.concatenate([a * head0, a * head1], axis=0)

    def chunk(j, carry):
        r0 = pl.multiple_of(j * CHUNK, CHUNK)
        rows = pl.ds(r0, CHUNK)
        r = r_ref[rows, :]
        k = k_ref[rows, :]
        v = v_ref[rows, :]
        lw = lw_ref[rows, :]
        a_gate = as_ref[rows, :]

        c = _hdot(tri, lw)
        c_end = _hdot(ones_seq, lw)
        kk = k * kk_w
        kk = kk / jnp.maximum(jnp.sqrt(_hdot(kk * kk, ones_head)), 1e-12)
        b_vec = kk * a_gate
        k_mod = k * (1.0 + (a_gate - 1.0) * ka)
        e_neg = jnp.exp(-c)
        e_end = jnp.exp(c_end - c)
        a_t = -kk * jnp.exp(c - lw)
        r_t = r * jnp.exp(c)

        lhs = jnp.concatenate([stack(a_t), stack(r_t)], axis=0).astype(BF16)
        rhs = jnp.concatenate([stack(b_vec * e_neg), stack(k_mod * e_neg)], axis=0).astype(BF16)
        big = lax.dot_general(lhs, rhs, NT, preferred_element_type=F32)
        l_ab = big[0:two, 0:two] * strict2
        l_ak = big[0:two, two:2 * two] * strict2
        m_rb = big[two:2 * two, 0:two] * incl2
        m_rk = big[two:2 * two, two:2 * two] * incl2
        v_s = stack(v).astype(BF16)

        ah_parts, rh_parts, states = [], [], []
        for gi in range(gseq):
            sl = slice(gi * tc, (gi + 1) * tc)
            sidx = j * gseq + gi if n_states > 1 else 0
            st = st_ref[sidx]
            states.append((sidx, st))
            ar = jnp.concatenate([a_t[sl], r_t[sl]], axis=0)
            prod = lax.dot_general(narrow(ar), narrow(st), NT, preferred_element_type=F32)
            ah_parts.append(prod[0:tc])
            rh_parts.append(prod[tc:2 * tc])
        ah0 = ah_parts[0] if gseq == 1 else jnp.concatenate(ah_parts, axis=0)
        rh0 = rh_parts[0] if gseq == 1 else jnp.concatenate(rh_parts, axis=0)

        x = stack(ah0) + jnp.dot(l_ak.astype(BF16), v_s, preferred_element_type=F32)
        p = l_ab
        for lev in range(n_levels):
            x = x + _bdot(p, x)
            if lev < n_levels - 1:
                p = _bdot(p, p)
        y_s = _bdot(m_rb, x) + jnp.dot(m_rk.astype(BF16), v_s, preferred_element_type=F32)
        y = rh0 + y_s[0:CHUNK] + y_s[CHUNK:two]
        u = x[0:CHUNK] + x[CHUNK:two]

        b_hat = b_vec * e_end
        k_hat = k_mod * e_end
        for gi, (sidx, st) in enumerate(states):
            sl = slice(gi * tc, (gi + 1) * tc)
            uv = jnp.concatenate([u[sl], v[sl]], axis=0)
            bk = jnp.concatenate([b_hat[sl], k_hat[sl]], axis=0)
            upd = lax.dot_general(narrow(uv), narrow(bk), TN, preferred_element_type=F32)
            st_ref[sidx] = st * jnp.exp(c_end[gi * tc:gi * tc + 1, :]) + upd * ones_head

        mean = _hdot(y, ones_head) * (1.0 / RW_N)
        d = y - mean
        var = _hdot(d * d, ones_head) * (1.0 / RW_N)
        out = d * lax.rsqrt(var + RW_LNX_EPS) * ln_g + ln_b
        out = out + _hdot(r * k_mod * rk, ones_head) * v
        y_ref[rows, :] = out.astype(y_ref.dtype)
        return carry

    lax.fori_loop(0, n_chunks, chunk, 0)

    @pl.when(i % steps_per_seq == steps_per_seq - 1)
    def _():
        sout_ref[:, 0] = st_ref[...]


def _rwkv_scan(r, k, v, lw, a_gate, par, s0_pairs, tseq):
    n = r.shape[0]
    rb, n_states, steps_per_seq = _scan_geometry(n, tseq)
    tok = pl.BlockSpec((rb, LANES), lambda p, i: (i, p))
    state = pl.BlockSpec((n_states, 1, LANES, LANES), lambda p, i: (i // steps_per_seq, p, 0, 0))
    return pl.pallas_call(
        functools.partial(_rwkv_scan_kernel, rb=rb, tseq=tseq),
        grid=(RW_PAIRS, n // rb),
        in_specs=[tok, tok, tok, tok, tok, pl.BlockSpec((None, 8, LANES), lambda p, i: (p, 0, 0)), state],
        out_specs=[tok, state],
        out_shape=[jax.ShapeDtypeStruct((n, MIX_W), BF16), jax.ShapeDtypeStruct(s0_pairs.shape, F32)],
        scratch_shapes=[pltpu.VMEM((n_states, LANES, LANES), F32)],
        compiler_params=_params("arbitrary", "arbitrary"),
        name="rwkv_scan",
    )(r, k, v, lw, a_gate, par, s0_pairs)


def _pairs_from_heads(s):
    nb = s.shape[0]
    s = s.reshape(nb, RW_PAIRS, 2, RW_N, RW_N)
    z = jnp.zeros_like(s[:, :, 0])
    top = jnp.concatenate([s[:, :, 0], z], axis=-1)
    bot = jnp.concatenate([z, s[:, :, 1]], axis=-1)
    return jnp.concatenate([top, bot], axis=-2)


def _heads_from_pairs(p):
    nb = p.shape[0]
    s = jnp.stack([p[:, :, :RW_N, :RW_N], p[:, :, RW_N:, RW_N:]], axis=2)
    return s.reshape(nb, RW_HEADS, RW_N, RW_N)


def _xattn_kernel(q_ref, k_ref, v_ref, o_ref, *, rows_per_seq):
    n_seq = k_ref.shape[0]
    scale = X_DH ** -0.5

    def one(s, carry):
        r0 = pl.multiple_of(s * rows_per_seq, rows_per_seq)
        rows = pl.ds(r0, rows_per_seq)
        q = q_ref[rows, :]
        outs = []
        for hh in range(X_HEADS):
            cols = slice(hh * X_DH, (hh + 1) * X_DH)
            sc = lax.dot_general(q[:, cols].astype(BF16), k_ref[s, :, cols].astype(BF16), NT,
                                 preferred_element_type=F32) * scale
            p = jnp.exp(sc - jnp.max(sc, axis=-1, keepdims=True))
            p = p / jnp.sum(p, axis=-1, keepdims=True)
            outs.append(jnp.dot(p.astype(BF16), v_ref[s, :, cols].astype(BF16), preferred_element_type=F32))
        o_ref[rows, :] = jnp.concatenate(outs, axis=-1)
        return carry

    lax.fori_loop(0, n_seq, one, 0)


def _xattn(xq, mk, mv, tseq):
    n = xq.shape[0]
    rb = min(512, n, 8 * tseq)
    n_seq = max(1, rb // tseq)
    steps_per_seq = max(1, tseq // rb)
    rows_per_seq = rb // n_seq
    row = pl.BlockSpec((rb, CROSS_W), lambda i: (i, 0))
    mem = pl.BlockSpec((n_seq, MEM_LEN, CROSS_W), lambda i: (i // steps_per_seq, 0, 0))
    return pl.pallas_call(
        functools.partial(_xattn_kernel, rows_per_seq=rows_per_seq),
        grid=(n // rb,),
        in_specs=[row, mem, mem],
        out_specs=row,
        out_shape=jax.ShapeDtypeStruct((n, CROSS_W), F32),
        compiler_params=_params("arbitrary"),
        name="mem_xattn",
    )(xq, mk, mv)


def _out_kernel(x_ref, mix_ref, xo_ref, sg_ref, w_ref, fg_ref, o_ref, *, final):
    sg = sg_ref[...].astype(F32)
    left = (mix_ref[...].astype(F32) * sg[:, :MIX_W]).astype(BF16)
    right = (xo_ref[...] * sg[:, MIX_W:]).astype(BF16)
    x = x_ref[...] + jnp.dot(left, w_ref[0:MIX_W, :], preferred_element_type=F32) \
        + jnp.dot(right, w_ref[MIX_W:BRANCH_W, :], preferred_element_type=F32)
    o_ref[...] = _rms(x, fg_ref[...]) if final else x


def _out_proj(x2, mix, xo, sg, w, final_g, final):
    n = x2.shape[0]
    tm = min(512, n)
    row = lambda width: pl.BlockSpec((tm, width), lambda i: (i, 0))
    return pl.pallas_call(
        functools.partial(_out_kernel, final=final),
        grid=(n // tm,),
        in_specs=[row(D_MODEL), row(MIX_W), row(CROSS_W), row(BRANCH_W),
                  _resident((BRANCH_W, D_MODEL), lambda i: (0, 0)),
                  _resident((1, D_MODEL), lambda i: (0, 0))],
        out_specs=row(D_MODEL),
        out_shape=jax.ShapeDtypeStruct((n, D_MODEL), F32),
        compiler_params=_params("arbitrary"),
        name="out_proj",
    )(x2, mix, xo, sg, w, final_g.reshape(1, D_MODEL))


def _trunk(x, mem_k, mem_v, s_hgrn, s_rwkv, s_shift, p):
    nb, tseq, _ = x.shape
    x2 = x.reshape(nb * tseq, D_MODEL)
    new_h, new_r, new_s = [], [], []
    v_first = None
    for i in range(DEPTH):
        j = i // 2
        if i % 2 == 0:
            q, k, gl, v, xq, sg = _hgrn_proj(x2, p["norm_g"][i], p["w_in"][i], p["lbs"][j])
            mix, s_new = _hgrn_scan(q, k, v, gl, s_hgrn[j], p["hg_onorm_g"][j], tseq)
            new_h.append(s_new)
        else:
            vres = None if v_first is None else (p["rw_v0"][j - 1], p["rw_v1"][j - 1], p["rw_v2"][j - 1], v_first)
            r, k, v, lw, a_gate, xq, sg, shift = _rwkv_proj(
                x2, s_shift[j], tseq, p["norm_g"][i], p["rw_mu"][j], p["w_in"][i],
                p["rw_w0"][j], p["rw_w1"][j], p["rw_w2"][j], p["rw_a0"][j], p["rw_a1"][j], p["rw_a2"][j], vres)
            if v_first is None:
                v_first = v
            mix, s_new = _rwkv_scan(r, k, v, lw, a_gate, p["rw_par"][j], _pairs_from_heads(s_rwkv[j]), tseq)
            new_r.append(_heads_from_pairs(s_new))
            new_s.append(shift)
        xo = _xattn(xq, mem_k[i], mem_v[i], tseq)
        x2 = _out_proj(x2, mix, xo, sg, p["w_out"][i], p["final_g"], final=(i == DEPTH - 1))
    return x2.reshape(nb, tseq, D_MODEL), jnp.stack(new_h), jnp.stack(new_r), jnp.stack(new_s)


def kernel(x_prompt, x_sample, mem_prompt, state_hgrn, state_rwkv, state_shift, cache_mem_k, cache_mem_v,
           norm_g, w_in, w_out, mem_norm_g, w_mem_kv, hg_lb, hg_onorm_g, rw_mu, rw_w0, rw_w1, rw_w2,
           rw_a0, rw_a1, rw_a2, rw_v0, rw_v1, rw_v2, rw_kk, rw_ka, rw_rk, rw_lnx_g, rw_lnx_b, final_g):
    n_rwkv = rw_mu.shape[0]
    lbs = jax.nn.softmax(hg_lb.astype(F32), axis=0)
    lbs = jnp.cumsum(lbs, axis=0) - lbs[0]
    par = jnp.stack([rw_kk, rw_ka, rw_rk.reshape(n_rwkv, MIX_W), rw_lnx_g, rw_lnx_b], axis=1)
    par = jnp.concatenate([par, jnp.zeros((n_rwkv, 3, MIX_W), F32)], axis=1)
    par = par.reshape(n_rwkv, 8, RW_PAIRS, LANES).transpose(0, 2, 1, 3)
    p = dict(norm_g=norm_g, w_in=w_in.astype(BF16), w_out=w_out.astype(BF16), lbs=lbs, hg_onorm_g=hg_onorm_g,
             rw_mu=rw_mu, rw_w0=rw_w0, rw_w1=rw_w1.astype(BF16), rw_w2=rw_w2.astype(BF16),
             rw_a0=rw_a0, rw_a1=rw_a1.astype(BF16), rw_a2=rw_a2.astype(BF16),
             rw_v0=rw_v0, rw_v1=rw_v1.astype(BF16), rw_v2=rw_v2.astype(BF16), rw_par=par, final_g=final_g)

    nb, mem_len, _ = mem_prompt.shape
    mk, mv = _memory_kv(mem_prompt.reshape(nb * mem_len, D_MODEL), mem_norm_g, w_mem_kv.astype(BF16))
    mk = mk.reshape(DEPTH, nb, mem_len, CROSS_W)
    mv = mv.reshape(DEPTH, nb, mem_len, CROSS_W)
    z_h = jnp.zeros((state_hgrn.shape[0], nb) + state_hgrn.shape[2:], F32)
    z_r = jnp.zeros((state_rwkv.shape[0], nb) + state_rwkv.shape[2:], F32)
    z_s = jnp.zeros((state_shift.shape[0], nb, D_MODEL), F32)
    y_p, sh_p, sr_p, ss_p = _trunk(x_prompt, mk, mv, z_h, z_r, z_s, p)

    nbs = x_sample.shape[0]
    cmk = cache_mem_k.reshape(DEPTH, nbs, mem_len, CROSS_W)
    cmv = cache_mem_v.reshape(DEPTH, nbs, mem_len, CROSS_W)
    y_s, sh_s, sr_s, ss_s = _trunk(x_sample, cmk, cmv, state_hgrn, state_rwkv, state_shift, p)
    return (y_p, y_s, sh_p, sr_p, ss_p,
            mk.reshape(DEPTH, nb, mem_len, X_HEADS, X_DH), mv.reshape(DEPTH, nb, mem_len, X_HEADS, X_DH),
            sh_s, sr_s, ss_s)
```

```python
import functools
import math

import jax
import jax.numpy as jnp
from jax import lax
from jax.experimental import pallas as pl
from jax.experimental.pallas import tpu as pltpu

F32 = jnp.float32
BF16 = jnp.bfloat16
HIGHEST = lax.Precision.HIGHEST

D_MODEL = 1024
DEPTH = 4
MIX_W = D_MODEL
HG_HEADS = 8
HG_D = MIX_W // HG_HEADS
RW_N = 64
RW_HEADS = MIX_W // RW_N
RW_PAIRS = RW_HEADS // 2
RW_LNX_EPS = 64e-5
MEM_LEN = 256
X_HEADS = 4
X_DH = 128
CROSS_W = X_HEADS * X_DH
BRANCH_W = MIX_W + CROSS_W
IN_COLS = 3 * MIX_W + CROSS_W + BRANCH_W
NORM_EPS = 1e-6
LOG_FLOOR = 1e-30
NEG_BIG = -1e30

LANES = 128
CHUNK = 64
HG_SUB = 16
COL_CHUNK = 512
VMEM_LIMIT = 56 * 1024 * 1024

NT = (((1,), (1,)), ((), ()))
TN = (((0,), (0,)), ((), ()))


def _params(*sem):
    return pltpu.CompilerParams(dimension_semantics=sem, vmem_limit_bytes=VMEM_LIMIT)


def _resident(shape, index_map):
    return pl.BlockSpec(shape, index_map, pipeline_mode=pl.Buffered(1))


def _rms(x, g):
    return x * lax.rsqrt(jnp.mean(x * x, axis=-1, keepdims=True) + NORM_EPS) * g


def _sigmoid(z):
    return 1.0 / (1.0 + jnp.exp(-z))


def _bdot(a, b):
    return jnp.dot(a.astype(BF16), b.astype(BF16), preferred_element_type=F32)


def _hdot(a, b):
    return jnp.dot(a, b, precision=HIGHEST, preferred_element_type=F32)


def _narrow(rows):
    if rows % 16 == 0:
        return lambda a: a.astype(BF16)
    return lambda a: a.astype(BF16).astype(F32)


def _log2(n):
    l = int(math.log2(n))
    assert 1 << l == n, n
    return l


def _memkv_kernel(x_ref, g_ref, w_ref, k_ref, v_ref):
    hb = _rms(x_ref[...], g_ref[...]).astype(BF16)
    kv = jnp.dot(hb, w_ref[...], preferred_element_type=F32)
    k_ref[...] = kv[:, :CROSS_W]
    v_ref[...] = kv[:, CROSS_W:]


def _memory_kv(mem2d, g, w):
    n = mem2d.shape[0]
    tm = min(512, n)
    out = jax.ShapeDtypeStruct((DEPTH, n, CROSS_W), F32)
    return pl.pallas_call(
        _memkv_kernel,
        grid=(DEPTH, n // tm),
        in_specs=[
            pl.BlockSpec((tm, D_MODEL), lambda l, i: (i, 0)),
            pl.BlockSpec((None, 1, D_MODEL), lambda l, i: (l, 0, 0)),
            pl.BlockSpec((None, D_MODEL, 2 * CROSS_W), lambda l, i: (l, 0, 0)),
        ],
        out_specs=[
            pl.BlockSpec((None, tm, CROSS_W), lambda l, i: (l, i, 0)),
            pl.BlockSpec((None, tm, CROSS_W), lambda l, i: (l, i, 0)),
        ],
        out_shape=[out, out],
        compiler_params=_params("arbitrary", "arbitrary"),
        name="memory_kv",
    )(mem2d, g.reshape(DEPTH, 1, D_MODEL), w)


def _hgrn_proj_kernel(x_ref, g_ref, w_ref, lb_ref, q_ref, k_ref, gl_ref, v_ref, xq_ref, sg_ref):
    hb = _rms(x_ref[...], g_ref[...]).astype(BF16)
    for j in range(IN_COLS // COL_CHUNK):
        c0 = j * COL_CHUNK
        z = jnp.dot(hb, w_ref[:, c0:c0 + COL_CHUNK], preferred_element_type=F32)
        if c0 < MIX_W:
            q_ref[:, c0:c0 + COL_CHUNK] = (z * _sigmoid(z)).astype(q_ref.dtype)
        elif c0 < 2 * MIX_W:
            d0 = c0 - MIX_W
            lb = lb_ref[:, d0:d0 + COL_CHUNK]
            f = lb + (1.0 - lb) * _sigmoid(z)
            gl_ref[:, d0:d0 + COL_CHUNK] = jnp.log(jnp.maximum(f, LOG_FLOOR))
            k_ref[:, d0:d0 + COL_CHUNK] = ((1.0 - lb) * _sigmoid(-z)).astype(k_ref.dtype)
        elif c0 < 3 * MIX_W:
            d0 = c0 - 2 * MIX_W
            v_ref[:, d0:d0 + COL_CHUNK] = z.astype(v_ref.dtype)
        elif c0 < 3 * MIX_W + CROSS_W:
            d0 = c0 - 3 * MIX_W
            xq_ref[:, d0:d0 + COL_CHUNK] = z
        else:
            d0 = c0 - 3 * MIX_W - CROSS_W
            sg_ref[:, d0:d0 + COL_CHUNK] = (z * _sigmoid(z)).astype(sg_ref.dtype)


def _hgrn_proj(x2, g, w, lb):
    n = x2.shape[0]
    tm = min(512, n)
    row = lambda width: pl.BlockSpec((tm, width), lambda i: (i, 0))
    sd = lambda width, dt: jax.ShapeDtypeStruct((n, width), dt)
    return pl.pallas_call(
        _hgrn_proj_kernel,
        grid=(n // tm,),
        in_specs=[
            row(D_MODEL),
            _resident((1, D_MODEL), lambda i: (0, 0)),
            _resident((D_MODEL, IN_COLS), lambda i: (0, 0)),
            _resident((1, MIX_W), lambda i: (0, 0)),
        ],
        out_specs=[row(MIX_W), row(MIX_W), row(MIX_W), row(MIX_W), row(CROSS_W), row(BRANCH_W)],
        out_shape=[sd(MIX_W, BF16), sd(MIX_W, BF16), sd(MIX_W, F32), sd(MIX_W, BF16),
                   sd(CROSS_W, F32), sd(BRANCH_W, BF16)],
        compiler_params=_params("arbitrary"),
        name="hgrn_proj",
    )(x2, g.reshape(1, D_MODEL), w, lb.reshape(1, MIX_W))


def _hgrn_scan_kernel(q_ref, k_ref, v_ref, g_ref, s0_ref, og_ref, o_ref, sout_ref, st_ref, *, rb, tseq):
    tc = min(tseq, CHUNK)
    gseq = CHUNK // tc
    sub = min(tc, HG_SUB)
    narrow = _narrow(tc)
    n_chunks = rb // CHUNK
    steps_per_seq = max(1, tseq // rb)
    n_states, width = st_ref.shape[0], st_ref.shape[1]
    i = pl.program_id(1)

    @pl.when(i % steps_per_seq == 0)
    def _():
        def load(s, c):
            for hh in range(width):
                st_ref[s, hh] = s0_ref[s, hh].T
            return c
        lax.fori_loop(0, n_states, load, 0)

    t_i = lax.broadcasted_iota(jnp.int32, (CHUNK, CHUNK), 0)
    s_i = lax.broadcasted_iota(jnp.int32, (CHUNK, CHUNK), 1)
    same = (t_i >> _log2(tc)) == (s_i >> _log2(tc))
    tri = jnp.where(same & (s_i <= t_i), 1.0, 0.0).astype(F32)
    ones_seq = jnp.where(same, 1.0, 0.0).astype(F32)
    row_in_sub = lax.broadcasted_iota(jnp.int32, (CHUNK, 1), 0) & (sub - 1)
    og = og_ref[...]

    heads = range(width)
    lanes = [slice(hh * HG_D, (hh + 1) * HG_D) for hh in heads]

    def chunk(j, carry):
        rows = pl.ds(pl.multiple_of(j * CHUNK, CHUNK), CHUNK)
        q = [q_ref[rows, l].astype(F32) for l in lanes]
        k = [k_ref[rows, l].astype(F32) for l in lanes]
        v = [v_ref[rows, l].astype(F32) for l in lanes]
        g = [g_ref[rows, l] for l in lanes]
        b = [_hdot(tri, x) for x in g]
        b_end = [x[CHUNK - 1:CHUNK, :] for x in b] if gseq == 1 else [_hdot(ones_seq, x) for x in g]
        q_in = [q[h] * jnp.exp(b[h]) for h in heads]
        k_dec = [k[h] * jnp.exp(b_end[h] - b[h]) for h in heads]
        vb = [x.astype(BF16) for x in v]

        o = []
        for gi in range(gseq):
            sl = slice(gi * tc, (gi + 1) * tc)
            sidx = j * gseq + gi if n_states > 1 else 0
            st = [st_ref[sidx, h] for h in heads]
            o.append([lax.dot_general(narrow(q_in[h][sl]), narrow(st[h]), NT, preferred_element_type=F32)
                      for h in heads])
            upd = [lax.dot_general(narrow(v[h][sl]), narrow(k_dec[h][sl]), TN, preferred_element_type=F32)
                   for h in heads]
            for h in heads:
                decay = b_end[h] if gseq == 1 else b_end[h][gi * tc:gi * tc + 1, :]
                st_ref[sidx, h] = st[h] * jnp.exp(decay) + upd[h]
        o = [o[0][h] if gseq == 1 else jnp.concatenate([part[h] for part in o], axis=0) for h in heads]

        if tc > sub:
            parts = [[jnp.zeros((sub, HG_D), F32)] for _ in heads]
            for bi in range(1, tc // sub):
                lo = bi * sub
                att = []
                for h in heads:
                    beta = b[h][lo - 1:lo, :]
                    q_b = (q[h][lo:lo + sub] * jnp.exp(b[h][lo:lo + sub] - beta)).astype(BF16)
                    k_b = (k[h][0:lo] * jnp.exp(beta - b[h][0:lo])).astype(BF16)
                    att.append(lax.dot_general(q_b, k_b, NT, preferred_element_type=F32))
                for h in heads:
                    parts[h].append(jnp.dot(att[h].astype(BF16), vb[h][0:lo], preferred_element_type=F32))
            o = [o[h] + jnp.concatenate(parts[h], axis=0) for h in heads]

        o = [o[h] + jnp.sum(q[h] * k[h], axis=-1, keepdims=True) * v[h] for h in heads]
        for dist in range(1, sub):
            valid = row_in_sub >= dist
            for h in heads:
                k_s = pltpu.roll(k[h], dist, 0)
                v_s = pltpu.roll(v[h], dist, 0)
                b_s = pltpu.roll(b[h], dist, 0)
                e = jnp.exp(jnp.where(valid, b[h] - b_s, NEG_BIG))
                o[h] = o[h] + jnp.sum(q[h] * k_s * e, axis=-1, keepdims=True) * v_s

        for h in heads:
            out = o[h] * lax.rsqrt(jnp.mean(o[h] * o[h], axis=-1, keepdims=True) + NORM_EPS) * og
            o_ref[rows, lanes[h]] = out.astype(o_ref.dtype)
        return carry

    lax.fori_loop(0, n_chunks, chunk, 0)

    @pl.when(i % steps_per_seq == steps_per_seq - 1)
    def _():
        def store(s, c):
            for hh in range(width):
                sout_ref[s, hh] = st_ref[s, hh].T
            return c
        lax.fori_loop(0, n_states, store, 0)


SCAN_WIDTH = 8
STATE_BLOCK_BYTES = 4 * 1024 * 1024


def _scan_geometry(n, tseq):
    rb = min(512, n)
    if tseq < rb:
        max_states = STATE_BLOCK_BYTES // (SCAN_WIDTH * LANES * LANES * 4)
        rb = min(rb, max(CHUNK, max_states * tseq))
    assert rb % CHUNK == 0 and n % rb == 0
    assert (tseq % rb == 0) or (rb % tseq == 0 and CHUNK % tseq == 0)
    n_states = max(1, rb // tseq)
    steps_per_seq = max(1, tseq // rb)
    return rb, n_states, steps_per_seq


def _hgrn_scan(q, k, v, gl, s0, og, tseq):
    n = q.shape[0]
    rb, n_states, steps_per_seq = _scan_geometry(n, tseq)
    tok = pl.BlockSpec((rb, SCAN_WIDTH * HG_D), lambda h, i: (i, h))
    state = pl.BlockSpec((n_states, SCAN_WIDTH, HG_D, HG_D), lambda h, i: (i // steps_per_seq, h, 0, 0))
    return pl.pallas_call(
        functools.partial(_hgrn_scan_kernel, rb=rb, tseq=tseq),
        grid=(HG_HEADS // SCAN_WIDTH, n // rb),
        in_specs=[tok, tok, tok, tok, state, _resident((1, HG_D), lambda h, i: (0, 0))],
        out_specs=[tok, state],
        out_shape=[jax.ShapeDtypeStruct((n, MIX_W), BF16), jax.ShapeDtypeStruct(s0.shape, F32)],
        scratch_shapes=[pltpu.VMEM((n_states, SCAN_WIDTH, HG_D, HG_D), F32)],
        compiler_params=_params("arbitrary", "arbitrary"),
        name="hgrn_scan",
    )(q, k, v, gl, s0, og.reshape(1, HG_D))


def _rwkv_proj_kernel(*refs, tseq, long_seq, has_vres):
    it = iter(refs)
    x_ref, prev_ref, ng_ref, mu_ref, w_ref = (next(it) for _ in range(5))
    w0_ref, w1_ref, w2_ref, a0_ref, a1_ref, a2_ref = (next(it) for _ in range(6))
    if has_vres:
        v0_ref, v1_ref, v2_ref, vf_ref = (next(it) for _ in range(4))
    r_ref, k_ref, v_ref, lw_ref, as_ref, xq_ref, sg_ref, h_ref = (next(it) for _ in range(8))
    carry_ref = next(it) if long_seq else None

    x = x_ref[0] if long_seq else x_ref[...]
    tm = x.shape[0]
    h = _rms(x, ng_ref[...])
    shifted = pltpu.roll(h, 1, 0)
    row = lax.broadcasted_iota(jnp.int32, (tm, 1), 0)
    if long_seq:
        tb = pl.program_id(1)

        @pl.when(tb == 0)
        def _():
            carry_ref[0:1, :] = prev_ref[0]

        h_prev = jnp.where(row == 0, carry_ref[0:1, :], shifted)
        carry_ref[0:1, :] = h[tm - 1:tm, :]

        @pl.when(tb == pl.num_programs(1) - 1)
        def _():
            h_ref[0] = h[tm - 1:tm, :]
    else:
        h_prev = jnp.where((row & (tseq - 1)) == 0, prev_ref[...], shifted)
        h_ref[...] = h

    dx = h_prev - h
    hb = h.astype(BF16)
    xr = (h + dx * mu_ref[0:1, :]).astype(BF16)
    xw = (h + dx * mu_ref[1:2, :]).astype(BF16)
    xk = (h + dx * mu_ref[2:3, :]).astype(BF16)
    xv = (h + dx * mu_ref[3:4, :]).astype(BF16)
    xa = (h + dx * mu_ref[4:5, :]).astype(BF16)

    w_mid = jnp.tanh(jnp.dot(xw, w1_ref[...], preferred_element_type=F32)).astype(BF16)
    a_mid = jnp.dot(xa, a1_ref[...], preferred_element_type=F32).astype(BF16)
    if has_vres:
        v_mid = jnp.dot(xv, v1_ref[...], preferred_element_type=F32).astype(BF16)

    for j in range(MIX_W // COL_CHUNK):
        c0 = j * COL_CHUNK
        cols = slice(c0, c0 + COL_CHUNK)
        r_ref[:, cols] = jnp.dot(xr, w_ref[:, c0:c0 + COL_CHUNK], preferred_element_type=F32)
        k_ref[:, cols] = jnp.dot(xk, w_ref[:, MIX_W + c0:MIX_W + c0 + COL_CHUNK], preferred_element_type=F32)
        vz = jnp.dot(xv, w_ref[:, 2 * MIX_W + c0:2 * MIX_W + c0 + COL_CHUNK], preferred_element_type=F32)
        if has_vres:
            gate = _sigmoid(v0_ref[:, cols] + jnp.dot(v_mid, v2_ref[:, cols], preferred_element_type=F32))
            vz = vz + (vf_ref[:, cols] - vz) * gate
        v_ref[:, cols] = vz
        u = w0_ref[:, cols] + jnp.dot(w_mid, w2_ref[:, cols], preferred_element_type=F32)
        softplus = jnp.maximum(-u, 0.0) + jnp.log(1.0 + jnp.exp(-jnp.abs(u)))
        lw_ref[:, cols] = -jnp.exp(-softplus - 0.5)
        as_ref[:, cols] = _sigmoid(a0_ref[:, cols] + jnp.dot(a_mid, a2_ref[:, cols], preferred_element_type=F32))

    base = 3 * MIX_W
    for j in range(CROSS_W // COL_CHUNK):
        c0 = j * COL_CHUNK
        xq_ref[:, c0:c0 + COL_CHUNK] = jnp.dot(hb, w_ref[:, base + c0:base + c0 + COL_CHUNK],
                                                preferred_element_type=F32)
    base = 3 * MIX_W + CROSS_W
    for j in range(BRANCH_W // COL_CHUNK):
        c0 = j * COL_CHUNK
        z = jnp.dot(hb, w_ref[:, base + c0:base + c0 + COL_CHUNK], preferred_element_type=F32)
        sg_ref[:, c0:c0 + COL_CHUNK] = (z * _sigmoid(z)).astype(sg_ref.dtype)


def _rwkv_proj(x2, shift, tseq, ng, mu, w, w0, w1, w2, a0, a1, a2, vres):
    n = x2.shape[0]
    nb = n // tseq
    tm = min(256, n)
    long_seq = tseq >= tm
    has_vres = vres is not None
    vec = lambda a: a.reshape(1, -1)
    mu8 = jnp.zeros((8, D_MODEL), F32).at[:5].set(mu)
    consts = [vec(ng), mu8, w, vec(w0), w1, w2, vec(a0), a1, a2]
    if has_vres:
        v0, v1, v2, v_first = vres
        consts += [vec(v0), v1, v2]
    if long_seq:
        assert tseq % tm == 0
        steps = tseq // tm
        grid = (nb, steps)
        cmap = lambda b, t: (0, 0)
        row = lambda width: pl.BlockSpec((tm, width), lambda b, t: (b * steps + t, 0))
        x_in = x2.reshape(nb, tseq, D_MODEL)
        x_spec = pl.BlockSpec((1, tm, D_MODEL), lambda b, t: (b, t, 0))
        prev_in = shift.reshape(nb, 1, D_MODEL)
        prev_spec = pl.BlockSpec((1, 1, D_MODEL), lambda b, t: (b, 0, 0))
        h_shape = jax.ShapeDtypeStruct((nb, 1, D_MODEL), F32)
        h_spec = pl.BlockSpec((1, 1, D_MODEL), lambda b, t: (b, 0, 0))
        scratch = [pltpu.VMEM((8, D_MODEL), F32)]
        sem = ("arbitrary", "arbitrary")
    else:
        assert tm % tseq == 0 and tseq & (tseq - 1) == 0
        grid = (n // tm,)
        cmap = lambda i: (0, 0)
        row = lambda width: pl.BlockSpec((tm, width), lambda i: (i, 0))
        x_in, x_spec = x2, row(D_MODEL)
        prev_in = jnp.zeros((nb, tseq, D_MODEL), F32).at[:, 0].set(shift).reshape(n, D_MODEL)
        prev_spec = row(D_MODEL)
        h_shape = jax.ShapeDtypeStruct((n, D_MODEL), F32)
        h_spec = row(D_MODEL)
        scratch = []
        sem = ("arbitrary",)
    in_specs = [x_spec, prev_spec] + [_resident(c.shape, cmap) for c in consts]
    args = [x_in, prev_in] + consts
    if has_vres:
        in_specs.append(row(MIX_W))
        args.append(v_first)
    sd = lambda width, dt: jax.ShapeDtypeStruct((n, width), dt)
    outs = pl.pallas_call(
        functools.partial(_rwkv_proj_kernel, tseq=tseq, long_seq=long_seq, has_vres=has_vres),
        grid=grid,
        in_specs=in_specs,
        out_specs=[row(MIX_W)] * 5 + [row(CROSS_W), row(BRANCH_W), h_spec],
        out_shape=[sd(MIX_W, F32)] * 5 + [sd(CROSS_W, F32), sd(BRANCH_W, BF16), h_shape],
        scratch_shapes=scratch,
        compiler_params=_params(*sem),
        name="rwkv_proj",
    )(*args)
    h_out = outs[7]
    new_shift = h_out.reshape(nb, D_MODEL) if long_seq else h_out.reshape(nb, tseq, D_MODEL)[:, -1]
    return list(outs[:7]) + [new_shift]


def _rwkv_scan_kernel(r_ref, k_ref, v_ref, lw_ref, as_ref, par_ref, s0_ref, y_ref, sout_ref, st_ref, *, rb, tseq):
    tc = min(tseq, CHUNK)
    gseq = CHUNK // tc
    n_chunks = rb // CHUNK
    steps_per_seq = max(1, tseq // rb)
    n_states, width = st_ref.shape[0], st_ref.shape[1]
    n_levels = _log2(tc)
    narrow = _narrow(tc)
    two = 2 * CHUNK
    i = pl.program_id(1)

    @pl.when(i % steps_per_seq == 0)
    def _():
        st_ref[...] = s0_ref[...]

    t_i = lax.broadcasted_iota(jnp.int32, (CHUNK, CHUNK), 0)
    s_i = lax.broadcasted_iota(jnp.int32, (CHUNK, CHUNK), 1)
    same = (t_i >> _log2(tc)) == (s_i >> _log2(tc))
    tri = jnp.where(same & (s_i <= t_i), 1.0, 0.0).astype(F32)
    ones_seq = jnp.where(same, 1.0, 0.0).astype(F32)
    n_i = lax.broadcasted_iota(jnp.int32, (two, two), 0) & (CHUNK - 1)
    m_i = lax.broadcasted_iota(jnp.int32, (two, two), 1) & (CHUNK - 1)
    same2 = (n_i >> _log2(tc)) == (m_i >> _log2(tc))
    strict2 = jnp.where(same2 & (m_i < n_i), 1.0, 0.0).astype(F32)
    incl2 = jnp.where(same2 & (m_i <= n_i), 1.0, 0.0).astype(F32)
    lane = lax.broadcasted_iota(jnp.int32, (1, LANES), 1)
    head0 = jnp.where(lane < RW_N, 1.0, 0.0).astype(F32)
    head1 = 1.0 - head0
    ones_head = jnp.where((lax.broadcasted_iota(jnp.int32, (LANES, LANES), 0) >> _log2(RW_N))
                          == (lax.broadcasted_iota(jnp.int32, (LANES, LANES), 1) >> _log2(RW_N)),
                          1.0, 0.0).astype(F32)
    def stack(a):
        return jnp.concatenate([a * head0, a * head1], axis=0)

    def head_sum(a):
        s0 = jnp.sum(a * head0, axis=-1, keepdims=True)
        s1 = jnp.sum(a * head1, axis=-1, keepdims=True)
        return jnp.where(lane < RW_N, s0, s1)

    pairs = range(width)
    lanes = [slice(pp * LANES, (pp + 1) * LANES) for pp in pairs]

    def chunk(j, carry):
        rows = pl.ds(pl.multiple_of(j * CHUNK, CHUNK), CHUNK)
        r = [r_ref[rows, l] for l in lanes]
        k = [k_ref[rows, l] for l in lanes]
        v = [v_ref[rows, l] for l in lanes]
        lw = [lw_ref[rows, l] for l in lanes]
        a_gate = [as_ref[rows, l] for l in lanes]

        c = [_hdot(tri, x) for x in lw]
        c_end = [x[CHUNK - 1:CHUNK, :] for x in c] if gseq == 1 else [_hdot(ones_seq, x) for x in lw]
        kk = [k[p] * par_ref[p, 0:1, :] for p in pairs]
        kk = [x / jnp.maximum(jnp.sqrt(head_sum(x * x)), 1e-12) for x in kk]
        b_vec = [kk[p] * a_gate[p] for p in pairs]
        k_mod = [k[p] * (1.0 + (a_gate[p] - 1.0) * par_ref[p, 1:2, :]) for p in pairs]
        e_neg = [jnp.exp(-x) for x in c]
        a_t = [-kk[p] * jnp.exp(c[p] - lw[p]) for p in pairs]
        r_t = [r[p] * jnp.exp(c[p]) for p in pairs]

        lhs = [jnp.concatenate([stack(a_t[p]), stack(r_t[p])], axis=0).astype(BF16) for p in pairs]
        rhs = [jnp.concatenate([stack(b_vec[p] * e_neg[p]), stack(k_mod[p] * e_neg[p])], axis=0).astype(BF16)
               for p in pairs]
        big = [lax.dot_general(lhs[p], rhs[p], NT, preferred_element_type=F32) for p in pairs]
        l_ab = [x[0:two, 0:two] * strict2 for x in big]
        l_ak = [x[0:two, two:2 * two] * strict2 for x in big]
        m_rb = [x[two:2 * two, 0:two] * incl2 for x in big]
        m_rk = [x[two:2 * two, two:2 * two] * incl2 for x in big]
        v_s = [stack(x).astype(BF16) for x in v]

        ah0, rh0, states = [], [], []
        for gi in range(gseq):
            sl = slice(gi * tc, (gi + 1) * tc)
            sidx = j * gseq + gi if n_states > 1 else 0
            st = [st_ref[sidx, p] for p in pairs]
            states.append((sidx, st))
            ar = [jnp.concatenate([a_t[p][sl], r_t[p][sl]], axis=0) for p in pairs]
            prod = [lax.dot_general(narrow(ar[p]), narrow(st[p]), NT, preferred_element_type=F32) for p in pairs]
            ah0.append([x[0:tc] for x in prod])
            rh0.append([x[tc:2 * tc] for x in prod])
        ah0 = [ah0[0][p] if gseq == 1 else jnp.concatenate([part[p] for part in ah0], axis=0) for p in pairs]
        rh0 = [rh0[0][p] if gseq == 1 else jnp.concatenate([part[p] for part in rh0], axis=0) for p in pairs]

        x = [stack(ah0[p]) + jnp.dot(l_ak[p].astype(BF16), v_s[p], preferred_element_type=F32) for p in pairs]
        pw = l_ab
        for lev in range(n_levels):
            x = [x[p] + _bdot(pw[p], x[p]) for p in pairs]
            if lev < n_levels - 1:
                pw = [_bdot(m, m) for m in pw]
        y_s = [_bdot(m_rb[p], x[p]) + jnp.dot(m_rk[p].astype(BF16), v_s[p], preferred_element_type=F32)
               for p in pairs]
        y = [rh0[p] + y_s[p][0:CHUNK] + y_s[p][CHUNK:two] for p in pairs]
        u = [m[0:CHUNK] + m[CHUNK:two] for m in x]

        e_end = [jnp.exp(c_end[p] - c[p]) for p in pairs]
        b_hat = [b_vec[p] * e_end[p] for p in pairs]
        k_hat = [k_mod[p] * e_end[p] for p in pairs]
        for gi, (sidx, st) in enumerate(states):
            sl = slice(gi * tc, (gi + 1) * tc)
            uv = [jnp.concatenate([u[p][sl], v[p][sl]], axis=0) for p in pairs]
            bk = [jnp.concatenate([b_hat[p][sl], k_hat[p][sl]], axis=0) for p in pairs]
            upd = [lax.dot_general(narrow(uv[p]), narrow(bk[p]), TN, preferred_element_type=F32) for p in pairs]
            for p in pairs:
                decay = c_end[p] if gseq == 1 else c_end[p][gi * tc:gi * tc + 1, :]
                st_ref[sidx, p] = st[p] * jnp.exp(decay) + upd[p] * ones_head

        mean = [head_sum(m) * (1.0 / RW_N) for m in y]
        d = [y[p] - mean[p] for p in pairs]
        var = [head_sum(m * m) * (1.0 / RW_N) for m in d]
        bonus = [head_sum(r[p] * k_mod[p] * par_ref[p, 2:3, :]) for p in pairs]
        for p in pairs:
            out = d[p] * lax.rsqrt(var[p] + RW_LNX_EPS) * par_ref[p, 3:4, :] + par_ref[p, 4:5, :]
            y_ref[rows, lanes[p]] = (out + bonus[p] * v[p]).astype(y_ref.dtype)
        return carry

    lax.fori_loop(0, n_chunks, chunk, 0)

    @pl.when(i % steps_per_seq == steps_per_seq - 1)
    def _():
        sout_ref[...] = st_ref[...]


def _rwkv_scan(r, k, v, lw, a_gate, par, s0_pairs, tseq):
    n = r.shape[0]
    rb, n_states, steps_per_seq = _scan_geometry(n, tseq)
    tok = pl.BlockSpec((rb, SCAN_WIDTH * LANES), lambda p, i: (i, p))
    state = pl.BlockSpec((n_states, SCAN_WIDTH, LANES, LANES), lambda p, i: (i // steps_per_seq, p, 0, 0))
    return pl.pallas_call(
        functools.partial(_rwkv_scan_kernel, rb=rb, tseq=tseq),
        grid=(RW_PAIRS // SCAN_WIDTH, n // rb),
        in_specs=[tok, tok, tok, tok, tok, pl.BlockSpec((SCAN_WIDTH, 8, LANES), lambda p, i: (p, 0, 0)), state],
        out_specs=[tok, state],
        out_shape=[jax.ShapeDtypeStruct((n, MIX_W), BF16), jax.ShapeDtypeStruct(s0_pairs.shape, F32)],
        scratch_shapes=[pltpu.VMEM((n_states, SCAN_WIDTH, LANES, LANES), F32)],
        compiler_params=_params("arbitrary", "arbitrary"),
        name="rwkv_scan",
    )(r, k, v, lw, a_gate, par, s0_pairs)


def _pairs_from_heads(s):
    nb = s.shape[0]
    s = s.reshape(nb, RW_PAIRS, 2, RW_N, RW_N)
    z = jnp.zeros_like(s[:, :, 0])
    top = jnp.concatenate([s[:, :, 0], z], axis=-1)
    bot = jnp.concatenate([z, s[:, :, 1]], axis=-1)
    return jnp.concatenate([top, bot], axis=-2)


def _heads_from_pairs(p):
    nb = p.shape[0]
    s = jnp.stack([p[:, :, :RW_N, :RW_N], p[:, :, RW_N:, RW_N:]], axis=2)
    return s.reshape(nb, RW_HEADS, RW_N, RW_N)


def _xattn_kernel(q_ref, k_ref, v_ref, o_ref, *, rows_per_seq):
    n_seq = k_ref.shape[0]
    scale = X_DH ** -0.5

    def one(s, carry):
        r0 = pl.multiple_of(s * rows_per_seq, rows_per_seq)
        rows = pl.ds(r0, rows_per_seq)
        q = q_ref[rows, :]
        outs = []
        for hh in range(X_HEADS):
            cols = slice(hh * X_DH, (hh + 1) * X_DH)
            sc = lax.dot_general(q[:, cols].astype(BF16), k_ref[s, :, cols].astype(BF16), NT,
                                 preferred_element_type=F32) * scale
            p = jnp.exp(sc - jnp.max(sc, axis=-1, keepdims=True))
            p = p / jnp.sum(p, axis=-1, keepdims=True)
            outs.append(jnp.dot(p.astype(BF16), v_ref[s, :, cols].astype(BF16), preferred_element_type=F32))
        o_ref[rows, :] = jnp.concatenate(outs, axis=-1)
        return carry

    lax.fori_loop(0, n_seq, one, 0)


def _xattn(xq, mk, mv, tseq):
    n = xq.shape[0]
    rb = min(512, n, 8 * tseq)
    n_seq = max(1, rb // tseq)
    steps_per_seq = max(1, tseq // rb)
    rows_per_seq = rb // n_seq
    row = pl.BlockSpec((rb, CROSS_W), lambda i: (i, 0))
    mem = pl.BlockSpec((n_seq, MEM_LEN, CROSS_W), lambda i: (i // steps_per_seq, 0, 0))
    return pl.pallas_call(
        functools.partial(_xattn_kernel, rows_per_seq=rows_per_seq),
        grid=(n // rb,),
        in_specs=[row, mem, mem],
        out_specs=row,
        out_shape=jax.ShapeDtypeStruct((n, CROSS_W), F32),
        compiler_params=_params("arbitrary"),
        name="mem_xattn",
    )(xq, mk, mv)


def _out_kernel(x_ref, mix_ref, xo_ref, sg_ref, w_ref, fg_ref, o_ref, *, final):
    sg = sg_ref[...].astype(F32)
    left = (mix_ref[...].astype(F32) * sg[:, :MIX_W]).astype(BF16)
    right = (xo_ref[...] * sg[:, MIX_W:]).astype(BF16)
    x = x_ref[...] + jnp.dot(left, w_ref[0:MIX_W, :], preferred_element_type=F32) \
        + jnp.dot(right, w_ref[MIX_W:BRANCH_W, :], preferred_element_type=F32)
    o_ref[...] = _rms(x, fg_ref[...]) if final else x


def _out_proj(x2, mix, xo, sg, w, final_g, final):
    n = x2.shape[0]
    tm = min(512, n)
    row = lambda width: pl.BlockSpec((tm, width), lambda i: (i, 0))
    return pl.pallas_call(
        functools.partial(_out_kernel, final=final),
        grid=(n // tm,),
        in_specs=[row(D_MODEL), row(MIX_W), row(CROSS_W), row(BRANCH_W),
                  _resident((BRANCH_W, D_MODEL), lambda i: (0, 0)),
                  _resident((1, D_MODEL), lambda i: (0, 0))],
        out_specs=row(D_MODEL),
        out_shape=jax.ShapeDtypeStruct((n, D_MODEL), F32),
        compiler_params=_params("arbitrary"),
        name="out_proj",
    )(x2, mix, xo, sg, w, final_g.reshape(1, D_MODEL))


def _trunk(x, mem_k, mem_v, s_hgrn, s_rwkv, s_shift, p):
    nb, tseq, _ = x.shape
    x2 = x.reshape(nb * tseq, D_MODEL)
    new_h, new_r, new_s = [], [], []
    v_first = None
    for i in range(DEPTH):
        j = i // 2
        if i % 2 == 0:
            q, k, gl, v, xq, sg = _hgrn_proj(x2, p["norm_g"][i], p["w_in"][i], p["lbs"][j])
            mix, s_new = _hgrn_scan(q, k, v, gl, s_hgrn[j], p["hg_onorm_g"][j], tseq)
            new_h.append(s_new)
        else:
            vres = None if v_first is None else (p["rw_v0"][j - 1], p["rw_v1"][j - 1], p["rw_v2"][j - 1], v_first)
            r, k, v, lw, a_gate, xq, sg, shift = _rwkv_proj(
                x2, s_shift[j], tseq, p["norm_g"][i], p["rw_mu"][j], p["w_in"][i],
                p["rw_w0"][j], p["rw_w1"][j], p["rw_w2"][j], p["rw_a0"][j], p["rw_a1"][j], p["rw_a2"][j], vres)
            if v_first is None:
                v_first = v
            mix, s_new = _rwkv_scan(r, k, v, lw, a_gate, p["rw_par"][j], _pairs_from_heads(s_rwkv[j]), tseq)
            new_r.append(_heads_from_pairs(s_new))
            new_s.append(shift)
        xo = _xattn(xq, mem_k[i], mem_v[i], tseq)
        x2 = _out_proj(x2, mix, xo, sg, p["w_out"][i], p["final_g"], final=(i == DEPTH - 1))
    return x2.reshape(nb, tseq, D_MODEL), jnp.stack(new_h), jnp.stack(new_r), jnp.stack(new_s)


def kernel(x_prompt, x_sample, mem_prompt, state_hgrn, state_rwkv, state_shift, cache_mem_k, cache_mem_v,
           norm_g, w_in, w_out, mem_norm_g, w_mem_kv, hg_lb, hg_onorm_g, rw_mu, rw_w0, rw_w1, rw_w2,
           rw_a0, rw_a1, rw_a2, rw_v0, rw_v1, rw_v2, rw_kk, rw_ka, rw_rk, rw_lnx_g, rw_lnx_b, final_g):
    n_rwkv = rw_mu.shape[0]
    lbs = jax.nn.softmax(hg_lb.astype(F32), axis=0)
    lbs = jnp.cumsum(lbs, axis=0) - lbs[0]
    par = jnp.stack([rw_kk, rw_ka, rw_rk.reshape(n_rwkv, MIX_W), rw_lnx_g, rw_lnx_b], axis=1)
    par = jnp.concatenate([par, jnp.zeros((n_rwkv, 3, MIX_W), F32)], axis=1)
    par = par.reshape(n_rwkv, 8, RW_PAIRS, LANES).transpose(0, 2, 1, 3)
    p = dict(norm_g=norm_g, w_in=w_in.astype(BF16), w_out=w_out.astype(BF16), lbs=lbs, hg_onorm_g=hg_onorm_g,
             rw_mu=rw_mu, rw_w0=rw_w0, rw_w1=rw_w1.astype(BF16), rw_w2=rw_w2.astype(BF16),
             rw_a0=rw_a0, rw_a1=rw_a1.astype(BF16), rw_a2=rw_a2.astype(BF16),
             rw_v0=rw_v0, rw_v1=rw_v1.astype(BF16), rw_v2=rw_v2.astype(BF16), rw_par=par, final_g=final_g)

    nb, mem_len, _ = mem_prompt.shape
    mk, mv = _memory_kv(mem_prompt.reshape(nb * mem_len, D_MODEL), mem_norm_g, w_mem_kv.astype(BF16))
    mk = mk.reshape(DEPTH, nb, mem_len, CROSS_W)
    mv = mv.reshape(DEPTH, nb, mem_len, CROSS_W)
    z_h = jnp.zeros((state_hgrn.shape[0], nb) + state_hgrn.shape[2:], F32)
    z_r = jnp.zeros((state_rwkv.shape[0], nb) + state_rwkv.shape[2:], F32)
    z_s = jnp.zeros((state_shift.shape[0], nb, D_MODEL), F32)
    y_p, sh_p, sr_p, ss_p = _trunk(x_prompt, mk, mv, z_h, z_r, z_s, p)

    nbs = x_sample.shape[0]
    cmk = cache_mem_k.reshape(DEPTH, nbs, mem_len, CROSS_W)
    cmv = cache_mem_v.reshape(DEPTH, nbs, mem_len, CROSS_W)
    y_s, sh_s, sr_s, ss_s = _trunk(x_sample, cmk, cmv, state_hgrn, state_rwkv, state_shift, p)
    return (y_p, y_s, sh_p, sr_p, ss_p,
            mk.reshape(DEPTH, nb, mem_len, X_HEADS, X_DH), mv.reshape(DEPTH, nb, mem_len, X_HEADS, X_DH),
            sh_s, sr_s, ss_s)
```

```python
import functools
import math

import jax
import jax.numpy as jnp
from jax import lax
from jax.experimental import pallas as pl
from jax.experimental.pallas import tpu as pltpu

F32 = jnp.float32
BF16 = jnp.bfloat16
HIGHEST = lax.Precision.HIGHEST

D_MODEL = 1024
DEPTH = 4
MIX_W = D_MODEL
HG_HEADS = 8
HG_D = MIX_W // HG_HEADS
RW_N = 64
RW_HEADS = MIX_W // RW_N
RW_PAIRS = RW_HEADS // 2
RW_LNX_EPS = 64e-5
MEM_LEN = 256
X_HEADS = 4
X_DH = 128
CROSS_W = X_HEADS * X_DH
BRANCH_W = MIX_W + CROSS_W
IN_COLS = 3 * MIX_W + CROSS_W + BRANCH_W
NORM_EPS = 1e-6
LOG_FLOOR = 1e-30
NEG_BIG = -1e30

LANES = 128
CHUNK = 64
HG_SUB = 16
COL_CHUNK = 512
VMEM_LIMIT = 56 * 1024 * 1024

NT = (((1,), (1,)), ((), ()))
TN = (((0,), (0,)), ((), ()))


def _params(*sem):
    return pltpu.CompilerParams(dimension_semantics=sem, vmem_limit_bytes=VMEM_LIMIT)


def _resident(shape, index_map):
    return pl.BlockSpec(shape, index_map, pipeline_mode=pl.Buffered(1))


def _rms(x, g):
    return x * lax.rsqrt(jnp.mean(x * x, axis=-1, keepdims=True) + NORM_EPS) * g


def _sigmoid(z):
    return 1.0 / (1.0 + jnp.exp(-z))


def _bdot(a, b):
    return jnp.dot(a.astype(BF16), b.astype(BF16), preferred_element_type=F32)


def _hdot(a, b):
    return jnp.dot(a, b, precision=HIGHEST, preferred_element_type=F32)


def _narrow(rows):
    if rows % 16 == 0:
        return lambda a: a.astype(BF16)
    return lambda a: a.astype(BF16).astype(F32)


def _log2(n):
    l = int(math.log2(n))
    assert 1 << l == n, n
    return l


def _memkv_kernel(x_ref, g_ref, w_ref, k_ref, v_ref):
    hb = _rms(x_ref[...], g_ref[...]).astype(BF16)
    kv = jnp.dot(hb, w_ref[...], preferred_element_type=F32)
    k_ref[...] = kv[:, :CROSS_W]
    v_ref[...] = kv[:, CROSS_W:]


def _memory_kv(mem2d, g, w):
    n = mem2d.shape[0]
    tm = min(512, n)
    out = jax.ShapeDtypeStruct((DEPTH, n, CROSS_W), F32)
    return pl.pallas_call(
        _memkv_kernel,
        grid=(DEPTH, n // tm),
        in_specs=[
            pl.BlockSpec((tm, D_MODEL), lambda l, i: (i, 0)),
            pl.BlockSpec((None, 1, D_MODEL), lambda l, i: (l, 0, 0)),
            pl.BlockSpec((None, D_MODEL, 2 * CROSS_W), lambda l, i: (l, 0, 0)),
        ],
        out_specs=[
            pl.BlockSpec((None, tm, CROSS_W), lambda l, i: (l, i, 0)),
            pl.BlockSpec((None, tm, CROSS_W), lambda l, i: (l, i, 0)),
        ],
        out_shape=[out, out],
        compiler_params=_params("arbitrary", "arbitrary"),
        name="memory_kv",
    )(mem2d, g.reshape(DEPTH, 1, D_MODEL), w)


def _hgrn_proj_kernel(x_ref, g_ref, w_ref, lb_ref, q_ref, k_ref, gl_ref, v_ref, xq_ref, sg_ref):
    hb = _rms(x_ref[...], g_ref[...]).astype(BF16)
    for j in range(IN_COLS // COL_CHUNK):
        c0 = j * COL_CHUNK
        z = jnp.dot(hb, w_ref[:, c0:c0 + COL_CHUNK], preferred_element_type=F32)
        if c0 < MIX_W:
            q_ref[:, c0:c0 + COL_CHUNK] = (z * _sigmoid(z)).astype(q_ref.dtype)
        elif c0 < 2 * MIX_W:
            d0 = c0 - MIX_W
            lb = lb_ref[:, d0:d0 + COL_CHUNK]
            f = lb + (1.0 - lb) * _sigmoid(z)
            gl_ref[:, d0:d0 + COL_CHUNK] = jnp.log(jnp.maximum(f, LOG_FLOOR))
            k_ref[:, d0:d0 + COL_CHUNK] = ((1.0 - lb) * _sigmoid(-z)).astype(k_ref.dtype)
        elif c0 < 3 * MIX_W:
            d0 = c0 - 2 * MIX_W
            v_ref[:, d0:d0 + COL_CHUNK] = z.astype(v_ref.dtype)
        elif c0 < 3 * MIX_W + CROSS_W:
            d0 = c0 - 3 * MIX_W
            xq_ref[:, d0:d0 + COL_CHUNK] = z
        else:
            d0 = c0 - 3 * MIX_W - CROSS_W
            sg_ref[:, d0:d0 + COL_CHUNK] = (z * _sigmoid(z)).astype(sg_ref.dtype)


def _hgrn_proj(x2, g, w, lb):
    n = x2.shape[0]
    tm = min(512, n)
    row = lambda width: pl.BlockSpec((tm, width), lambda i: (i, 0))
    sd = lambda width, dt: jax.ShapeDtypeStruct((n, width), dt)
    return pl.pallas_call(
        _hgrn_proj_kernel,
        grid=(n // tm,),
        in_specs=[
            row(D_MODEL),
            _resident((1, D_MODEL), lambda i: (0, 0)),
            _resident((D_MODEL, IN_COLS), lambda i: (0, 0)),
            _resident((1, MIX_W), lambda i: (0, 0)),
        ],
        out_specs=[row(MIX_W), row(MIX_W), row(MIX_W), row(MIX_W), row(CROSS_W), row(BRANCH_W)],
        out_shape=[sd(MIX_W, BF16), sd(MIX_W, BF16), sd(MIX_W, F32), sd(MIX_W, BF16),
                   sd(CROSS_W, F32), sd(BRANCH_W, BF16)],
        compiler_params=_params("arbitrary"),
        name="hgrn_proj",
    )(x2, g.reshape(1, D_MODEL), w, lb.reshape(1, MIX_W))


def _hgrn_scan_kernel(q_ref, k_ref, v_ref, g_ref, s0_ref, og_ref, *rest, rb, tseq, aliased):
    o_ref, sout_ref, st_ref = rest[1:] if aliased else rest
    tc = min(tseq, CHUNK)
    gseq = CHUNK // tc
    sub = min(tc, HG_SUB)
    narrow = _narrow(tc)
    n_chunks = rb // CHUNK
    steps_per_seq = max(1, tseq // rb)
    n_states, width = st_ref.shape[0], st_ref.shape[1]
    i = pl.program_id(1)

    @pl.when(i % steps_per_seq == 0)
    def _():
        def load(s, c):
            for hh in range(width):
                st_ref[s, hh] = s0_ref[s, hh].T
            return c
        lax.fori_loop(0, n_states, load, 0)

    t_i = lax.broadcasted_iota(jnp.int32, (CHUNK, CHUNK), 0)
    s_i = lax.broadcasted_iota(jnp.int32, (CHUNK, CHUNK), 1)
    same = (t_i >> _log2(tc)) == (s_i >> _log2(tc))
    tri = jnp.where(same & (s_i <= t_i), 1.0, 0.0).astype(F32)
    ones_seq = jnp.where(same, 1.0, 0.0).astype(F32)
    row_in_sub = lax.broadcasted_iota(jnp.int32, (CHUNK, 1), 0) & (sub - 1)
    og = og_ref[...]

    heads = range(width)
    lanes = [slice(hh * HG_D, (hh + 1) * HG_D) for hh in heads]

    def chunk(j, carry):
        rows = pl.ds(pl.multiple_of(j * CHUNK, CHUNK), CHUNK)
        q = [q_ref[rows, l].astype(F32) for l in lanes]
        k = [k_ref[rows, l].astype(F32) for l in lanes]
        v = [v_ref[rows, l].astype(F32) for l in lanes]
        g = [g_ref[rows, l] for l in lanes]
        b = [_hdot(tri, x) for x in g]
        b_end = [x[CHUNK - 1:CHUNK, :] for x in b] if gseq == 1 else [_hdot(ones_seq, x) for x in g]
        q_in = [q[h] * jnp.exp(b[h]) for h in heads]
        k_dec = [k[h] * jnp.exp(b_end[h] - b[h]) for h in heads]
        vb = [x.astype(BF16) for x in v]

        o = []
        for gi in range(gseq):
            sl = slice(gi * tc, (gi + 1) * tc)
            sidx = j * gseq + gi if n_states > 1 else 0
            st = [st_ref[sidx, h] for h in heads]
            o.append([lax.dot_general(narrow(q_in[h][sl]), narrow(st[h]), NT, preferred_element_type=F32)
                      for h in heads])
            upd = [lax.dot_general(narrow(v[h][sl]), narrow(k_dec[h][sl]), TN, preferred_element_type=F32)
                   for h in heads]
            for h in heads:
                decay = b_end[h] if gseq == 1 else b_end[h][gi * tc:gi * tc + 1, :]
                st_ref[sidx, h] = st[h] * jnp.exp(decay) + upd[h]
        o = [o[0][h] if gseq == 1 else jnp.concatenate([part[h] for part in o], axis=0) for h in heads]

        if tc > sub:
            parts = [[jnp.zeros((sub, HG_D), F32)] for _ in heads]
            for bi in range(1, tc // sub):
                lo = bi * sub
                att = []
                for h in heads:
                    beta = b[h][lo - 1:lo, :]
                    q_b = (q[h][lo:lo + sub] * jnp.exp(b[h][lo:lo + sub] - beta)).astype(BF16)
                    k_b = (k[h][0:lo] * jnp.exp(beta - b[h][0:lo])).astype(BF16)
                    att.append(lax.dot_general(q_b, k_b, NT, preferred_element_type=F32))
                for h in heads:
                    parts[h].append(jnp.dot(att[h].astype(BF16), vb[h][0:lo], preferred_element_type=F32))
            o = [o[h] + jnp.concatenate(parts[h], axis=0) for h in heads]

        o = [o[h] + jnp.sum(q[h] * k[h], axis=-1, keepdims=True) * v[h] for h in heads]
        for dist in range(1, sub):
            valid = row_in_sub >= dist
            for h in heads:
                k_s = pltpu.roll(k[h], dist, 0)
                v_s = pltpu.roll(v[h], dist, 0)
                b_s = pltpu.roll(b[h], dist, 0)
                e = jnp.exp(jnp.where(valid, b[h] - b_s, NEG_BIG))
                o[h] = o[h] + jnp.sum(q[h] * k_s * e, axis=-1, keepdims=True) * v_s

        for h in heads:
            out = o[h] * lax.rsqrt(jnp.mean(o[h] * o[h], axis=-1, keepdims=True) + NORM_EPS) * og
            o_ref[rows, lanes[h]] = out.astype(o_ref.dtype)
        return carry

    lax.fori_loop(0, n_chunks, chunk, 0)

    @pl.when(i % steps_per_seq == steps_per_seq - 1)
    def _():
        def store(s, c):
            for hh in range(width):
                sout_ref[s, hh] = st_ref[s, hh].T
            return c
        lax.fori_loop(0, n_states, store, 0)


SCAN_WIDTH = 8
STATE_BLOCK_BYTES = 4 * 1024 * 1024


def _scan_geometry(n, tseq):
    rb = min(512, n)
    if tseq < rb:
        max_states = STATE_BLOCK_BYTES // (SCAN_WIDTH * LANES * LANES * 4)
        rb = min(rb, max(CHUNK, max_states * tseq))
    assert rb % CHUNK == 0 and n % rb == 0
    assert (tseq % rb == 0) or (rb % tseq == 0 and CHUNK % tseq == 0)
    n_states = max(1, rb // tseq)
    steps_per_seq = max(1, tseq // rb)
    return rb, n_states, steps_per_seq


def _layer_state_call(kernel_fn, name, heads, state_tail, grid0, tok_width, tok_inputs, consts, const_specs,
                      s_all, layer, s_new_all, tseq):
    n = tok_inputs[0].shape[0]
    rb, n_states, steps_per_seq = _scan_geometry(n, tseq)
    tok = pl.BlockSpec((rb, tok_width), lambda h, i: (i, h))
    state = pl.BlockSpec((None, n_states, heads) + state_tail,
                         lambda h, i: (layer, i // steps_per_seq, h) + (0,) * len(state_tail))
    in_specs = [tok] * len(tok_inputs) + [state] + const_specs
    args = list(tok_inputs) + [s_all] + consts
    aliases = {}
    if s_new_all is not None:
        in_specs.append(pl.BlockSpec(memory_space=pl.ANY))
        args.append(s_new_all)
        aliases = {len(args) - 1: 1}
    return pl.pallas_call(
        functools.partial(kernel_fn, rb=rb, tseq=tseq, aliased=s_new_all is not None),
        grid=(grid0, n // rb),
        in_specs=in_specs,
        out_specs=[tok, state],
        out_shape=[jax.ShapeDtypeStruct((n, MIX_W), BF16), jax.ShapeDtypeStruct(s_all.shape, F32)],
        scratch_shapes=[pltpu.VMEM((n_states, SCAN_WIDTH, LANES, LANES), F32)],
        input_output_aliases=aliases,
        compiler_params=_params("arbitrary", "arbitrary"),
        name=name,
    )(*args)


def _hgrn_scan(q, k, v, gl, s_all, layer, s_new_all, og, tseq):
    return _layer_state_call(
        _hgrn_scan_kernel, "hgrn_scan", SCAN_WIDTH, (HG_D, HG_D), HG_HEADS // SCAN_WIDTH, SCAN_WIDTH * HG_D,
        [q, k, v, gl], [og.reshape(1, HG_D)], [_resident((1, HG_D), lambda h, i: (0, 0))],
        s_all, layer, s_new_all, tseq)


def _rwkv_proj_kernel(*refs, tseq, long_seq, has_vres):
    it = iter(refs)
    x_ref, prev_ref, ng_ref, mu_ref, w_ref = (next(it) for _ in range(5))
    w0_ref, w1_ref, w2_ref, a0_ref, a1_ref, a2_ref = (next(it) for _ in range(6))
    if has_vres:
        v0_ref, v1_ref, v2_ref, vf_ref = (next(it) for _ in range(4))
    r_ref, k_ref, v_ref, lw_ref, as_ref, xq_ref, sg_ref, h_ref = (next(it) for _ in range(8))
    carry_ref = next(it) if long_seq else None

    x = x_ref[0] if long_seq else x_ref[...]
    tm = x.shape[0]
    h = _rms(x, ng_ref[...])
    shifted = pltpu.roll(h, 1, 0)
    row = lax.broadcasted_iota(jnp.int32, (tm, 1), 0)
    if long_seq:
        tb = pl.program_id(1)

        @pl.when(tb == 0)
        def _():
            carry_ref[0:1, :] = prev_ref[0]

        h_prev = jnp.where(row == 0, carry_ref[0:1, :], shifted)
        carry_ref[0:1, :] = h[tm - 1:tm, :]

        @pl.when(tb == pl.num_programs(1) - 1)
        def _():
            h_ref[0] = h[tm - 1:tm, :]
    else:
        h_prev = jnp.where((row & (tseq - 1)) == 0, prev_ref[...], shifted)
        h_ref[...] = h

    dx = h_prev - h
    hb = h.astype(BF16)
    xr = (h + dx * mu_ref[0:1, :]).astype(BF16)
    xw = (h + dx * mu_ref[1:2, :]).astype(BF16)
    xk = (h + dx * mu_ref[2:3, :]).astype(BF16)
    xv = (h + dx * mu_ref[3:4, :]).astype(BF16)
    xa = (h + dx * mu_ref[4:5, :]).astype(BF16)

    w_mid = jnp.tanh(jnp.dot(xw, w1_ref[...], preferred_element_type=F32)).astype(BF16)
    a_mid = jnp.dot(xa, a1_ref[...], preferred_element_type=F32).astype(BF16)
    if has_vres:
        v_mid = jnp.dot(xv, v1_ref[...], preferred_element_type=F32).astype(BF16)

    for j in range(MIX_W // COL_CHUNK):
        c0 = j * COL_CHUNK
        cols = slice(c0, c0 + COL_CHUNK)
        r_ref[:, cols] = jnp.dot(xr, w_ref[:, c0:c0 + COL_CHUNK], preferred_element_type=F32)
        k_ref[:, cols] = jnp.dot(xk, w_ref[:, MIX_W + c0:MIX_W + c0 + COL_CHUNK], preferred_element_type=F32)
        vz = jnp.dot(xv, w_ref[:, 2 * MIX_W + c0:2 * MIX_W + c0 + COL_CHUNK], preferred_element_type=F32)
        if has_vres:
            gate = _sigmoid(v0_ref[:, cols] + jnp.dot(v_mid, v2_ref[:, cols], preferred_element_type=F32))
            vz = vz + (vf_ref[:, cols] - vz) * gate
        v_ref[:, cols] = vz
        u = w0_ref[:, cols] + jnp.dot(w_mid, w2_ref[:, cols], preferred_element_type=F32)
        softplus = jnp.maximum(-u, 0.0) + jnp.log(1.0 + jnp.exp(-jnp.abs(u)))
        lw_ref[:, cols] = -jnp.exp(-softplus - 0.5)
        as_ref[:, cols] = _sigmoid(a0_ref[:, cols] + jnp.dot(a_mid, a2_ref[:, cols], preferred_element_type=F32))

    base = 3 * MIX_W
    for j in range(CROSS_W // COL_CHUNK):
        c0 = j * COL_CHUNK
        xq_ref[:, c0:c0 + COL_CHUNK] = jnp.dot(hb, w_ref[:, base + c0:base + c0 + COL_CHUNK],
                                                preferred_element_type=F32)
    base = 3 * MIX_W + CROSS_W
    for j in range(BRANCH_W // COL_CHUNK):
        c0 = j * COL_CHUNK
        z = jnp.dot(hb, w_ref[:, base + c0:base + c0 + COL_CHUNK], preferred_element_type=F32)
        sg_ref[:, c0:c0 + COL_CHUNK] = (z * _sigmoid(z)).astype(sg_ref.dtype)


def _rwkv_proj(x2, shift, tseq, ng, mu, w, w0, w1, w2, a0, a1, a2, vres):
    n = x2.shape[0]
    nb = n // tseq
    tm = min(256, n)
    long_seq = tseq >= tm
    has_vres = vres is not None
    vec = lambda a: a.reshape(1, -1)
    mu8 = jnp.zeros((8, D_MODEL), F32).at[:5].set(mu)
    consts = [vec(ng), mu8, w, vec(w0), w1, w2, vec(a0), a1, a2]
    if has_vres:
        v0, v1, v2, v_first = vres
        consts += [vec(v0), v1, v2]
    if long_seq:
        assert tseq % tm == 0
        steps = tseq // tm
        grid = (nb, steps)
        cmap = lambda b, t: (0, 0)
        row = lambda width: pl.BlockSpec((tm, width), lambda b, t: (b * steps + t, 0))
        x_in = x2.reshape(nb, tseq, D_MODEL)
        x_spec = pl.BlockSpec((1, tm, D_MODEL), lambda b, t: (b, t, 0))
        prev_in = shift.reshape(nb, 1, D_MODEL)
        prev_spec = pl.BlockSpec((1, 1, D_MODEL), lambda b, t: (b, 0, 0))
        h_shape = jax.ShapeDtypeStruct((nb, 1, D_MODEL), F32)
        h_spec = pl.BlockSpec((1, 1, D_MODEL), lambda b, t: (b, 0, 0))
        scratch = [pltpu.VMEM((8, D_MODEL), F32)]
        sem = ("arbitrary", "arbitrary")
    else:
        assert tm % tseq == 0 and tseq & (tseq - 1) == 0
        grid = (n // tm,)
        cmap = lambda i: (0, 0)
        row = lambda width: pl.BlockSpec((tm, width), lambda i: (i, 0))
        x_in, x_spec = x2, row(D_MODEL)
        prev_in = jnp.zeros((nb, tseq, D_MODEL), F32).at[:, 0].set(shift).reshape(n, D_MODEL)
        prev_spec = row(D_MODEL)
        h_shape = jax.ShapeDtypeStruct((n, D_MODEL), F32)
        h_spec = row(D_MODEL)
        scratch = []
        sem = ("arbitrary",)
    in_specs = [x_spec, prev_spec] + [_resident(c.shape, cmap) for c in consts]
    args = [x_in, prev_in] + consts
    if has_vres:
        in_specs.append(row(MIX_W))
        args.append(v_first)
    sd = lambda width, dt: jax.ShapeDtypeStruct((n, width), dt)
    outs = pl.pallas_call(
        functools.partial(_rwkv_proj_kernel, tseq=tseq, long_seq=long_seq, has_vres=has_vres),
        grid=grid,
        in_specs=in_specs,
        out_specs=[row(MIX_W)] * 5 + [row(CROSS_W), row(BRANCH_W), h_spec],
        out_shape=[sd(MIX_W, F32)] * 5 + [sd(CROSS_W, F32), sd(BRANCH_W, BF16), h_shape],
        scratch_shapes=scratch,
        compiler_params=_params(*sem),
        name="rwkv_proj",
    )(*args)
    h_out = outs[7]
    new_shift = h_out.reshape(nb, D_MODEL) if long_seq else h_out.reshape(nb, tseq, D_MODEL)[:, -1]
    return list(outs[:7]) + [new_shift]


def _rwkv_scan_kernel(r_ref, k_ref, v_ref, lw_ref, as_ref, s0_ref, par_ref, *rest, rb, tseq, aliased):
    y_ref, sout_ref, st_ref = rest[1:] if aliased else rest
    tc = min(tseq, CHUNK)
    gseq = CHUNK // tc
    n_chunks = rb // CHUNK
    steps_per_seq = max(1, tseq // rb)
    n_states, width = st_ref.shape[0], st_ref.shape[1]
    n_levels = _log2(tc)
    narrow = _narrow(tc)
    two = 2 * CHUNK
    i = pl.program_id(1)

    pairs = range(width)
    lo, hi = slice(0, RW_N), slice(RW_N, LANES)

    @pl.when(i % steps_per_seq == 0)
    def _():
        zero = jnp.zeros((RW_N, RW_N), F32)

        def load(s, c):
            for p in pairs:
                st_ref[s, p, lo, lo] = s0_ref[s, 2 * p]
                st_ref[s, p, lo, hi] = zero
                st_ref[s, p, hi, lo] = zero
                st_ref[s, p, hi, hi] = s0_ref[s, 2 * p + 1]
            return c
        lax.fori_loop(0, n_states, load, 0)

    t_i = lax.broadcasted_iota(jnp.int32, (CHUNK, CHUNK), 0)
    s_i = lax.broadcasted_iota(jnp.int32, (CHUNK, CHUNK), 1)
    same = (t_i >> _log2(tc)) == (s_i >> _log2(tc))
    tri = jnp.where(same & (s_i <= t_i), 1.0, 0.0).astype(F32)
    ones_seq = jnp.where(same, 1.0, 0.0).astype(F32)
    n_i =lax.broadcasted_iota(jnp.int32, (two, two), 0) & (CHUNK - 1)
    m_i = lax.broadcasted_iota(jnp.int32, (two, two), 1) & (CHUNK - 1)
    same2 = (n_i >> _log2(tc)) == (m_i >> _log2(tc))
    strict2 = jnp.where(same2 & (m_i < n_i), 1.0, 0.0).astype(F32)
    incl2 = jnp.where(same2 & (m_i <= n_i), 1.0, 0.0).astype(F32)
    lane = lax.broadcasted_iota(jnp.int32, (1, LANES), 1)
    head0 = jnp.where(lane < RW_N, 1.0, 0.0).astype(F32)
    head1 = 1.0 - head0
    ones_head = jnp.where((lax.broadcasted_iota(jnp.int32, (LANES, LANES), 0) >> _log2(RW_N))
                          == (lax.broadcasted_iota(jnp.int32, (LANES, LANES), 1) >> _log2(RW_N)),
                          1.0, 0.0).astype(F32)
    def stack(a):
        return jnp.concatenate([a * head0, a * head1], axis=0)

    def head_sum(a):
        s0 = jnp.sum(a * head0, axis=-1, keepdims=True)
        s1 = jnp.sum(a * head1, axis=-1, keepdims=True)
        return jnp.where(lane < RW_N, s0, s1)

    lanes = [slice(pp * LANES, (pp + 1) * LANES) for pp in pairs]

    def chunk(j, carry):
        rows = pl.ds(pl.multiple_of(j * CHUNK, CHUNK), CHUNK)
        r = [r_ref[rows, l] for l in lanes]
        k = [k_ref[rows, l] for l in lanes]
        v = [v_ref[rows, l] for l in lanes]
        lw = [lw_ref[rows, l] for l in lanes]
        a_gate = [as_ref[rows, l] for l in lanes]

        c = [_hdot(tri, x) for x in lw]
        c_end = [x[CHUNK - 1:CHUNK, :] for x in c] if gseq == 1 else [_hdot(ones_seq, x) for x in lw]
        kk = [k[p] * par_ref[p, 0:1, :] for p in pairs]
        kk = [x / jnp.maximum(jnp.sqrt(head_sum(x * x)), 1e-12) for x in kk]
        b_vec = [kk[p] * a_gate[p] for p in pairs]
        k_mod = [k[p] * (1.0 + (a_gate[p] - 1.0) * par_ref[p, 1:2, :]) for p in pairs]
        e_neg = [jnp.exp(-x) for x in c]
        a_t = [-kk[p] * jnp.exp(c[p] - lw[p]) for p in pairs]
        r_t = [r[p] * jnp.exp(c[p]) for p in pairs]

        lhs = [jnp.concatenate([stack(a_t[p]), stack(r_t[p])], axis=0).astype(BF16) for p in pairs]
        rhs = [jnp.concatenate([stack(b_vec[p] * e_neg[p]), stack(k_mod[p] * e_neg[p])], axis=0).astype(BF16)
               for p in pairs]
        big = [lax.dot_general(lhs[p], rhs[p], NT, preferred_element_type=F32) for p in pairs]
        l_ab = [x[0:two, 0:two] * strict2 for x in big]
        l_ak = [x[0:two, two:2 * two] * strict2 for x in big]
        m_rb = [x[two:2 * two, 0:two] * incl2 for x in big]
        m_rk = [x[two:2 * two, two:2 * two] * incl2 for x in big]
        v_s = [stack(x).astype(BF16) for x in v]

        ah0, rh0, states = [], [], []
        for gi in range(gseq):
            sl = slice(gi * tc, (gi + 1) * tc)
            sidx = j * gseq + gi if n_states > 1 else 0
            st = [st_ref[sidx, p] for p in pairs]
            states.append((sidx, st))
            ar = [jnp.concatenate([a_t[p][sl], r_t[p][sl]], axis=0) for p in pairs]
            prod = [lax.dot_general(narrow(ar[p]), narrow(st[p]), NT, preferred_element_type=F32) for p in pairs]
            ah0.append([x[0:tc] for x in prod])
            rh0.append([x[tc:2 * tc] for x in prod])
        ah0 = [ah0[0][p] if gseq == 1 else jnp.concatenate([part[p] for part in ah0], axis=0) for p in pairs]
        rh0 = [rh0[0][p] if gseq == 1 else jnp.concatenate([part[p] for part in rh0], axis=0) for p in pairs]

        x = [stack(ah0[p]) + jnp.dot(l_ak[p].astype(BF16), v_s[p], preferred_element_type=F32) for p in pairs]
        pw = l_ab
        for lev in range(n_levels):
            x = [x[p] + _bdot(pw[p], x[p]) for p in pairs]
            if lev < n_levels - 1:
                pw = [_bdot(m, m) for m in pw]
        y_s = [_bdot(m_rb[p], x[p]) + jnp.dot(m_rk[p].astype(BF16), v_s[p], preferred_element_type=F32)
               for p in pairs]
        y = [rh0[p] + y_s[p][0:CHUNK] + y_s[p][CHUNK:two] for p in pairs]
        u = [m[0:CHUNK] + m[CHUNK:two] for m in x]

        e_end = [jnp.exp(c_end[p] - c[p]) for p in pairs]
        b_hat = [b_vec[p] * e_end[p] for p in pairs]
        k_hat = [k_mod[p] * e_end[p] for p in pairs]
        for gi, (sidx, st) in enumerate(states):
            sl = slice(gi * tc, (gi + 1) * tc)
            uv = [jnp.concatenate([u[p][sl], v[p][sl]], axis=0) for p in pairs]
            bk = [jnp.concatenate([b_hat[p][sl], k_hat[p][sl]], axis=0) for p in pairs]
            upd = [lax.dot_general(narrow(uv[p]), narrow(bk[p]), TN, preferred_element_type=F32) for p in pairs]
            for p in pairs:
                decay = c_end[p] if gseq == 1 else c_end[p][gi * tc:gi * tc + 1, :]
                st_ref[sidx, p] = st[p] * jnp.exp(decay) + upd[p] * ones_head

        mean = [head_sum(m) * (1.0 / RW_N) for m in y]
        d = [y[p] - mean[p] for p in pairs]
        var = [head_sum(m * m) * (1.0 / RW_N) for m in d]
        bonus = [head_sum(r[p] * k_mod[p] * par_ref[p, 2:3, :]) for p in pairs]
        for p in pairs:
            out = d[p] * lax.rsqrt(var[p] + RW_LNX_EPS) * par_ref[p, 3:4, :] + par_ref[p, 4:5, :]
            y_ref[rows, lanes[p]] = (out + bonus[p] * v[p]).astype(y_ref.dtype)
        return carry

    lax.fori_loop(0, n_chunks, chunk, 0)

    @pl.when(i % steps_per_seq == steps_per_seq - 1)
    def _():
        def store(s, c):
            for p in pairs:
                sout_ref[s, 2 * p] = st_ref[s, p, lo, lo]
                sout_ref[s, 2 * p + 1] = st_ref[s, p, hi, hi]
            return c
        lax.fori_loop(0, n_states, store, 0)


def _rwkv_scan(r, k, v, lw, a_gate, s_all, layer, s_new_all, par, tseq):
    return _layer_state_call(
        _rwkv_scan_kernel, "rwkv_scan", 2 * SCAN_WIDTH, (RW_N, RW_N), RW_PAIRS // SCAN_WIDTH, SCAN_WIDTH * LANES,
        [r, k, v, lw, a_gate], [par], [pl.BlockSpec((SCAN_WIDTH, 8, LANES), lambda p, i: (p, 0, 0))],
        s_all, layer, s_new_all, tseq)


def _xattn_kernel(q_ref, k_ref, v_ref, o_ref, *, rows_per_seq, interleaved):
    n_seq = k_ref.shape[0]
    scale = X_DH ** -0.5
    units = [(s, hh) for s in range(n_seq) for hh in range(X_HEADS)]

    def memory(ref, s, hh):
        if interleaved:
            return ref[s, pl.ds(hh, MEM_LEN, stride=X_HEADS), :].astype(BF16)
        return ref[s, :, hh * X_DH:(hh + 1) * X_DH].astype(BF16)

    q = [q_ref[s * rows_per_seq:(s + 1) * rows_per_seq, hh * X_DH:(hh + 1) * X_DH].astype(BF16) for s, hh in units]
    sc = [lax.dot_general(q[u], memory(k_ref, s, hh), NT, preferred_element_type=F32) * scale
          for u, (s, hh) in enumerate(units)]
    p = [jnp.exp(x - jnp.max(x, axis=-1, keepdims=True)) for x in sc]
    denom = [jnp.sum(x, axis=-1, keepdims=True) for x in p]
    o = [jnp.dot(p[u].astype(BF16), memory(v_ref, s, hh), preferred_element_type=F32) / denom[u]
         for u, (s, hh) in enumerate(units)]
    for s in range(n_seq):
        o_ref[s * rows_per_seq:(s + 1) * rows_per_seq, :] = jnp.concatenate(
            o[s * X_HEADS:(s + 1) * X_HEADS], axis=-1)


def _xattn(xq, mk_all, mv_all, layer, tseq):
    n = xq.shape[0]
    interleaved = mk_all.shape[-1] == X_DH
    rb = min(512, n, 8 * tseq)
    n_seq = max(1, rb // tseq)
    steps_per_seq = max(1, tseq // rb)
    rows_per_seq = rb // n_seq
    row = pl.BlockSpec((rb, CROSS_W), lambda i: (i, 0))
    mem = pl.BlockSpec((None, n_seq) + mk_all.shape[2:], lambda i: (layer, i // steps_per_seq, 0, 0))
    return pl.pallas_call(
        functools.partial(_xattn_kernel, rows_per_seq=rows_per_seq, interleaved=interleaved),
        grid=(n // rb,),
        in_specs=[row, mem, mem],
        out_specs=row,
        out_shape=jax.ShapeDtypeStruct((n, CROSS_W), F32),
        compiler_params=_params("arbitrary"),
        name="mem_xattn",
    )(xq, mk_all, mv_all)


def _out_kernel(x_ref, mix_ref, xo_ref, sg_ref, w_ref, fg_ref, o_ref, *, final):
    sg = sg_ref[...].astype(F32)
    left = (mix_ref[...].astype(F32) * sg[:, :MIX_W]).astype(BF16)
    right = (xo_ref[...] * sg[:, MIX_W:]).astype(BF16)
    x = x_ref[...] + jnp.dot(left, w_ref[0:MIX_W, :], preferred_element_type=F32) \
        + jnp.dot(right, w_ref[MIX_W:BRANCH_W, :], preferred_element_type=F32)
    o_ref[...] = _rms(x, fg_ref[...]) if final else x


def _out_proj(x2, mix, xo, sg, w, final_g, final):
    n = x2.shape[0]
    tm = min(512, n)
    row = lambda width: pl.BlockSpec((tm, width), lambda i: (i, 0))
    return pl.pallas_call(
        functools.partial(_out_kernel, final=final),
        grid=(n // tm,),
        in_specs=[row(D_MODEL), row(MIX_W), row(CROSS_W), row(BRANCH_W),
                  _resident((BRANCH_W, D_MODEL), lambda i: (0, 0)),
                  _resident((1, D_MODEL), lambda i: (0, 0))],
        out_specs=row(D_MODEL),
        out_shape=jax.ShapeDtypeStruct((n, D_MODEL), F32),
        compiler_params=_params("arbitrary"),
        name="out_proj",
    )(x2, mix, xo, sg, w, final_g.reshape(1, D_MODEL))


def _trunk(x, mem_k, mem_v, s_hgrn, s_rwkv, s_shift, p):
    nb, tseq, _ = x.shape
    x2 = x.reshape(nb * tseq, D_MODEL)
    new_h, new_r, new_s = None, None, []
    v_first = None
    for i in range(DEPTH):
        j = i // 2
        if i % 2 == 0:
            q, k, gl, v, xq, sg = _hgrn_proj(x2, p["norm_g"][i], p["w_in"][i], p["lbs"][j])
            mix, new_h = _hgrn_scan(q, k, v, gl, s_hgrn, j, new_h, p["hg_onorm_g"][j], tseq)
        else:
            vres = None if v_first is None else (p["rw_v0"][j - 1], p["rw_v1"][j - 1], p["rw_v2"][j - 1], v_first)
            r, k, v, lw, a_gate, xq, sg, shift = _rwkv_proj(
                x2, s_shift[j], tseq, p["norm_g"][i], p["rw_mu"][j], p["w_in"][i],
                p["rw_w0"][j], p["rw_w1"][j], p["rw_w2"][j], p["rw_a0"][j], p["rw_a1"][j], p["rw_a2"][j], vres)
            if v_first is None:
                v_first = v
            mix, new_r = _rwkv_scan(r, k, v, lw, a_gate, s_rwkv, j, new_r, p["rw_par"][j], tseq)
            new_s.append(shift)
        xo = _xattn(xq, mem_k, mem_v, i, tseq)
        x2 = _out_proj(x2, mix, xo, sg, p["w_out"][i], p["final_g"], final=(i == DEPTH - 1))
    return x2.reshape(nb, tseq, D_MODEL), new_h, new_r, jnp.stack(new_s)


def kernel(x_prompt, x_sample, mem_prompt, state_hgrn, state_rwkv, state_shift, cache_mem_k, cache_mem_v,
           norm_g, w_in, w_out, mem_norm_g, w_mem_kv, hg_lb, hg_onorm_g, rw_mu, rw_w0, rw_w1, rw_w2,
           rw_a0, rw_a1, rw_a2, rw_v0, rw_v1, rw_v2, rw_kk, rw_ka, rw_rk, rw_lnx_g, rw_lnx_b, final_g):
    n_rwkv = rw_mu.shape[0]
    lbs = jax.nn.softmax(hg_lb.astype(F32), axis=0)
    lbs = jnp.cumsum(lbs, axis=0) - lbs[0]
    par = jnp.stack([rw_kk, rw_ka, rw_rk.reshape(n_rwkv, MIX_W), rw_lnx_g, rw_lnx_b], axis=1)
    par = jnp.concatenate([par, jnp.zeros((n_rwkv, 3, MIX_W), F32)], axis=1)
    par = par.reshape(n_rwkv, 8, RW_PAIRS, LANES).transpose(0, 2, 1, 3)
    p = dict(norm_g=norm_g, w_in=w_in.astype(BF16), w_out=w_out.astype(BF16), lbs=lbs, hg_onorm_g=hg_onorm_g,
             rw_mu=rw_mu, rw_w0=rw_w0, rw_w1=rw_w1.astype(BF16), rw_w2=rw_w2.astype(BF16),
             rw_a0=rw_a0, rw_a1=rw_a1.astype(BF16), rw_a2=rw_a2.astype(BF16),
             rw_v0=rw_v0, rw_v1=rw_v1.astype(BF16), rw_v2=rw_v2.astype(BF16), rw_par=par, final_g=final_g)

    nb, mem_len, _ = mem_prompt.shape
    mk, mv = _memory_kv(mem_prompt.reshape(nb * mem_len, D_MODEL), mem_norm_g, w_mem_kv.astype(BF16))
    mk = mk.reshape(DEPTH, nb, mem_len, CROSS_W)
    mv = mv.reshape(DEPTH, nb, mem_len, CROSS_W)
    z_h = jnp.zeros((state_hgrn.shape[0], nb) + state_hgrn.shape[2:], F32)
    z_r = jnp.zeros((state_rwkv.shape[0], nb) + state_rwkv.shape[2:], F32)
    z_s = jnp.zeros((state_shift.shape[0], nb, D_MODEL), F32)
    y_p, sh_p, sr_p, ss_p = _trunk(x_prompt, mk, mv, z_h, z_r, z_s, p)

    nbs = x_sample.shape[0]
    cmk = cache_mem_k.reshape(DEPTH, nbs, mem_len * X_HEADS, X_DH)
    cmv = cache_mem_v.reshape(DEPTH, nbs, mem_len * X_HEADS, X_DH)
    y_s, sh_s, sr_s, ss_s = _trunk(x_sample, cmk, cmv, state_hgrn, state_rwkv, state_shift, p)
    return (y_p, y_s, sh_p, sr_p, ss_p,
            mk.reshape(DEPTH, nb, mem_len, X_HEADS, X_DH), mv.reshape(DEPTH, nb, mem_len, X_HEADS, X_DH),
            sh_s, sr_s, ss_s)
```

```python
import functools
import math

import jax
import jax.numpy as jnp
from jax import lax
from jax.experimental import pallas as pl
from jax.experimental.pallas import tpu as pltpu

F32 = jnp.float32
BF16 = jnp.bfloat16

D_MODEL = 1024
DEPTH = 4
MIX_W = D_MODEL
HG_HEADS = 8
HG_D = MIX_W // HG_HEADS
RW_N = 64
RW_HEADS = MIX_W // RW_N
RW_PAIRS = RW_HEADS // 2
RW_LNX_EPS = 64e-5
MEM_LEN = 256
X_HEADS = 4
X_DH = 128
CROSS_W = X_HEADS * X_DH
BRANCH_W = MIX_W + CROSS_W
IN_COLS = 3 * MIX_W + CROSS_W + BRANCH_W
NORM_EPS = 1e-6
LOG_FLOOR = 1e-30
NEG_BIG = -1e30

LANES = 128
CHUNK = 64
HG_SUB = 8
COL_CHUNK = 512
VMEM_LIMIT = 56 * 1024 * 1024

NT = (((1,), (1,)), ((), ()))
TN = (((0,), (0,)), ((), ()))


def _params(*sem):
    return pltpu.CompilerParams(dimension_semantics=sem, vmem_limit_bytes=VMEM_LIMIT)


def _resident(shape, index_map):
    return pl.BlockSpec(shape, index_map, pipeline_mode=pl.Buffered(1))


def _rms(x, g):
    return x * lax.rsqrt(jnp.mean(x * x, axis=-1, keepdims=True) + NORM_EPS) * g


def _sigmoid(z):
    return 1.0 / (1.0 + jnp.exp(-z))


def _bdot(a, b):
    return jnp.dot(a.astype(BF16), b.astype(BF16), preferred_element_type=F32)


def _sum_rows(sel, x):
    hi = x.astype(BF16)
    rest = x - hi.astype(F32)
    mid = rest.astype(BF16)
    low = (rest - mid.astype(F32)).astype(BF16)
    parts = jnp.dot(sel, jnp.concatenate([hi, mid, low], axis=1), preferred_element_type=F32)
    return parts[:, 0:LANES] + parts[:, LANES:2 * LANES] + parts[:, 2 * LANES:3 * LANES]


def _narrow(rows):
    if rows % 16 == 0:
        return lambda a: a.astype(BF16)
    return lambda a: a.astype(BF16).astype(F32)


def _log2(n):
    l = int(math.log2(n))
    assert 1 << l == n, n
    return l


def _memkv_kernel(x_ref, g_ref, w_ref, k_ref, v_ref):
    hb = _rms(x_ref[...], g_ref[...]).astype(BF16)
    kv = jnp.dot(hb, w_ref[...], preferred_element_type=F32)
    k_ref[...] = kv[:, :CROSS_W]
    v_ref[...] = kv[:, CROSS_W:]


def _memory_kv(mem2d, g, w):
    n = mem2d.shape[0]
    tm = min(512, n)
    out = jax.ShapeDtypeStruct((DEPTH, n, CROSS_W), F32)
    return pl.pallas_call(
        _memkv_kernel,
        grid=(DEPTH, n // tm),
        in_specs=[
            pl.BlockSpec((tm, D_MODEL), lambda l, i: (i, 0)),
            pl.BlockSpec((None, 1, D_MODEL), lambda l, i: (l, 0, 0)),
            pl.BlockSpec((None, D_MODEL, 2 * CROSS_W), lambda l, i: (l, 0, 0)),
        ],
        out_specs=[
            pl.BlockSpec((None, tm, CROSS_W), lambda l, i: (l, i, 0)),
            pl.BlockSpec((None, tm, CROSS_W), lambda l, i: (l, i, 0)),
        ],
        out_shape=[out, out],
        compiler_params=_params("arbitrary", "arbitrary"),
        name="memory_kv",
    )(mem2d, g.reshape(DEPTH, 1, D_MODEL), w)


def _hgrn_proj_kernel(x_ref, g_ref, w_ref, lb_ref, q_ref, k_ref, gl_ref, v_ref, xq_ref, sg_ref):
    hb = _rms(x_ref[...], g_ref[...]).astype(BF16)
    for j in range(IN_COLS // COL_CHUNK):
        c0 = j * COL_CHUNK
        z = jnp.dot(hb, w_ref[:, c0:c0 + COL_CHUNK], preferred_element_type=F32)
        if c0 < MIX_W:
            q_ref[:, c0:c0 + COL_CHUNK] = (z * _sigmoid(z)).astype(q_ref.dtype)
        elif c0 < 2 * MIX_W:
            d0 = c0 - MIX_W
            lb = lb_ref[:, d0:d0 + COL_CHUNK]
            f = lb + (1.0 - lb) * _sigmoid(z)
            gl_ref[:, d0:d0 + COL_CHUNK] = jnp.log(jnp.maximum(f, LOG_FLOOR))
            k_ref[:, d0:d0 + COL_CHUNK] = ((1.0 - lb) * _sigmoid(-z)).astype(k_ref.dtype)
        elif c0 < 3 * MIX_W:
            d0 = c0 - 2 * MIX_W
            v_ref[:, d0:d0 + COL_CHUNK] = z.astype(v_ref.dtype)
        elif c0 < 3 * MIX_W + CROSS_W:
            d0 = c0 - 3 * MIX_W
            xq_ref[:, d0:d0 + COL_CHUNK] = z
        else:
            d0 = c0 - 3 * MIX_W - CROSS_W
            sg_ref[:, d0:d0 + COL_CHUNK] = (z * _sigmoid(z)).astype(sg_ref.dtype)


def _hgrn_proj(x2, g, w, lb):
    n = x2.shape[0]
    tm = min(512, n)
    row = lambda width: pl.BlockSpec((tm, width), lambda i: (i, 0))
    sd = lambda width, dt: jax.ShapeDtypeStruct((n, width), dt)
    return pl.pallas_call(
        _hgrn_proj_kernel,
        grid=(n // tm,),
        in_specs=[
            row(D_MODEL),
            _resident((1, D_MODEL), lambda i: (0, 0)),
            _resident((D_MODEL, IN_COLS), lambda i: (0, 0)),
            _resident((1, MIX_W), lambda i: (0, 0)),
        ],
        out_specs=[row(MIX_W), row(MIX_W), row(MIX_W), row(MIX_W), row(CROSS_W), row(BRANCH_W)],
        out_shape=[sd(MIX_W, BF16), sd(MIX_W, BF16), sd(MIX_W, F32), sd(MIX_W, BF16),
                   sd(CROSS_W, F32), sd(BRANCH_W, BF16)],
        compiler_params=_params("arbitrary"),
        name="hgrn_proj",
    )(x2, g.reshape(1, D_MODEL), w, lb.reshape(1, MIX_W))


def _hgrn_scan_kernel(q_ref, k_ref, v_ref, g_ref, s0_ref, og_ref, *rest, rb, tseq, aliased):
    o_ref, sout_ref, st_ref = rest[1:] if aliased else rest
    tc = min(tseq, CHUNK)
    gseq = CHUNK // tc
    sub = min(tc, HG_SUB)
    narrow = _narrow(tc)
    n_chunks = rb // CHUNK
    steps_per_seq = max(1, tseq // rb)
    groups, n_states, width = st_ref.shape[0], st_ref.shape[1], st_ref.shape[2]
    i = pl.program_id(1)
    units = [(g, hh) for g in range(groups) for hh in range(width)]

    @pl.when(i % steps_per_seq == 0)
    def _():
        def load(s, c):
            for g, hh in units:
                st_ref[g, s, hh] = s0_ref[g, s, hh].T
            return c
        lax.fori_loop(0, n_states, load, 0)

    t_i = lax.broadcasted_iota(jnp.int32, (CHUNK, CHUNK), 0)
    s_i = lax.broadcasted_iota(jnp.int32, (CHUNK, CHUNK), 1)
    same = (t_i >> _log2(tc)) == (s_i >> _log2(tc))
    tri = jnp.where(same & (s_i <= t_i), 1.0, 0.0).astype(BF16)
    ones_seq = jnp.where(same, 1.0, 0.0).astype(BF16)
    row = lax.broadcasted_iota(jnp.int32, (CHUNK, 1), 0)
    row_in_sub = row & (sub - 1)
    og = og_ref[...]
    halves = [sub << lev for lev in range(_log2(tc // sub))]
    late = [(row & half) != 0 for half in halves]
    pair_mask = [jnp.where(((t_i >> _log2(2 * half)) == (s_i >> _log2(2 * half)))
                           & ((t_i & half) != 0) & ((s_i & half) == 0), 1.0, 0.0).astype(F32) for half in halves]

    heads = range(len(units))

    def tokens(ref, rows):
        return [ref[g, rows, hh * HG_D:(hh + 1) * HG_D] for g, hh in units]

    def chunk(j, carry):
        rows = pl.ds(pl.multiple_of(j * CHUNK, CHUNK), CHUNK)
        q = [x.astype(F32) for x in tokens(q_ref, rows)]
        k = [x.astype(F32) for x in tokens(k_ref, rows)]
        v = [x.astype(F32) for x in tokens(v_ref, rows)]
        g = tokens(g_ref, rows)
        b = [_sum_rows(tri, x) for x in g]
        b_end = [x[CHUNK - 1:CHUNK, :] for x in b] if gseq == 1 else [_sum_rows(ones_seq, x) for x in g]
        q_in = [q[h] * jnp.exp(b[h]) for h in heads]
        k_dec = [k[h] * jnp.exp(b_end[h] - b[h]) for h in heads]
        vb = [x.astype(BF16) for x in v]

        o = []
        for gi in range(gseq):
            sl = slice(gi * tc, (gi + 1) * tc)
            sidx = j * gseq + gi if n_states > 1 else 0
            st = [st_ref[g, sidx, hh] for g, hh in units]
            o.append([lax.dot_general(narrow(q_in[h][sl]), narrow(st[h]), NT, preferred_element_type=F32)
                      for h in heads])
            upd = [lax.dot_general(narrow(v[h][sl]), narrow(k_dec[h][sl]), TN, preferred_element_type=F32)
                   for h in heads]
            for h in heads:
                decay = b_end[h] if gseq == 1 else b_end[h][gi * tc:gi * tc + 1, :]
                st_ref[units[h][0], sidx, units[h][1]] = st[h] * jnp.exp(decay) + upd[h]
        o = [o[0][h] if gseq == 1 else jnp.concatenate([part[h] for part in o], axis=0) for h in heads]

        if halves:
            att = [None for _ in heads]
            for half, is_late, mask in zip(halves, late, pair_mask):
                for h in heads:
                    blocks = b[h].reshape(CHUNK // (2 * half), 2 * half, HG_D)
                    beta = jnp.broadcast_to(blocks[:, half - 1:half, :], blocks.shape).reshape(CHUNK, HG_D)
                    q_l = (q[h] * jnp.exp(jnp.where(is_late, b[h] - beta, NEG_BIG))).astype(BF16)
                    k_l = (k[h] * jnp.exp(jnp.where(is_late, NEG_BIG, beta - b[h]))).astype(BF16)
                    part = lax.dot_general(q_l, k_l, NT, preferred_element_type=F32) * mask
                    att[h] = part if att[h] is None else att[h] + part
            o = [o[h] + jnp.dot(att[h].astype(BF16), vb[h], preferred_element_type=F32) for h in heads]

        o = [o[h] + jnp.sum(q[h] * k[h], axis=-1, keepdims=True) * v[h] for h in heads]
        for dist in range(1, sub):
            valid = row_in_sub >= dist
            for h in heads:
                k_s = pltpu.roll(k[h], dist, 0)
                v_s = pltpu.roll(v[h], dist, 0)
                b_s = pltpu.roll(b[h], dist, 0)
                e = jnp.exp(jnp.where(valid, b[h] - b_s, NEG_BIG))
                o[h] = o[h] + jnp.sum(q[h] * k_s * e, axis=-1, keepdims=True) * v_s

        for h in heads:
            out = o[h] * lax.rsqrt(jnp.mean(o[h] * o[h], axis=-1, keepdims=True) + NORM_EPS) * og
            g, hh = units[h]
            o_ref[g, rows, hh * HG_D:(hh + 1) * HG_D] = out.astype(o_ref.dtype)
        return carry

    lax.fori_loop(0, n_chunks, chunk, 0)

    @pl.when(i % steps_per_seq == steps_per_seq - 1)
    def _():
        def store(s, c):
            for g, hh in units:
                sout_ref[g, s, hh] = st_ref[g, s, hh].T
            return c
        lax.fori_loop(0, n_states, store, 0)


SCAN_WIDTH = 8
SCAN_GROUPS = 2
STATE_BLOCK_BYTES = 4 * 1024 * 1024


def _scan_geometry(n, tseq):
    rb = min(512, n)
    if tseq < rb:
        max_states = STATE_BLOCK_BYTES // (SCAN_WIDTH * LANES * LANES * 4)
        rb = min(rb, max(CHUNK, max_states * tseq))
    assert rb % CHUNK == 0 and n % rb == 0
    assert (tseq % rb == 0) or (rb % tseq == 0 and CHUNK % tseq == 0)
    n_states = max(1, rb // tseq)
    steps_per_seq = max(1, tseq // rb)
    return rb, n_states, steps_per_seq


def _layer_state_call(kernel_fn, name, heads, state_tail, grid0, tok_width, tok_inputs, consts, const_specs,
                      s_all, layer, s_new_all, tseq):
    n = tok_inputs[0].shape[0]
    layers, nb = s_all.shape[0], s_all.shape[1]
    groups = SCAN_GROUPS if tseq >= CHUNK else 1
    assert n % groups == 0 and nb % groups == 0
    rb, n_states, steps_per_seq = _scan_geometry(n // groups, tseq)
    grouped = (layers, groups, nb // groups) + s_all.shape[2:]
    tok = pl.BlockSpec((groups, rb, tok_width), lambda h, i: (0, i, h))
    state = pl.BlockSpec((None, groups, n_states, heads) + state_tail,
                         lambda h, i: (layer, 0, i // steps_per_seq, h) + (0,) * len(state_tail))
    in_specs = [tok] * len(tok_inputs) + [state] + const_specs
    args = [t.reshape(groups, n // groups, t.shape[1]) for t in tok_inputs]
    args += [s_all.reshape(grouped)] + consts
    aliases = {}
    if s_new_all is not None:
        in_specs.append(pl.BlockSpec(memory_space=pl.ANY))
        args.append(s_new_all.reshape(grouped))
        aliases = {len(args) - 1: 1}
    out, s_new = pl.pallas_call(
        functools.partial(kernel_fn, rb=rb, tseq=tseq, aliased=s_new_all is not None),
        grid=(grid0, n // groups // rb),
        in_specs=in_specs,
        out_specs=[tok, state],
        out_shape=[jax.ShapeDtypeStruct((groups, n // groups, MIX_W), BF16),
                   jax.ShapeDtypeStruct(grouped, F32)],
        scratch_shapes=[pltpu.VMEM((groups, n_states, SCAN_WIDTH, LANES, LANES), F32)],
        input_output_aliases=aliases,
        compiler_params=_params("arbitrary", "arbitrary"),
        name=name,
    )(*args)
    return out.reshape(n, MIX_W), s_new.reshape(s_all.shape)


def _hgrn_scan(q, k, v, gl, s_all, layer, s_new_all, og, tseq):
    return _layer_state_call(
        _hgrn_scan_kernel, "hgrn_scan", SCAN_WIDTH, (HG_D, HG_D), HG_HEADS // SCAN_WIDTH, SCAN_WIDTH * HG_D,
        [q, k, v, gl], [og.reshape(1, HG_D)], [_resident((1, HG_D), lambda h, i: (0, 0))],
        s_all, layer, s_new_all, tseq)


def _rwkv_proj_kernel(*refs, tseq, long_seq, has_vres):
    it = iter(refs)
    x_ref, prev_ref, ng_ref, mu_ref, w_ref = (next(it) for _ in range(5))
    w0_ref, w1_ref, w2_ref, a0_ref, a1_ref, a2_ref = (next(it) for _ in range(6))
    if has_vres:
        v0_ref, v1_ref, v2_ref, vf_ref = (next(it) for _ in range(4))
    r_ref, k_ref, v_ref, lw_ref, as_ref, xq_ref, sg_ref, h_ref = (next(it) for _ in range(8))
    carry_ref = next(it) if long_seq else None

    x = x_ref[0] if long_seq else x_ref[...]
    tm = x.shape[0]
    h = _rms(x, ng_ref[...])
    shifted = pltpu.roll(h, 1, 0)
    row = lax.broadcasted_iota(jnp.int32, (tm, 1), 0)
    if long_seq:
        tb = pl.program_id(1)

        @pl.when(tb == 0)
        def _():
            carry_ref[0:1, :] = prev_ref[0]

        h_prev = jnp.where(row == 0, carry_ref[0:1, :], shifted)
        carry_ref[0:1, :] = h[tm - 1:tm, :]

        @pl.when(tb == pl.num_programs(1) - 1)
        def _():
            h_ref[0] = h[tm - 1:tm, :]
    else:
        h_prev = jnp.where((row & (tseq - 1)) == 0, prev_ref[...], shifted)
        h_ref[...] = h

    dx = h_prev - h
    hb = h.astype(BF16)
    xr = (h + dx * mu_ref[0:1, :]).astype(BF16)
    xw = (h + dx * mu_ref[1:2, :]).astype(BF16)
    xk = (h + dx * mu_ref[2:3, :]).astype(BF16)
    xv = (h + dx * mu_ref[3:4, :]).astype(BF16)
    xa = (h + dx * mu_ref[4:5, :]).astype(BF16)

    w_mid = jnp.tanh(jnp.dot(xw, w1_ref[...], preferred_element_type=F32)).astype(BF16)
    a_mid = jnp.dot(xa, a1_ref[...], preferred_element_type=F32).astype(BF16)
    if has_vres:
        v_mid = jnp.dot(xv, v1_ref[...], preferred_element_type=F32).astype(BF16)

    for j in range(MIX_W // COL_CHUNK):
        c0 = j * COL_CHUNK
        cols = slice(c0, c0 + COL_CHUNK)
        r_ref[:, cols] = jnp.dot(xr, w_ref[:, c0:c0 + COL_CHUNK], preferred_element_type=F32)
        k_ref[:, cols] = jnp.dot(xk, w_ref[:, MIX_W + c0:MIX_W + c0 + COL_CHUNK], preferred_element_type=F32)
        vz = jnp.dot(xv, w_ref[:, 2 * MIX_W + c0:2 * MIX_W + c0 + COL_CHUNK], preferred_element_type=F32)
        if has_vres:
            gate = _sigmoid(v0_ref[:, cols] + jnp.dot(v_mid, v2_ref[:, cols], preferred_element_type=F32))
            vz = vz + (vf_ref[:, cols] - vz) * gate
        v_ref[:, cols] = vz
        u = w0_ref[:, cols] + jnp.dot(w_mid, w2_ref[:, cols], preferred_element_type=F32)
        softplus = jnp.maximum(-u, 0.0) + jnp.log(1.0 + jnp.exp(-jnp.abs(u)))
        lw_ref[:, cols] = -jnp.exp(-softplus - 0.5)
        as_ref[:, cols] = _sigmoid(a0_ref[:, cols] + jnp.dot(a_mid, a2_ref[:, cols], preferred_element_type=F32))

    base = 3 * MIX_W
    for j in range(CROSS_W // COL_CHUNK):
        c0 = j * COL_CHUNK
        xq_ref[:, c0:c0 + COL_CHUNK] = jnp.dot(hb, w_ref[:, base + c0:base + c0 + COL_CHUNK],
                                                preferred_element_type=F32)
    base = 3 * MIX_W + CROSS_W
    for j in range(BRANCH_W // COL_CHUNK):
        c0 = j * COL_CHUNK
        z = jnp.dot(hb, w_ref[:, base + c0:base + c0 + COL_CHUNK], preferred_element_type=F32)
        sg_ref[:, c0:c0 + COL_CHUNK] = (z * _sigmoid(z)).astype(sg_ref.dtype)


def _rwkv_proj(x2, shift, tseq, ng, mu, w, w0, w1, w2, a0, a1, a2, vres):
    n = x2.shape[0]
    nb = n // tseq
    tm = min(256, n)
    long_seq = tseq >= tm
    has_vres = vres is not None
    vec = lambda a: a.reshape(1, -1)
    mu8 = jnp.zeros((8, D_MODEL), F32).at[:5].set(mu)
    consts = [vec(ng), mu8, w, vec(w0), w1, w2, vec(a0), a1, a2]
    if has_vres:
        v0, v1, v2, v_first = vres
        consts += [vec(v0), v1, v2]
    if long_seq:
        assert tseq % tm == 0
        steps = tseq // tm
        grid = (nb, steps)
        cmap = lambda b, t: (0, 0)
        row = lambda width: pl.BlockSpec((tm, width), lambda b, t: (b * steps + t, 0))
        x_in = x2.reshape(nb, tseq, D_MODEL)
        x_spec = pl.BlockSpec((1, tm, D_MODEL), lambda b, t: (b, t, 0))
        prev_in = shift.reshape(nb, 1, D_MODEL)
        prev_spec = pl.BlockSpec((1, 1, D_MODEL), lambda b, t: (b, 0, 0))
        h_shape = jax.ShapeDtypeStruct((nb, 1, D_MODEL), F32)
        h_spec = pl.BlockSpec((1, 1, D_MODEL), lambda b, t: (b, 0, 0))
        scratch = [pltpu.VMEM((8, D_MODEL), F32)]
        sem = ("arbitrary", "arbitrary")
    else:
        assert tm % tseq == 0 and tseq & (tseq - 1) == 0
        grid = (n // tm,)
        cmap = lambda i: (0, 0)
        row = lambda width: pl.BlockSpec((tm, width), lambda i: (i, 0))
        x_in, x_spec = x2, row(D_MODEL)
        prev_in = jnp.zeros((nb, tseq, D_MODEL), F32).at[:, 0].set(shift).reshape(n, D_MODEL)
        prev_spec = row(D_MODEL)
        h_shape = jax.ShapeDtypeStruct((n, D_MODEL), F32)
        h_spec = row(D_MODEL)
        scratch = []
        sem = ("arbitrary",)
    in_specs = [x_spec, prev_spec] + [_resident(c.shape, cmap) for c in consts]
    args = [x_in, prev_in] + consts
    if has_vres:
        in_specs.append(row(MIX_W))
        args.append(v_first)
    sd = lambda width, dt: jax.ShapeDtypeStruct((n, width), dt)
    outs = pl.pallas_call(
        functools.partial(_rwkv_proj_kernel, tseq=tseq, long_seq=long_seq, has_vres=has_vres),
        grid=grid,
        in_specs=in_specs,
        out_specs=[row(MIX_W)] * 5 + [row(CROSS_W), row(BRANCH_W), h_spec],
        out_shape=[sd(MIX_W, F32)] * 5 + [sd(CROSS_W, F32), sd(BRANCH_W, BF16), h_shape],
        scratch_shapes=scratch,
        compiler_params=_params(*sem),
        name="rwkv_proj",
    )(*args)
    h_out = outs[7]
    new_shift = h_out.reshape(nb, D_MODEL) if long_seq else h_out.reshape(nb, tseq, D_MODEL)[:, -1]
    return list(outs[:7]) + [new_shift]


def _rwkv_scan_kernel(r_ref, k_ref, v_ref, lw_ref, as_ref, s0_ref, par_ref, *rest, rb, tseq, aliased):
    y_ref, sout_ref, st_ref = rest[1:] if aliased else rest
    tc = min(tseq, CHUNK)
    gseq = CHUNK // tc
    n_chunks = rb // CHUNK
    steps_per_seq = max(1, tseq // rb)
    groups, n_states, width = st_ref.shape[0], st_ref.shape[1], st_ref.shape[2]
    n_levels = _log2(tc)
    narrow = _narrow(tc)
    i = pl.program_id(1)

    units = [(g, p) for g in range(groups) for p in range(width)]
    pairs = range(len(units))
    lo, hi = slice(0, RW_N), slice(RW_N, LANES)

    @pl.when(i % steps_per_seq == 0)
    def _():
        zero = jnp.zeros((RW_N, RW_N), F32)

        def load(s, c):
            for g, p in units:
                st_ref[g, s, p, lo, lo] = s0_ref[g, s, 2 * p]
                st_ref[g, s, p, lo, hi] = zero
                st_ref[g, s, p, hi, lo] = zero
                st_ref[g, s, p, hi, hi] = s0_ref[g, s, 2 * p + 1]
            return c
        lax.fori_loop(0, n_states, load, 0)

    t_i = lax.broadcasted_iota(jnp.int32, (CHUNK, CHUNK), 0)
    s_i = lax.broadcasted_iota(jnp.int32, (CHUNK, CHUNK), 1)
    same = (t_i >> _log2(tc)) == (s_i >> _log2(tc))
    tri = jnp.where(same & (s_i <= t_i), 1.0, 0.0).astype(BF16)
    ones_seq = jnp.where(same, 1.0, 0.0).astype(BF16)
    n_i = lax.broadcasted_iota(jnp.int32, (CHUNK, 2 * CHUNK), 0)
    m_i = lax.broadcasted_iota(jnp.int32, (CHUNK, 2 * CHUNK), 1) & (CHUNK - 1)
    same2 = (n_i >> _log2(tc)) == (m_i >> _log2(tc))
    strict2 = jnp.where(same2 & (m_i < n_i), 1.0, 0.0).astype(F32)
    incl2 = jnp.where(same2 & (m_i <= n_i), 1.0, 0.0).astype(F32)
    lane = lax.broadcasted_iota(jnp.int32, (1, LANES), 1)
    head0 = jnp.where(lane < RW_N, 1.0, 0.0).astype(F32)
    head1 = 1.0 - head0
    ones_head = jnp.where((lax.broadcasted_iota(jnp.int32, (LANES, LANES), 0) >> _log2(RW_N))
                          == (lax.broadcasted_iota(jnp.int32, (LANES, LANES), 1) >> _log2(RW_N)),
                          1.0, 0.0).astype(F32)
    def stack(a):
        return jnp.concatenate([a * head0, a * head1], axis=0)

    def head_sum(a):
        s0 = jnp.sum(a * head0, axis=-1, keepdims=True)
        s1 = jnp.sum(a * head1, axis=-1, keepdims=True)
        return jnp.where(lane < RW_N, s0, s1)

    def tokens(ref, rows):
        return [ref[g, rows, p * LANES:(p + 1) * LANES] for g, p in units]

    def par(u, row):
        return par_ref[units[u][1], row:row + 1, :]

    def chunk(j, carry):
        rows = pl.ds(pl.multiple_of(j * CHUNK, CHUNK), CHUNK)
        r, k, v = tokens(r_ref, rows), tokens(k_ref, rows), tokens(v_ref, rows)
        lw, a_gate = tokens(lw_ref, rows), tokens(as_ref, rows)

        c = [_sum_rows(tri, x) for x in lw]
        c_end = [x[CHUNK - 1:CHUNK, :] for x in c] if gseq == 1 else [_sum_rows(ones_seq, x) for x in lw]
        kk = [k[p] * par(p, 0) for p in pairs]
        kk = [x / jnp.maximum(jnp.sqrt(head_sum(x * x)), 1e-12) for x in kk]
        b_vec = [kk[p] * a_gate[p] for p in pairs]
        k_mod = [k[p] * (1.0 + (a_gate[p] - 1.0) * par(p, 1)) for p in pairs]
        e_neg = [jnp.exp(-x) for x in c]
        a_t = [-kk[p] * jnp.exp(c[p] - lw[p]) for p in pairs]
        r_t = [r[p] * jnp.exp(c[p]) for p in pairs]

        lhs = [jnp.concatenate([a_t[p], r_t[p]], axis=0).astype(BF16) for p in pairs]
        rhs = [jnp.concatenate([stack(b_vec[p] * e_neg[p]), stack(k_mod[p] * e_neg[p])], axis=0).astype(BF16)
               for p in pairs]
        big = [lax.dot_general(lhs[p], rhs[p], NT, preferred_element_type=F32) for p in pairs]
        l_ab = [x[0:CHUNK, 0:LANES] * strict2 for x in big]
        l_ak = [x[0:CHUNK, LANES:2 * LANES] * strict2 for x in big]
        m_rb = [x[CHUNK:2 * CHUNK, 0:LANES] * incl2 for x in big]
        m_rk = [x[CHUNK:2 * CHUNK, LANES:2 * LANES] * incl2 for x in big]
        v_s = [stack(x).astype(BF16) for x in v]

        ah0, rh0, states = [], [], []
        for gi in range(gseq):
            sl = slice(gi * tc, (gi + 1) * tc)
            sidx = j * gseq + gi if n_states > 1 else 0
            st = [st_ref[g, sidx, p] for g, p in units]
            states.append((sidx, st))
            ar = [jnp.concatenate([a_t[p][sl], r_t[p][sl]], axis=0) for p in pairs]
            prod = [lax.dot_general(narrow(ar[p]), narrow(st[p]), NT, preferred_element_type=F32) for p in pairs]
            ah0.append([x[0:tc] for x in prod])
            rh0.append([x[tc:2 * tc] for x in prod])
        ah0 = [ah0[0][p] if gseq == 1 else jnp.concatenate([part[p] for part in ah0], axis=0) for p in pairs]
        rh0 = [rh0[0][p] if gseq == 1 else jnp.concatenate([part[p] for part in rh0], axis=0) for p in pairs]

        u = [ah0[p] + jnp.dot(l_ak[p].astype(BF16), v_s[p], preferred_element_type=F32) for p in pairs]
        pw = l_ab
        for lev in range(n_levels):
            u = [u[p] + _bdot(pw[p], stack(u[p])) for p in pairs]
            if lev < n_levels - 1:
                pw = [_bdot(m, stack(m)) for m in pw]
        y = [rh0[p] + jnp.dot(jnp.concatenate([m_rb[p], m_rk[p]], axis=1).astype(BF16),
                              jnp.concatenate([stack(u[p]).astype(BF16), v_s[p]], axis=0),
                              preferred_element_type=F32) for p in pairs]

        e_end = [jnp.exp(c_end[p] - c[p]) for p in pairs]
        b_hat = [b_vec[p] * e_end[p] for p in pairs]
        k_hat = [k_mod[p] * e_end[p] for p in pairs]
        for gi, (sidx, st) in enumerate(states):
            sl = slice(gi * tc, (gi + 1) * tc)
            uv = [jnp.concatenate([u[p][sl], v[p][sl]], axis=0) for p in pairs]
            bk = [jnp.concatenate([b_hat[p][sl], k_hat[p][sl]], axis=0) for p in pairs]
            upd = [lax.dot_general(narrow(uv[p]), narrow(bk[p]), TN, preferred_element_type=F32) for p in pairs]
            for p in pairs:
                decay = c_end[p] if gseq == 1 else c_end[p][gi * tc:gi * tc + 1, :]
                st_ref[units[p][0], sidx, units[p][1]] = st[p] * jnp.exp(decay) + upd[p] * ones_head

        mean = [head_sum(m) * (1.0 / RW_N) for m in y]
        d = [y[p] - mean[p] for p in pairs]
        var = [head_sum(m * m) * (1.0 / RW_N) for m in d]
        bonus = [head_sum(r[p] * k_mod[p] * par(p, 2)) for p in pairs]
        for p in pairs:
            out = d[p] * lax.rsqrt(var[p] + RW_LNX_EPS) * par(p, 3) + par(p, 4)
            g, pp = units[p]
            y_ref[g, rows, pp * LANES:(pp + 1) * LANES] = (out + bonus[p] * v[p]).astype(y_ref.dtype)
        return carry

    lax.fori_loop(0, n_chunks, chunk, 0)

    @pl.when(i % steps_per_seq == steps_per_seq - 1)
    def _():
        def store(s, c):
            for g, p in units:
                sout_ref[g, s, 2 * p] = st_ref[g, s, p, lo, lo]
                sout_ref[g, s, 2 * p + 1] = st_ref[g, s, p, hi, hi]
            return c
        lax.fori_loop(0, n_states, store, 0)


def _rwkv_scan(r, k, v, lw, a_gate, s_all, layer, s_new_all, par, tseq):
    return _layer_state_call(
        _rwkv_scan_kernel, "rwkv_scan", 2 * SCAN_WIDTH, (RW_N, RW_N), RW_PAIRS // SCAN_WIDTH, SCAN_WIDTH * LANES,
        [r, k, v, lw, a_gate], [par], [pl.BlockSpec((SCAN_WIDTH, 8, LANES), lambda p, i: (p, 0, 0))],
        s_all, layer, s_new_all, tseq)


def _xattn_kernel(q_ref, k_ref, v_ref, o_ref, *, rows_per_seq, interleaved):
    n_seq = k_ref.shape[0]
    scale = X_DH ** -0.5
    units = [(s, hh) for s in range(n_seq) for hh in range(X_HEADS)]

    def memory(ref, s, hh):
        if interleaved:
            return ref[s, pl.ds(hh, MEM_LEN, stride=X_HEADS), :].astype(BF16)
        return ref[s, :, hh * X_DH:(hh + 1) * X_DH].astype(BF16)

    q = [q_ref[s * rows_per_seq:(s + 1) * rows_per_seq, hh * X_DH:(hh + 1) * X_DH].astype(BF16) for s, hh in units]
    sc = [lax.dot_general(q[u], memory(k_ref, s, hh), NT, preferred_element_type=F32) * scale
          for u, (s, hh) in enumerate(units)]
    p = [jnp.exp(x - jnp.max(x, axis=-1, keepdims=True)) for x in sc]
    denom = [jnp.sum(x, axis=-1, keepdims=True) for x in p]
    o = [jnp.dot(p[u].astype(BF16), memory(v_ref, s, hh), preferred_element_type=F32) / denom[u]
         for u, (s, hh) in enumerate(units)]
    for s in range(n_seq):
        o_ref[s * rows_per_seq:(s + 1) * rows_per_seq, :] = jnp.concatenate(
            o[s * X_HEADS:(s + 1) * X_HEADS], axis=-1)


def _xattn(xq, mk_all, mv_all, layer, tseq):
    n = xq.shape[0]
    interleaved = mk_all.shape[-1] == X_DH
    rb = min(512, n, 8 * tseq)
    n_seq = max(1, rb // tseq)
    steps_per_seq = max(1, tseq // rb)
    rows_per_seq = rb // n_seq
    row = pl.BlockSpec((rb, CROSS_W), lambda i: (i, 0))
    mem = pl.BlockSpec((None, n_seq) + mk_all.shape[2:], lambda i: (layer, i // steps_per_seq, 0, 0))
    return pl.pallas_call(
        functools.partial(_xattn_kernel, rows_per_seq=rows_per_seq, interleaved=interleaved),
        grid=(n // rb,),
        in_specs=[row, mem, mem],
        out_specs=row,
        out_shape=jax.ShapeDtypeStruct((n, CROSS_W), F32),
        compiler_params=_params("arbitrary"),
        name="mem_xattn",
    )(xq, mk_all, mv_all)


def _out_kernel(x_ref, mix_ref, xo_ref, sg_ref, w_ref, fg_ref, o_ref, *, final):
    sg = sg_ref[...].astype(F32)
    left = (mix_ref[...].astype(F32) * sg[:, :MIX_W]).astype(BF16)
    right = (xo_ref[...] * sg[:, MIX_W:]).astype(BF16)
    x = x_ref[...] + jnp.dot(left, w_ref[0:MIX_W, :], preferred_element_type=F32) \
        + jnp.dot(right, w_ref[MIX_W:BRANCH_W, :], preferred_element_type=F32)
    o_ref[...] = _rms(x, fg_ref[...]) if final else x


def _out_proj(x2, mix, xo, sg, w, final_g, final):
    n = x2.shape[0]
    tm = min(512, n)
    row = lambda width: pl.BlockSpec((tm, width), lambda i: (i, 0))
    return pl.pallas_call(
        functools.partial(_out_kernel, final=final),
        grid=(n // tm,),
        in_specs=[row(D_MODEL), row(MIX_W), row(CROSS_W), row(BRANCH_W),
                  _resident((BRANCH_W, D_MODEL), lambda i: (0, 0)),
                  _resident((1, D_MODEL), lambda i: (0, 0))],
        out_specs=row(D_MODEL),
        out_shape=jax.ShapeDtypeStruct((n, D_MODEL), F32),
        compiler_params=_params("arbitrary"),
        name="out_proj",
    )(x2, mix, xo, sg, w, final_g.reshape(1, D_MODEL))


def _trunk(x, mem_k, mem_v, s_hgrn, s_rwkv, s_shift, p):
    nb, tseq, _ = x.shape
    x2 = x.reshape(nb * tseq, D_MODEL)
    new_h, new_r, new_s = None, None, []
    v_first = None
    for i in range(DEPTH):
        j = i // 2
        if i % 2 == 0:
            q, k, gl, v, xq, sg = _hgrn_proj(x2, p["norm_g"][i], p["w_in"][i], p["lbs"][j])
            mix, new_h = _hgrn_scan(q, k, v, gl, s_hgrn, j, new_h, p["hg_onorm_g"][j], tseq)
        else:
            vres = None if v_first is None else (p["rw_v0"][j - 1], p["rw_v1"][j - 1], p["rw_v2"][j - 1], v_first)
            r, k, v, lw, a_gate, xq, sg, shift = _rwkv_proj(
                x2, s_shift[j], tseq, p["norm_g"][i], p["rw_mu"][j], p["w_in"][i],
                p["rw_w0"][j], p["rw_w1"][j], p["rw_w2"][j], p["rw_a0"][j], p["rw_a1"][j], p["rw_a2"][j], vres)
            if v_first is None:
                v_first = v
            mix, new_r = _rwkv_scan(r, k, v, lw, a_gate, s_rwkv, j, new_r, p["rw_par"][j], tseq)
            new_s.append(shift)
        xo = _xattn(xq, mem_k, mem_v, i, tseq)
        x2 = _out_proj(x2, mix, xo, sg, p["w_out"][i], p["final_g"], final=(i == DEPTH - 1))
    return x2.reshape(nb, tseq, D_MODEL), new_h, new_r, jnp.stack(new_s)


def kernel(x_prompt, x_sample, mem_prompt, state_hgrn, state_rwkv, state_shift, cache_mem_k, cache_mem_v,
           norm_g, w_in, w_out, mem_norm_g, w_mem_kv, hg_lb, hg_onorm_g, rw_mu, rw_w0, rw_w1, rw_w2,
           rw_a0, rw_a1, rw_a2, rw_v0, rw_v1, rw_v2, rw_kk, rw_ka, rw_rk, rw_lnx_g, rw_lnx_b, final_g):
    n_rwkv = rw_mu.shape[0]
    lbs = jax.nn.softmax(hg_lb.astype(F32), axis=0)
    lbs = jnp.cumsum(lbs, axis=0) - lbs[0]
    par = jnp.stack([rw_kk, rw_ka, rw_rk.reshape(n_rwkv, MIX_W), rw_lnx_g, rw_lnx_b], axis=1)
    par = jnp.concatenate([par, jnp.zeros((n_rwkv, 3, MIX_W), F32)], axis=1)
    par = par.reshape(n_rwkv, 8, RW_PAIRS, LANES).transpose(0, 2, 1, 3)
    p = dict(norm_g=norm_g, w_in=w_in.astype(BF16), w_out=w_out.astype(BF16), lbs=lbs, hg_onorm_g=hg_onorm_g,
             rw_mu=rw_mu, rw_w0=rw_w0, rw_w1=rw_w1.astype(BF16), rw_w2=rw_w2.astype(BF16),
             rw_a0=rw_a0, rw_a1=rw_a1.astype(BF16), rw_a2=rw_a2.astype(BF16),
             rw_v0=rw_v0, rw_v1=rw_v1.astype(BF16), rw_v2=rw_v2.astype(BF16), rw_par=par, final_g=final_g)

    nb, mem_len, _ = mem_prompt.shape
    mk, mv = _memory_kv(mem_prompt.reshape(nb * mem_len, D_MODEL), mem_norm_g, w_mem_kv.astype(BF16))
    mk = mk.reshape(DEPTH, nb, mem_len, CROSS_W)
    mv = mv.reshape(DEPTH, nb, mem_len, CROSS_W)
    z_h = jnp.zeros((state_hgrn.shape[0], nb) + state_hgrn.shape[2:], F32)
    z_r = jnp.zeros((state_rwkv.shape[0], nb) + state_rwkv.shape[2:], F32)
    z_s = jnp.zeros((state_shift.shape[0], nb, D_MODEL), F32)
    y_p, sh_p, sr_p, ss_p = _trunk(x_prompt, mk, mv, z_h, z_r, z_s, p)

    nbs = x_sample.shape[0]
    cmk = cache_mem_k.reshape(DEPTH, nbs, mem_len * X_HEADS, X_DH)
    cmv = cache_mem_v.reshape(DEPTH, nbs, mem_len * X_HEADS, X_DH)
    y_s, sh_s, sr_s, ss_s = _trunk(x_sample, cmk, cmv, state_hgrn, state_rwkv, state_shift, p)
    return (y_p, y_s, sh_p, sr_p, ss_p,
            mk.reshape(DEPTH, nb, mem_len, X_HEADS, X_DH), mv.reshape(DEPTH, nb, mem_len, X_HEADS, X_DH),
            sh_s, sr_s, ss_s)
```

```python
import functools
import math

import jax
import jax.numpy as jnp
from jax import lax
from jax.experimental import pallas as pl
from jax.experimental.pallas import tpu as pltpu

F32 = jnp.float32
BF16 = jnp.bfloat16

D_MODEL = 1024
DEPTH = 4
MIX_W = D_MODEL
HG_HEADS = 8
HG_D = MIX_W // HG_HEADS
RW_N = 64
RW_HEADS = MIX_W // RW_N
RW_PAIRS = RW_HEADS // 2
RW_LNX_EPS = 64e-5
MEM_LEN = 256
X_HEADS = 4
X_DH = 128
CROSS_W = X_HEADS * X_DH
BRANCH_W = MIX_W + CROSS_W
IN_COLS = 3 * MIX_W + CROSS_W + BRANCH_W
NORM_EPS = 1e-6
LOG_FLOOR = 1e-30
NEG_BIG = -1e30

LANES = 128
SUBLANES = 8
CHUNK = 64
COL_CHUNK = 512
VMEM_LIMIT = 56 * 1024 * 1024

NT = (((1,), (1,)), ((), ()))
TN = (((0,), (0,)), ((), ()))


def _params(*sem):
    return pltpu.CompilerParams(dimension_semantics=sem, vmem_limit_bytes=VMEM_LIMIT)


def _resident(shape, index_map):
    return pl.BlockSpec(shape, index_map, pipeline_mode=pl.Buffered(1))


def _rms(x, g):
    return x * lax.rsqrt(jnp.mean(x * x, axis=-1, keepdims=True) + NORM_EPS) * g


def _sigmoid(z):
    return 1.0 / (1.0 + jnp.exp(-z))


def _sum_rows(sel, x):
    hi = x.astype(BF16)
    rest = x - hi.astype(F32)
    mid = rest.astype(BF16)
    low = (rest - mid.astype(F32)).astype(BF16)
    parts = jnp.dot(sel, jnp.concatenate([hi, mid, low], axis=1), preferred_element_type=F32)
    return parts[:, 0:LANES] + parts[:, LANES:2 * LANES] + parts[:, 2 * LANES:3 * LANES]


def _narrow(rows):
    if rows % 16 == 0:
        return lambda a: a.astype(BF16)
    return lambda a: a.astype(BF16).astype(F32)


def _log2(n):
    l = int(math.log2(n))
    assert 1 << l == n, n
    return l


def _memkv_kernel(x_ref, g_ref, w_ref, k_ref, v_ref):
    hb = _rms(x_ref[...], g_ref[...]).astype(BF16)
    kv = jnp.dot(hb, w_ref[...], preferred_element_type=F32)
    k_ref[...] = kv[:, :CROSS_W]
    v_ref[...] = kv[:, CROSS_W:]


def _memory_kv(mem2d, g, w):
    n = mem2d.shape[0]
    tm = min(512, n)
    out = jax.ShapeDtypeStruct((DEPTH, n, CROSS_W), F32)
    return pl.pallas_call(
        _memkv_kernel,
        grid=(DEPTH, n // tm),
        in_specs=[
            pl.BlockSpec((tm, D_MODEL), lambda l, i: (i, 0)),
            pl.BlockSpec((None, 1, D_MODEL), lambda l, i: (l, 0, 0)),
            pl.BlockSpec((None, D_MODEL, 2 * CROSS_W), lambda l, i: (l, 0, 0)),
        ],
        out_specs=[
            pl.BlockSpec((None, tm, CROSS_W), lambda l, i: (l, i, 0)),
            pl.BlockSpec((None, tm, CROSS_W), lambda l, i: (l, i, 0)),
        ],
        out_shape=[out, out],
        compiler_params=_params("arbitrary", "arbitrary"),
        name="memory_kv",
    )(mem2d, g.reshape(DEPTH, 1, D_MODEL), w)


def _hgrn_proj_kernel(x_ref, g_ref, w_ref, lb_ref, q_ref, k_ref, gl_ref, v_ref, xq_ref, sg_ref):
    hb = _rms(x_ref[...], g_ref[...]).astype(BF16)
    for j in range(IN_COLS // COL_CHUNK):
        c0 = j * COL_CHUNK
        z = jnp.dot(hb, w_ref[:, c0:c0 + COL_CHUNK], preferred_element_type=F32)
        if c0 < MIX_W:
            q_ref[:, c0:c0 + COL_CHUNK] = (z * _sigmoid(z)).astype(q_ref.dtype)
        elif c0 < 2 * MIX_W:
            d0 = c0 - MIX_W
            lb = lb_ref[:, d0:d0 + COL_CHUNK]
            f = lb + (1.0 - lb) * _sigmoid(z)
            gl_ref[:, d0:d0 + COL_CHUNK] = jnp.log(jnp.maximum(f, LOG_FLOOR))
            k_ref[:, d0:d0 + COL_CHUNK] = ((1.0 - lb) * _sigmoid(-z)).astype(k_ref.dtype)
        elif c0 < 3 * MIX_W:
            d0 = c0 - 2 * MIX_W
            v_ref[:, d0:d0 + COL_CHUNK] = z.astype(v_ref.dtype)
        elif c0 < 3 * MIX_W + CROSS_W:
            d0 = c0 - 3 * MIX_W
            xq_ref[:, d0:d0 + COL_CHUNK] = z
        else:
            d0 = c0 - 3 * MIX_W - CROSS_W
            sg_ref[:, d0:d0 + COL_CHUNK] = (z * _sigmoid(z)).astype(sg_ref.dtype)


def _hgrn_proj(x2, g, w, lb):
    n = x2.shape[0]
    tm = min(512, n)
    row = lambda width: pl.BlockSpec((tm, width), lambda i: (i, 0))
    sd = lambda width, dt: jax.ShapeDtypeStruct((n, width), dt)
    return pl.pallas_call(
        _hgrn_proj_kernel,
        grid=(n // tm,),
        in_specs=[
            row(D_MODEL),
            _resident((1, D_MODEL), lambda i: (0, 0)),
            _resident((D_MODEL, IN_COLS), lambda i: (0, 0)),
            _resident((1, MIX_W), lambda i: (0, 0)),
        ],
        out_specs=[row(MIX_W), row(MIX_W), row(MIX_W), row(MIX_W), row(CROSS_W), row(BRANCH_W)],
        out_shape=[sd(MIX_W, BF16), sd(MIX_W, BF16), sd(MIX_W, F32), sd(MIX_W, BF16),
                   sd(CROSS_W, F32), sd(BRANCH_W, BF16)],
        compiler_params=_params("arbitrary"),
        name="hgrn_proj",
    )(x2, g.reshape(1, D_MODEL), w, lb.reshape(1, MIX_W))


def _hgrn_scan_kernel(q_ref, k_ref, v_ref, g_ref, s0_ref, og_ref, *rest, rb, tseq, aliased):
    o_ref, sout_ref, st_ref = rest[1:] if aliased else rest
    tc = min(tseq, CHUNK)
    gseq = CHUNK // tc
    narrow = _narrow(tc)
    n_chunks = rb // CHUNK
    steps_per_seq = max(1, tseq // rb)
    groups, n_states, width = st_ref.shape[0], st_ref.shape[1], st_ref.shape[2]
    i = pl.program_id(1)
    units = [(g, hh) for g in range(groups) for hh in range(width)]

    @pl.when(i % steps_per_seq == 0)
    def _():
        def load(s, c):
            for g, hh in units:
                st_ref[g, s, hh] = s0_ref[g, s, hh].T
            return c
        lax.fori_loop(0, n_states, load, 0)

    t_i = lax.broadcasted_iota(jnp.int32, (CHUNK, CHUNK), 0)
    s_i = lax.broadcasted_iota(jnp.int32, (CHUNK, CHUNK), 1)
    same = (t_i >> _log2(tc)) == (s_i >> _log2(tc))
    tri = jnp.where(same & (s_i <= t_i), 1.0, 0.0).astype(BF16)
    ones_seq = jnp.where(same, 1.0, 0.0).astype(BF16)
    row = lax.broadcasted_iota(jnp.int32, (CHUNK, 1), 0)
    og = og_ref[...]
    halves = [1 << lev for lev in range(_log2(tc))]
    late = [(row & half) != 0 for half in halves]
    pair_mask = [jnp.where(((t_i >> _log2(2 * half)) == (s_i >> _log2(2 * half)))
                           & ((t_i & half) != 0) & ((s_i & half) == 0), 1.0, 0.0).astype(F32) for half in halves]
    small = [half for half in halves if 2 * half < SUBLANES]
    pick = jnp.concatenate([jnp.where(s_i == (t_i & ~(2 * half - 1)) + half - 1, 1.0, 0.0)
                            for half in small], axis=0).astype(BF16)

    heads = range(len(units))

    def tokens(ref, rows):
        return [ref[g, rows, hh * HG_D:(hh + 1) * HG_D] for g, hh in units]

    def chunk(j, carry):
        rows = pl.ds(pl.multiple_of(j * CHUNK, CHUNK), CHUNK)
        q = [x.astype(F32) for x in tokens(q_ref, rows)]
        k = [x.astype(F32) for x in tokens(k_ref, rows)]
        v = [x.astype(F32) for x in tokens(v_ref, rows)]
        g = tokens(g_ref, rows)
        b = [_sum_rows(tri, x) for x in g]
        b_end = [x[CHUNK - 1:CHUNK, :] for x in b] if gseq == 1 else [_sum_rows(ones_seq, x) for x in g]
        q_in = [q[h] * jnp.exp(b[h]) for h in heads]
        k_dec = [k[h] * jnp.exp(b_end[h] - b[h]) for h in heads]
        vb = [x.astype(BF16) for x in v]

        o = []
        for gi in range(gseq):
            sl = slice(gi * tc, (gi + 1) * tc)
            sidx = j * gseq + gi if n_states > 1 else 0
            st = [st_ref[g, sidx, hh] for g, hh in units]
            o.append([lax.dot_general(narrow(q_in[h][sl]), narrow(st[h]), NT, preferred_element_type=F32)
                      for h in heads])
            upd = [lax.dot_general(narrow(v[h][sl]), narrow(k_dec[h][sl]), TN, preferred_element_type=F32)
                   for h in heads]
            for h in heads:
                decay = b_end[h] if gseq == 1 else b_end[h][gi * tc:gi * tc + 1, :]
                st_ref[units[h][0], sidx, units[h][1]] = st[h] * jnp.exp(decay) + upd[h]
        o = [o[0][h] if gseq == 1 else jnp.concatenate([part[h] for part in o], axis=0) for h in heads]

        beta_small = [_sum_rows(pick, x) for x in b]
        att = [None for _ in heads]
        for lev, (half, is_late, mask) in enumerate(zip(halves, late, pair_mask)):
            for h in heads:
                if half in small:
                    beta = beta_small[h][lev * CHUNK:(lev + 1) * CHUNK]
                else:
                    blocks = b[h].reshape(CHUNK // (2 * half), 2 * half, HG_D)
                    beta = jnp.broadcast_to(blocks[:, half - 1:half, :], blocks.shape).reshape(CHUNK, HG_D)
                q_l = (q[h] * jnp.exp(jnp.where(is_late, b[h] - beta, NEG_BIG))).astype(BF16)
                k_l = (k[h] * jnp.exp(jnp.where(is_late, NEG_BIG, beta - b[h]))).astype(BF16)
                part = lax.dot_general(q_l, k_l, NT, preferred_element_type=F32) * mask
                att[h] = part if att[h] is None else att[h] + part
        o = [o[h] + jnp.dot(att[h].astype(BF16), vb[h], preferred_element_type=F32) for h in heads]
        o = [o[h] + jnp.sum(q[h] * k[h], axis=-1, keepdims=True) * v[h] for h in heads]

        for h in heads:
            out = o[h] * lax.rsqrt(jnp.mean(o[h] * o[h], axis=-1, keepdims=True) + NORM_EPS) * og
            g, hh = units[h]
            o_ref[g, rows, hh * HG_D:(hh + 1) * HG_D] = out.astype(o_ref.dtype)
        return carry

    lax.fori_loop(0, n_chunks, chunk, 0)

    @pl.when(i % steps_per_seq == steps_per_seq - 1)
    def _():
        def store(s, c):
            for g, hh in units:
                sout_ref[g, s, hh] = st_ref[g, s, hh].T
            return c
        lax.fori_loop(0, n_states, store, 0)


SCAN_WIDTH = 8
SCAN_GROUPS = 2
STATE_BLOCK_BYTES = 4 * 1024 * 1024


def _scan_geometry(n, tseq):
    rb = min(512, n)
    if tseq < rb:
        max_states = STATE_BLOCK_BYTES // (SCAN_WIDTH * LANES * LANES * 4)
        rb = min(rb, max(CHUNK, max_states * tseq))
    assert rb % CHUNK == 0 and n % rb == 0
    assert (tseq % rb == 0) or (rb % tseq == 0 and CHUNK % tseq == 0)
    n_states = max(1, rb // tseq)
    steps_per_seq = max(1, tseq // rb)
    return rb, n_states, steps_per_seq


def _layer_state_call(kernel_fn, name, heads, state_tail, grid0, tok_width, tok_inputs, consts, const_specs,
                      s_all, layer, s_new_all, tseq):
    n = tok_inputs[0].shape[0]
    layers, nb = s_all.shape[0], s_all.shape[1]
    groups = SCAN_GROUPS if tseq >= CHUNK else 1
    assert n % groups == 0 and nb % groups == 0
    rb, n_states, steps_per_seq = _scan_geometry(n // groups, tseq)
    grouped = (layers, groups, nb // groups) + s_all.shape[2:]
    tok = pl.BlockSpec((groups, rb, tok_width), lambda h, i: (0, i, h))
    state = pl.BlockSpec((None, groups, n_states, heads) + state_tail,
                         lambda h, i: (layer, 0, i // steps_per_seq, h) + (0,) * len(state_tail))
    in_specs = [tok] * len(tok_inputs) + [state] + const_specs
    args = [t.reshape(groups, n // groups, t.shape[1]) for t in tok_inputs]
    args += [s_all.reshape(grouped)] + consts
    aliases = {}
    if s_new_all is not None:
        in_specs.append(pl.BlockSpec(memory_space=pl.ANY))
        args.append(s_new_all.reshape(grouped))
        aliases = {len(args) - 1: 1}
    out, s_new = pl.pallas_call(
        functools.partial(kernel_fn, rb=rb, tseq=tseq, aliased=s_new_all is not None),
        grid=(grid0, n // groups // rb),
        in_specs=in_specs,
        out_specs=[tok, state],
        out_shape=[jax.ShapeDtypeStruct((groups, n // groups, MIX_W), BF16),
                   jax.ShapeDtypeStruct(grouped, F32)],
        scratch_shapes=[pltpu.VMEM((groups, n_states, SCAN_WIDTH, LANES, LANES), F32)],
        input_output_aliases=aliases,
        compiler_params=_params("arbitrary", "arbitrary"),
        name=name,
    )(*args)
    return out.reshape(n, MIX_W), s_new.reshape(s_all.shape)


def _hgrn_scan(q, k, v, gl, s_all, layer, s_new_all, og, tseq):
    return _layer_state_call(
        _hgrn_scan_kernel, "hgrn_scan", SCAN_WIDTH, (HG_D, HG_D), HG_HEADS // SCAN_WIDTH, SCAN_WIDTH * HG_D,
        [q, k, v, gl], [og.reshape(1, HG_D)], [_resident((1, HG_D), lambda h, i: (0, 0))],
        s_all, layer, s_new_all, tseq)


def _rwkv_proj_kernel(*refs, tseq, long_seq, has_vres):
    it = iter(refs)
    x_ref, prev_ref, ng_ref, mu_ref, w_ref = (next(it) for _ in range(5))
    w0_ref, w1_ref, w2_ref, a0_ref, a1_ref, a2_ref = (next(it) for _ in range(6))
    if has_vres:
        v0_ref, v1_ref, v2_ref, vf_ref = (next(it) for _ in range(4))
    r_ref, k_ref, v_ref, lw_ref, as_ref, xq_ref, sg_ref, h_ref = (next(it) for _ in range(8))
    carry_ref = next(it) if long_seq else None

    x = x_ref[0] if long_seq else x_ref[...]
    tm = x.shape[0]
    h = _rms(x, ng_ref[...])
    shifted = pltpu.roll(h, 1, 0)
    row = lax.broadcasted_iota(jnp.int32, (tm, 1), 0)
    if long_seq:
        tb = pl.program_id(1)

        @pl.when(tb == 0)
        def _():
            carry_ref[0:1, :] = prev_ref[0]

        h_prev = jnp.where(row == 0, carry_ref[0:1, :], shifted)
        carry_ref[0:1, :] = h[tm - 1:tm, :]

        @pl.when(tb == pl.num_programs(1) - 1)
        def _():
            h_ref[0] = h[tm - 1:tm, :]
    else:
        h_prev = jnp.where((row & (tseq - 1)) == 0, prev_ref[...], shifted)
        h_ref[...] = h

    dx = h_prev - h
    hb = h.astype(BF16)
    xr = (h + dx * mu_ref[0:1, :]).astype(BF16)
    xw = (h + dx * mu_ref[1:2, :]).astype(BF16)
    xk = (h + dx * mu_ref[2:3, :]).astype(BF16)
    xv = (h + dx * mu_ref[3:4, :]).astype(BF16)
    xa = (h + dx * mu_ref[4:5, :]).astype(BF16)

    w_mid = jnp.tanh(jnp.dot(xw, w1_ref[...], preferred_element_type=F32)).astype(BF16)
    a_mid = jnp.dot(xa, a1_ref[...], preferred_element_type=F32).astype(BF16)
    if has_vres:
        v_mid = jnp.dot(xv, v1_ref[...], preferred_element_type=F32).astype(BF16)

    for j in range(MIX_W // COL_CHUNK):
        c0 = j * COL_CHUNK
        cols = slice(c0, c0 + COL_CHUNK)
        r_ref[:, cols] = jnp.dot(xr, w_ref[:, c0:c0 + COL_CHUNK], preferred_element_type=F32)
        k_ref[:, cols] = jnp.dot(xk, w_ref[:, MIX_W + c0:MIX_W + c0 + COL_CHUNK], preferred_element_type=F32)
        vz = jnp.dot(xv, w_ref[:, 2 * MIX_W + c0:2 * MIX_W + c0 + COL_CHUNK], preferred_element_type=F32)
        if has_vres:
            gate = _sigmoid(v0_ref[:, cols] + jnp.dot(v_mid, v2_ref[:, cols], preferred_element_type=F32))
            vz = vz + (vf_ref[:, cols] - vz) * gate
        v_ref[:, cols] = vz
        u = w0_ref[:, cols] + jnp.dot(w_mid, w2_ref[:, cols], preferred_element_type=F32)
        softplus = jnp.maximum(-u, 0.0) + jnp.log(1.0 + jnp.exp(-jnp.abs(u)))
        lw_ref[:, cols] = -jnp.exp(-softplus - 0.5)
        as_ref[:, cols] = _sigmoid(a0_ref[:, cols] + jnp.dot(a_mid, a2_ref[:, cols], preferred_element_type=F32))

    base = 3 * MIX_W
    for j in range(CROSS_W // COL_CHUNK):
        c0 = j * COL_CHUNK
        xq_ref[:, c0:c0 + COL_CHUNK] = jnp.dot(hb, w_ref[:, base + c0:base + c0 + COL_CHUNK],
                                                preferred_element_type=F32)
    base = 3 * MIX_W + CROSS_W
    for j in range(BRANCH_W // COL_CHUNK):
        c0 = j * COL_CHUNK
        z = jnp.dot(hb, w_ref[:, base + c0:base + c0 + COL_CHUNK], preferred_element_type=F32)
        sg_ref[:, c0:c0 + COL_CHUNK] = (z * _sigmoid(z)).astype(sg_ref.dtype)


def _rwkv_proj(x2, shift, tseq, ng, mu, w, w0, w1, w2, a0, a1, a2, vres):
    n = x2.shape[0]
    nb = n // tseq
    tm = min(256, n)
    long_seq = tseq >= tm
    has_vres = vres is not None
    vec = lambda a: a.reshape(1, -1)
    mu8 = jnp.zeros((8, D_MODEL), F32).at[:5].set(mu)
    consts = [vec(ng), mu8, w, vec(w0), w1, w2, vec(a0), a1, a2]
    if has_vres:
        v0, v1, v2, v_first = vres
        consts += [vec(v0), v1, v2]
    if long_seq:
        assert tseq % tm == 0
        steps = tseq // tm
        grid = (nb, steps)
        cmap = lambda b, t: (0, 0)
        row = lambda width: pl.BlockSpec((tm, width), lambda b, t: (b * steps + t, 0))
        x_in = x2.reshape(nb, tseq, D_MODEL)
        x_spec = pl.BlockSpec((1, tm, D_MODEL), lambda b, t: (b, t, 0))
        prev_in = shift.reshape(nb, 1, D_MODEL)
        prev_spec = pl.BlockSpec((1, 1, D_MODEL), lambda b, t: (b, 0, 0))
        h_shape = jax.ShapeDtypeStruct((nb, 1, D_MODEL), F32)
        h_spec = pl.BlockSpec((1, 1, D_MODEL), lambda b, t: (b, 0, 0))
        scratch = [pltpu.VMEM((8, D_MODEL), F32)]
        sem = ("arbitrary", "arbitrary")
    else:
        assert tm % tseq == 0 and tseq & (tseq - 1) == 0
        grid = (n // tm,)
        cmap = lambda i: (0, 0)
        row = lambda width: pl.BlockSpec((tm, width), lambda i: (i, 0))
        x_in, x_spec = x2, row(D_MODEL)
        prev_in = jnp.zeros((nb, tseq, D_MODEL), F32).at[:, 0].set(shift).reshape(n, D_MODEL)
        prev_spec = row(D_MODEL)
        h_shape = jax.ShapeDtypeStruct((n, D_MODEL), F32)
        h_spec = row(D_MODEL)
        scratch = []
        sem = ("arbitrary",)
    in_specs = [x_spec, prev_spec] + [_resident(c.shape, cmap) for c in consts]
    args = [x_in, prev_in] + consts
    if has_vres:
        in_specs.append(row(MIX_W))
        args.append(v_first)
    sd = lambda width, dt: jax.ShapeDtypeStruct((n, width), dt)
    outs = pl.pallas_call(
        functools.partial(_rwkv_proj_kernel, tseq=tseq, long_seq=long_seq, has_vres=has_vres),
        grid=grid,
        in_specs=in_specs,
        out_specs=[row(MIX_W)] * 5 + [row(CROSS_W), row(BRANCH_W), h_spec],
        out_shape=[sd(MIX_W, F32)] * 5 + [sd(CROSS_W, F32), sd(BRANCH_W, BF16), h_shape],
        scratch_shapes=scratch,
        compiler_params=_params(*sem),
        name="rwkv_proj",
    )(*args)
    h_out = outs[7]
    new_shift = h_out.reshape(nb, D_MODEL) if long_seq else h_out.reshape(nb, tseq, D_MODEL)[:, -1]
    return list(outs[:7]) + [new_shift]


def _rwkv_scan_kernel(r_ref, k_ref, v_ref, lw_ref, as_ref, s0_ref, par_ref, *rest, rb, tseq, aliased):
    y_ref, sout_ref, st_ref = rest[1:] if aliased else rest
    tc = min(tseq, CHUNK)
    gseq = CHUNK // tc
    n_chunks = rb // CHUNK
    steps_per_seq = max(1, tseq // rb)
    groups, n_states, width = st_ref.shape[0], st_ref.shape[1], st_ref.shape[2]
    n_levels = _log2(tc)
    narrow = _narrow(tc)
    i = pl.program_id(1)

    units = [(g, p) for g in range(groups) for p in range(width)]
    pairs = range(len(units))
    lo, hi = slice(0, RW_N), slice(RW_N, LANES)

    @pl.when(i % steps_per_seq == 0)
    def _():
        zero = jnp.zeros((RW_N, RW_N), F32)

        def load(s, c):
            for g, p in units:
                st_ref[g, s, p, lo, lo] = s0_ref[g, s, 2 * p]
                st_ref[g, s, p, lo, hi] = zero
                st_ref[g, s, p, hi, lo] = zero
                st_ref[g, s, p, hi, hi] = s0_ref[g, s, 2 * p + 1]
            return c
        lax.fori_loop(0, n_states, load, 0)

    t_i = lax.broadcasted_iota(jnp.int32, (CHUNK, CHUNK), 0)
    s_i = lax.broadcasted_iota(jnp.int32, (CHUNK, CHUNK), 1)
    same = (t_i >> _log2(tc)) == (s_i >> _log2(tc))
    tri = jnp.where(same & (s_i <= t_i), 1.0, 0.0).astype(BF16)
    ones_seq = jnp.where(same, 1.0, 0.0).astype(BF16)
    n_i = lax.broadcasted_iota(jnp.int32, (CHUNK, 2 * CHUNK), 0)
    m_i = lax.broadcasted_iota(jnp.int32, (CHUNK, 2 * CHUNK), 1) & (CHUNK - 1)
    same2 = (n_i >> _log2(tc)) == (m_i >> _log2(tc))
    strict2 = jnp.where(same2 & (m_i < n_i), 1.0, 0.0).astype(F32)
    incl2 = jnp.where(same2 & (m_i <= n_i), 1.0, 0.0).astype(F32)
    lane = lax.broadcasted_iota(jnp.int32, (1, LANES), 1)
    head0 = jnp.where(lane < RW_N, 1.0, 0.0).astype(F32)
    head1 = 1.0 - head0
    ones_head = jnp.where((lax.broadcasted_iota(jnp.int32, (LANES, LANES), 0) >> _log2(RW_N))
                          == (lax.broadcasted_iota(jnp.int32, (LANES, LANES), 1) >> _log2(RW_N)),
                          1.0, 0.0).astype(F32)
    head0_b, head1_b = head0.astype(BF16), head1.astype(BF16)

    def stack(a):
        ab = a.astype(BF16)
        return jnp.concatenate([ab * head0_b, ab * head1_b], axis=0)

    def head_sum(a):
        s0 = jnp.sum(a * head0, axis=-1, keepdims=True)
        s1 = jnp.sum(a * head1, axis=-1, keepdims=True)
        return jnp.where(lane < RW_N, s0, s1)

    def tokens(ref, rows):
        return [ref[g, rows, p * LANES:(p + 1) * LANES] for g, p in units]

    def par(u, row):
        return par_ref[units[u][1], row:row + 1, :]

    def chunk(j, carry):
        rows = pl.ds(pl.multiple_of(j * CHUNK, CHUNK), CHUNK)
        r, k, v = tokens(r_ref, rows), tokens(k_ref, rows), tokens(v_ref, rows)
        lw, a_gate = tokens(lw_ref, rows), tokens(as_ref, rows)

        c = [_sum_rows(tri, x) for x in lw]
        c_end = [x[CHUNK - 1:CHUNK, :] for x in c] if gseq == 1 else [_sum_rows(ones_seq, x) for x in lw]
        kk = [k[p] * par(p, 0) for p in pairs]
        kk = [x / jnp.maximum(jnp.sqrt(head_sum(x * x)), 1e-12) for x in kk]
        b_vec = [kk[p] * a_gate[p] for p in pairs]
        k_mod = [k[p] * (1.0 + (a_gate[p] - 1.0) * par(p, 1)) for p in pairs]
        e_neg = [jnp.exp(-x) for x in c]
        a_t = [-kk[p] * jnp.exp(c[p] - lw[p]) for p in pairs]
        r_t = [r[p] * jnp.exp(c[p]) for p in pairs]

        lhs = [jnp.concatenate([a_t[p], r_t[p]], axis=0).astype(BF16) for p in pairs]
        rhs = [jnp.concatenate([stack(b_vec[p] * e_neg[p]), stack(k_mod[p] * e_neg[p])], axis=0) for p in pairs]
        big = [lax.dot_general(lhs[p], rhs[p], NT, preferred_element_type=F32) for p in pairs]
        l_ab = [x[0:CHUNK, 0:LANES] * strict2 for x in big]
        l_ak = [x[0:CHUNK, LANES:2 * LANES] * strict2 for x in big]
        m_rb = [x[CHUNK:2 * CHUNK, 0:LANES] * incl2 for x in big]
        m_rk = [x[CHUNK:2 * CHUNK, LANES:2 * LANES] * incl2 for x in big]
        v_s = [stack(x) for x in v]

        ah0, rh0, states = [], [], []
        for gi in range(gseq):
            sl = slice(gi * tc, (gi + 1) * tc)
            sidx = j * gseq + gi if n_states > 1 else 0
            st = [st_ref[g, sidx, p] for g, p in units]
            states.append((sidx, st))
            ar = [jnp.concatenate([a_t[p][sl], r_t[p][sl]], axis=0) for p in pairs]
            prod = [lax.dot_general(narrow(ar[p]), narrow(st[p]), NT, preferred_element_type=F32) for p in pairs]
            ah0.append([x[0:tc] for x in prod])
            rh0.append([x[tc:2 * tc] for x in prod])
        ah0 = [ah0[0][p] if gseq == 1 else jnp.concatenate([part[p] for part in ah0], axis=0) for p in pairs]
        rh0 = [rh0[0][p] if gseq == 1 else jnp.concatenate([part[p] for part in rh0], axis=0) for p in pairs]

        u = [ah0[p] + jnp.dot(l_ak[p].astype(BF16), v_s[p], preferred_element_type=F32) for p in pairs]
        pw = l_ab
        for lev in range(n_levels):
            pb = [m.astype(BF16) for m in pw]
            u = [u[p] + jnp.dot(pb[p], stack(u[p]), preferred_element_type=F32) for p in pairs]
            if lev < n_levels - 1:
                pw = [jnp.dot(pb[p], stack(pw[p]), preferred_element_type=F32) for p in pairs]
        y = [rh0[p] + jnp.dot(jnp.concatenate([m_rb[p], m_rk[p]], axis=1).astype(BF16),
                              jnp.concatenate([stack(u[p]), v_s[p]], axis=0),
                              preferred_element_type=F32) for p in pairs]

        e_end = [jnp.exp(c_end[p] - c[p]) for p in pairs]
        b_hat = [b_vec[p] * e_end[p] for p in pairs]
        k_hat = [k_mod[p] * e_end[p] for p in pairs]
        for gi, (sidx, st) in enumerate(states):
            sl = slice(gi * tc, (gi + 1) * tc)
            uv = [jnp.concatenate([u[p][sl], v[p][sl]], axis=0) for p in pairs]
            bk = [jnp.concatenate([b_hat[p][sl], k_hat[p][sl]], axis=0) for p in pairs]
            upd = [lax.dot_general(narrow(uv[p]), narrow(bk[p]), TN, preferred_element_type=F32) for p in pairs]
            for p in pairs:
                decay = c_end[p] if gseq == 1 else c_end[p][gi * tc:gi * tc + 1, :]
                st_ref[units[p][0], sidx, units[p][1]] = st[p] * jnp.exp(decay) + upd[p] * ones_head

        mean = [head_sum(m) * (1.0 / RW_N) for m in y]
        d = [y[p] - mean[p] for p in pairs]
        var = [head_sum(m * m) * (1.0 / RW_N) for m in d]
        bonus = [head_sum(r[p] * k_mod[p] * par(p, 2)) for p in pairs]
        for p in pairs:
            out = d[p] * lax.rsqrt(var[p] + RW_LNX_EPS) * par(p, 3) + par(p, 4)
            g, pp = units[p]
            y_ref[g, rows, pp * LANES:(pp + 1) * LANES] = (out + bonus[p] * v[p]).astype(y_ref.dtype)
        return carry

    lax.fori_loop(0, n_chunks, chunk, 0)

    @pl.when(i % steps_per_seq == steps_per_seq - 1)
    def _():
        def store(s, c):
            for g, p in units:
                sout_ref[g, s, 2 * p] = st_ref[g, s, p, lo, lo]
                sout_ref[g, s, 2 * p + 1] = st_ref[g, s, p, hi, hi]
            return c
        lax.fori_loop(0, n_states, store, 0)


def _rwkv_scan(r, k, v, lw, a_gate, s_all, layer, s_new_all, par, tseq):
    return _layer_state_call(
        _rwkv_scan_kernel, "rwkv_scan", 2 * SCAN_WIDTH, (RW_N, RW_N), RW_PAIRS // SCAN_WIDTH, SCAN_WIDTH * LANES,
        [r, k, v, lw, a_gate], [par], [pl.BlockSpec((SCAN_WIDTH, 8, LANES), lambda p, i: (p, 0, 0))],
        s_all, layer, s_new_all, tseq)


def _cross_attention(q_ref, k_ref, v_ref, rows_per_seq, interleaved):
    n_seq = k_ref.shape[0]
    scale = X_DH ** -0.5
    units = [(s, hh) for s in range(n_seq) for hh in range(X_HEADS)]

    def memory(ref, s, hh):
        if interleaved:
            return ref[s, pl.ds(hh, MEM_LEN, stride=X_HEADS), :].astype(BF16)
        return ref[s, :, hh * X_DH:(hh + 1) * X_DH].astype(BF16)

    q = [q_ref[s * rows_per_seq:(s + 1) * rows_per_seq, hh * X_DH:(hh + 1) * X_DH].astype(BF16) for s, hh in units]
    sc = [lax.dot_general(q[u], memory(k_ref, s, hh), NT, preferred_element_type=F32) * scale
          for u, (s, hh) in enumerate(units)]
    p = [jnp.exp(x - jnp.max(x, axis=-1, keepdims=True)) for x in sc]
    denom = [jnp.sum(x, axis=-1, keepdims=True) for x in p]
    o = [jnp.dot(p[u].astype(BF16), memory(v_ref, s, hh), preferred_element_type=F32) / denom[u]
         for u, (s, hh) in enumerate(units)]
    return [jnp.concatenate(o[s * X_HEADS:(s + 1) * X_HEADS], axis=-1) for s in range(n_seq)]


def _out_kernel(x_ref, mix_ref, xq_ref, k_ref, v_ref, sg_ref, w_ref, fg_ref, o_ref, *, rows_per_seq, interleaved,
                final):
    xo = _cross_attention(xq_ref, k_ref, v_ref, rows_per_seq, interleaved)
    xo = xo[0] if len(xo) == 1 else jnp.concatenate(xo, axis=0)
    sg = sg_ref[...].astype(F32)
    left = (mix_ref[...].astype(F32) * sg[:, :MIX_W]).astype(BF16)
    right = (xo * sg[:, MIX_W:]).astype(BF16)
    x = x_ref[...] + jnp.dot(left, w_ref[0:MIX_W, :], preferred_element_type=F32) \
        + jnp.dot(right, w_ref[MIX_W:BRANCH_W, :], preferred_element_type=F32)
    o_ref[...] = _rms(x, fg_ref[...]) if final else x


def _out_proj(x2, mix, xq, mk_all, mv_all, layer, tseq, sg, w, final_g, final):
    n = x2.shape[0]
    interleaved = mk_all.shape[-1] == X_DH
    rb = min(512, n, 8 * tseq)
    n_seq = max(1, rb // tseq)
    steps_per_seq = max(1, tseq // rb)
    row = lambda width: pl.BlockSpec((rb, width), lambda i: (i, 0))
    mem = pl.BlockSpec((None, n_seq) + mk_all.shape[2:], lambda i: (layer, i // steps_per_seq, 0, 0))
    return pl.pallas_call(
        functools.partial(_out_kernel, rows_per_seq=rb // n_seq, interleaved=interleaved, final=final),
        grid=(n // rb,),
        in_specs=[row(D_MODEL), row(MIX_W), row(CROSS_W), mem, mem, row(BRANCH_W),
                  _resident((BRANCH_W, D_MODEL), lambda i: (0, 0)),
                  _resident((1, D_MODEL), lambda i: (0, 0))],
        out_specs=row(D_MODEL),
        out_shape=jax.ShapeDtypeStruct((n, D_MODEL), F32),
        compiler_params=_params("arbitrary"),
        name="out_proj",
    )(x2, mix, xq, mk_all, mv_all, sg, w, final_g.reshape(1, D_MODEL))


def _trunk(x, mem_k, mem_v, s_hgrn, s_rwkv, s_shift, p):
    nb, tseq, _ = x.shape
    x2 = x.reshape(nb * tseq, D_MODEL)
    new_h, new_r, new_s = None, None, []
    v_first = None
    for i in range(DEPTH):
        j = i // 2
        if i % 2 == 0:
            q, k, gl, v, xq, sg = _hgrn_proj(x2, p["norm_g"][i], p["w_in"][i], p["lbs"][j])
            mix, new_h = _hgrn_scan(q, k, v, gl, s_hgrn, j, new_h, p["hg_onorm_g"][j], tseq)
        else:
            vres = None if v_first is None else (p["rw_v0"][j - 1], p["rw_v1"][j - 1], p["rw_v2"][j - 1], v_first)
            r, k, v, lw, a_gate, xq, sg, shift = _rwkv_proj(
                x2, s_shift[j], tseq, p["norm_g"][i], p["rw_mu"][j], p["w_in"][i],
                p["rw_w0"][j], p["rw_w1"][j], p["rw_w2"][j], p["rw_a0"][j], p["rw_a1"][j], p["rw_a2"][j], vres)
            if v_first is None:
                v_first = v
            mix, new_r = _rwkv_scan(r, k, v, lw, a_gate, s_rwkv, j, new_r, p["rw_par"][j], tseq)
            new_s.append(shift)
        x2 = _out_proj(x2, mix, xq, mem_k, mem_v, i, tseq, sg, p["w_out"][i], p["final_g"], final=(i == DEPTH - 1))
    return x2.reshape(nb, tseq, D_MODEL), new_h, new_r, jnp.stack(new_s)


def kernel(x_prompt, x_sample, mem_prompt, state_hgrn, state_rwkv, state_shift, cache_mem_k, cache_mem_v,
           norm_g, w_in, w_out, mem_norm_g, w_mem_kv, hg_lb, hg_onorm_g, rw_mu, rw_w0, rw_w1, rw_w2,
           rw_a0, rw_a1, rw_a2, rw_v0, rw_v1, rw_v2, rw_kk, rw_ka, rw_rk, rw_lnx_g, rw_lnx_b, final_g):
    n_rwkv = rw_mu.shape[0]
    lbs = jax.nn.softmax(hg_lb.astype(F32), axis=0)
    lbs = jnp.cumsum(lbs, axis=0) - lbs[0]
    par = jnp.stack([rw_kk, rw_ka, rw_rk.reshape(n_rwkv, MIX_W), rw_lnx_g, rw_lnx_b], axis=1)
    par = jnp.concatenate([par, jnp.zeros((n_rwkv, 3, MIX_W), F32)], axis=1)
    par = par.reshape(n_rwkv, 8, RW_PAIRS, LANES).transpose(0, 2, 1, 3)
    p = dict(norm_g=norm_g, w_in=w_in.astype(BF16), w_out=w_out.astype(BF16), lbs=lbs, hg_onorm_g=hg_onorm_g,
             rw_mu=rw_mu, rw_w0=rw_w0, rw_w1=rw_w1.astype(BF16), rw_w2=rw_w2.astype(BF16),
             rw_a0=rw_a0, rw_a1=rw_a1.astype(BF16), rw_a2=rw_a2.astype(BF16),
             rw_v0=rw_v0, rw_v1=rw_v1.astype(BF16), rw_v2=rw_v2.astype(BF16), rw_par=par, final_g=final_g)

    nb, mem_len, _ = mem_prompt.shape
    mk, mv = _memory_kv(mem_prompt.reshape(nb * mem_len, D_MODEL), mem_norm_g, w_mem_kv.astype(BF16))
    mk = mk.reshape(DEPTH, nb, mem_len, CROSS_W)
    mv = mv.reshape(DEPTH, nb, mem_len, CROSS_W)
    z_h = jnp.zeros((state_hgrn.shape[0], nb) + state_hgrn.shape[2:], F32)
    z_r = jnp.zeros((state_rwkv.shape[0], nb) + state_rwkv.shape[2:], F32)
    z_s = jnp.zeros((state_shift.shape[0], nb, D_MODEL), F32)
    y_p, sh_p, sr_p, ss_p = _trunk(x_prompt, mk, mv, z_h, z_r, z_s, p)

    nbs = x_sample.shape[0]
    cmk = cache_mem_k.reshape(DEPTH, nbs, mem_len * X_HEADS, X_DH)
    cmv = cache_mem_v.reshape(DEPTH, nbs, mem_len * X_HEADS, X_DH)
    y_s, sh_s, sr_s, ss_s = _trunk(x_sample, cmk, cmv, state_hgrn, state_rwkv, state_shift, p)
    return (y_p, y_s, sh_p, sr_p, ss_p,
            mk.reshape(DEPTH, nb, mem_len, X_HEADS, X_DH), mv.reshape(DEPTH, nb, mem_len, X_HEADS, X_DH),
            sh_s, sr_s, ss_s)
```

```python
import functools
import math

import jax
import jax.numpy as jnp
from jax import lax
from jax.experimental import pallas as pl
from jax.experimental.pallas import tpu as pltpu

F32 = jnp.float32
BF16 = jnp.bfloat16

D_MODEL = 1024
DEPTH = 4
MIX_W = D_MODEL
HG_HEADS = 8
HG_D = MIX_W // HG_HEADS
RW_N = 64
RW_HEADS = MIX_W // RW_N
RW_PAIRS = RW_HEADS // 2
RW_LNX_EPS = 64e-5
MEM_LEN = 256
X_HEADS = 4
X_DH = 128
CROSS_W = X_HEADS * X_DH
BRANCH_W = MIX_W + CROSS_W
IN_COLS = 3 * MIX_W + CROSS_W + BRANCH_W
NORM_EPS = 1e-6
LOG_FLOOR = 1e-30
NEG_BIG = -1e30

LANES = 128
SUBLANES = 8
CHUNK = 64
COL_CHUNK = 512
VMEM_LIMIT = 56 * 1024 * 1024

NT = (((1,), (1,)), ((), ()))
TN = (((0,), (0,)), ((), ()))


def _params(*sem):
    return pltpu.CompilerParams(dimension_semantics=sem, vmem_limit_bytes=VMEM_LIMIT)


def _resident(shape, index_map):
    return pl.BlockSpec(shape, index_map, pipeline_mode=pl.Buffered(1))


def _rms(x, g):
    return x * lax.rsqrt(jnp.mean(x * x, axis=-1, keepdims=True) + NORM_EPS) * g


def _sigmoid(z):
    return 1.0 / (1.0 + jnp.exp(-z))


def _sum_rows(sel, x):
    hi = x.astype(BF16)
    rest = x - hi.astype(F32)
    mid = rest.astype(BF16)
    low = (rest - mid.astype(F32)).astype(BF16)
    parts = jnp.dot(sel, jnp.concatenate([hi, mid, low], axis=1), preferred_element_type=F32)
    return parts[:, 0:LANES] + parts[:, LANES:2 * LANES] + parts[:, 2 * LANES:3 * LANES]


def _narrow(rows):
    if rows % 16 == 0:
        return lambda a: a.astype(BF16)
    return lambda a: a.astype(BF16).astype(F32)


def _log2(n):
    l = int(math.log2(n))
    assert 1 << l == n, n
    return l


def _memkv_kernel(x_ref, g_ref, w_ref, k_ref, v_ref):
    hb = _rms(x_ref[...], g_ref[...]).astype(BF16)
    kv = jnp.dot(hb, w_ref[...], preferred_element_type=F32)
    k_ref[...] = kv[:, :CROSS_W]
    v_ref[...] = kv[:, CROSS_W:]


def _memory_kv(mem2d, g, w):
    n = mem2d.shape[0]
    tm = min(512, n)
    out = jax.ShapeDtypeStruct((DEPTH, n, CROSS_W), F32)
    return pl.pallas_call(
        _memkv_kernel,
        grid=(DEPTH, n // tm),
        in_specs=[
            pl.BlockSpec((tm, D_MODEL), lambda l, i: (i, 0)),
            pl.BlockSpec((None, 1, D_MODEL), lambda l, i: (l, 0, 0)),
            pl.BlockSpec((None, D_MODEL, 2 * CROSS_W), lambda l, i: (l, 0, 0)),
        ],
        out_specs=[
            pl.BlockSpec((None, tm, CROSS_W), lambda l, i: (l, i, 0)),
            pl.BlockSpec((None, tm, CROSS_W), lambda l, i: (l, i, 0)),
        ],
        out_shape=[out, out],
        compiler_params=_params("arbitrary", "arbitrary"),
        name="memory_kv",
    )(mem2d, g.reshape(DEPTH, 1, D_MODEL), w)


def _hgrn_proj_kernel(x_ref, g_ref, w_ref, lb_ref, q_ref, k_ref, gl_ref, v_ref, xq_ref, sg_ref):
    hb = _rms(x_ref[...], g_ref[...]).astype(BF16)
    for j in range(IN_COLS // COL_CHUNK):
        c0 = j * COL_CHUNK
        z = jnp.dot(hb, w_ref[:, c0:c0 + COL_CHUNK], preferred_element_type=F32)
        if c0 < MIX_W:
            q_ref[:, c0:c0 + COL_CHUNK] = (z * _sigmoid(z)).astype(q_ref.dtype)
        elif c0 < 2 * MIX_W:
            d0 = c0 - MIX_W
            lb = lb_ref[:, d0:d0 + COL_CHUNK]
            f = lb + (1.0 - lb) * _sigmoid(z)
            gl_ref[:, d0:d0 + COL_CHUNK] = jnp.log(jnp.maximum(f, LOG_FLOOR))
            k_ref[:, d0:d0 + COL_CHUNK] = ((1.0 - lb) * _sigmoid(-z)).astype(k_ref.dtype)
        elif c0 < 3 * MIX_W:
            d0 = c0 - 2 * MIX_W
            v_ref[:, d0:d0 + COL_CHUNK] = z.astype(v_ref.dtype)
        elif c0 < 3 * MIX_W + CROSS_W:
            d0 = c0 - 3 * MIX_W
            xq_ref[:, d0:d0 + COL_CHUNK] = z
        else:
            d0 = c0 - 3 * MIX_W - CROSS_W
            sg_ref[:, d0:d0 + COL_CHUNK] = (z * _sigmoid(z)).astype(sg_ref.dtype)


def _hgrn_proj(x2, g, w, lb):
    n = x2.shape[0]
    tm = min(1024, n)
    row = lambda width: pl.BlockSpec((tm, width), lambda i: (i, 0))
    sd = lambda width, dt: jax.ShapeDtypeStruct((n, width), dt)
    return pl.pallas_call(
        _hgrn_proj_kernel,
        grid=(n // tm,),
        in_specs=[
            row(D_MODEL),
            _resident((1, D_MODEL), lambda i: (0, 0)),
            _resident((D_MODEL, IN_COLS), lambda i: (0, 0)),
            _resident((1, MIX_W), lambda i: (0, 0)),
        ],
        out_specs=[row(MIX_W), row(MIX_W), row(MIX_W), row(MIX_W), row(CROSS_W), row(BRANCH_W)],
        out_shape=[sd(MIX_W, BF16), sd(MIX_W, BF16), sd(MIX_W, F32), sd(MIX_W, BF16),
                   sd(CROSS_W, F32), sd(BRANCH_W, BF16)],
        compiler_params=_params("arbitrary"),
        name="hgrn_proj",
    )(x2, g.reshape(1, D_MODEL), w, lb.reshape(1, MIX_W))


def _own_layer(sout_all_ref, layer, aliased):
    return sout_all_ref if aliased else sout_all_ref.at[layer]


def _zero_other_layers(sout_all_ref, layer, aliased, s):
    if aliased:
        return
    for other in range(sout_all_ref.shape[0]):
        if other != layer:
            for g in range(sout_all_ref.shape[1]):
                sout_all_ref[other, g, s] = jnp.zeros(sout_all_ref.shape[3:], F32)


def _hgrn_scan_kernel(q_ref, k_ref, v_ref, g_ref, s0_ref, og_ref, *rest, rb, tseq, layer, aliased):
    o_ref, sout_all_ref, st_ref = rest[1:] if aliased else rest
    sout_ref = _own_layer(sout_all_ref, layer, aliased)
    tc = min(tseq, CHUNK)
    gseq = CHUNK // tc
    narrow = _narrow(tc)
    n_chunks = rb // CHUNK
    steps_per_seq = max(1, tseq // rb)
    groups, n_states, width = st_ref.shape[0], st_ref.shape[1], st_ref.shape[2]
    i = pl.program_id(1)
    units = [(g, hh) for g in range(groups) for hh in range(width)]

    @pl.when(i % steps_per_seq == 0)
    def _():
        def load(s, c):
            for g, hh in units:
                st_ref[g, s, hh] = s0_ref[g, s, hh].T
            return c
        lax.fori_loop(0, n_states, load, 0)

    t_i = lax.broadcasted_iota(jnp.int32, (CHUNK, CHUNK), 0)
    s_i = lax.broadcasted_iota(jnp.int32, (CHUNK, CHUNK), 1)
    same = (t_i >> _log2(tc)) == (s_i >> _log2(tc))
    tri = jnp.where(same & (s_i <= t_i), 1.0, 0.0).astype(BF16)
    ones_seq = jnp.where(same, 1.0, 0.0).astype(BF16)
    row = lax.broadcasted_iota(jnp.int32, (CHUNK, 1), 0)
    og = og_ref[...]
    halves = [1 << lev for lev in range(_log2(tc))]
    late = [(row & half) != 0 for half in halves]
    pair_mask = [jnp.where(((t_i >> _log2(2 * half)) == (s_i >> _log2(2 * half)))
                           & ((t_i & half) != 0) & ((s_i & half) == 0), 1.0, 0.0).astype(F32) for half in halves]
    small = [half for half in halves if 2 * half < SUBLANES]
    pick = jnp.concatenate([jnp.where(s_i == (t_i & ~(2 * half - 1)) + half - 1, 1.0, 0.0)
                            for half in small], axis=0).astype(BF16)

    heads = range(len(units))

    def tokens(ref, rows):
        return [ref[g, rows, hh * HG_D:(hh + 1) * HG_D] for g, hh in units]

    def chunk(j, carry):
        rows = pl.ds(pl.multiple_of(j * CHUNK, CHUNK), CHUNK)
        q = [x.astype(F32) for x in tokens(q_ref, rows)]
        k = [x.astype(F32) for x in tokens(k_ref, rows)]
        v = [x.astype(F32) for x in tokens(v_ref, rows)]
        g = tokens(g_ref, rows)
        b = [_sum_rows(tri, x) for x in g]
        b_end = [x[CHUNK - 1:CHUNK, :] for x in b] if gseq == 1 else [_sum_rows(ones_seq, x) for x in g]
        q_in = [q[h] * jnp.exp(b[h]) for h in heads]
        k_dec = [k[h] * jnp.exp(b_end[h] - b[h]) for h in heads]
        vb = [x.astype(BF16) for x in v]

        o = []
        for gi in range(gseq):
            sl = slice(gi * tc, (gi + 1) * tc)
            sidx = j * gseq + gi if n_states > 1 else 0
            st = [st_ref[g, sidx, hh] for g, hh in units]
            o.append([lax.dot_general(narrow(q_in[h][sl]), narrow(st[h]), NT, preferred_element_type=F32)
                      for h in heads])
            upd = [lax.dot_general(narrow(v[h][sl]), narrow(k_dec[h][sl]), TN, preferred_element_type=F32)
                   for h in heads]
            for h in heads:
                decay = b_end[h] if gseq == 1 else b_end[h][gi * tc:gi * tc + 1, :]
                st_ref[units[h][0], sidx, units[h][1]] = st[h] * jnp.exp(decay) + upd[h]
        o = [o[0][h] if gseq == 1 else jnp.concatenate([part[h] for part in o], axis=0) for h in heads]

        beta_small = [_sum_rows(pick, x) for x in b]
        att = [None for _ in heads]
        for lev, (half, is_late, mask) in enumerate(zip(halves, late, pair_mask)):
            for h in heads:
                if half in small:
                    beta = beta_small[h][lev * CHUNK:(lev + 1) * CHUNK]
                else:
                    blocks = b[h].reshape(CHUNK // (2 * half), 2 * half, HG_D)
                    beta = jnp.broadcast_to(blocks[:, half - 1:half, :], blocks.shape).reshape(CHUNK, HG_D)
                q_l = (q[h] * jnp.exp(jnp.where(is_late, b[h] - beta, NEG_BIG))).astype(BF16)
                k_l = (k[h] * jnp.exp(jnp.where(is_late, NEG_BIG, beta - b[h]))).astype(BF16)
                part = lax.dot_general(q_l, k_l, NT, preferred_element_type=F32) * mask
                att[h] = part if att[h] is None else att[h] + part
        o = [o[h] + jnp.dot(att[h].astype(BF16), vb[h], preferred_element_type=F32) for h in heads]
        o = [o[h] + jnp.sum(q[h] * k[h], axis=-1, keepdims=True) * v[h] for h in heads]

        for h in heads:
            out = o[h] * lax.rsqrt(jnp.mean(o[h] * o[h], axis=-1, keepdims=True) + NORM_EPS) * og
            g, hh = units[h]
            o_ref[g, rows, hh * HG_D:(hh + 1) * HG_D] = out.astype(o_ref.dtype)
        return carry

    lax.fori_loop(0, n_chunks, chunk, 0)

    @pl.when(i % steps_per_seq == steps_per_seq - 1)
    def _():
        def store(s, c):
            for g, hh in units:
                sout_ref[g, s, hh] = st_ref[g, s, hh].T
            _zero_other_layers(sout_all_ref, layer, aliased, s)
            return c
        lax.fori_loop(0, n_states, store, 0)


SCAN_WIDTH = 8
SCAN_GROUPS = 2
SCAN_ROWS = 256
STATE_BLOCK_BYTES = 4 * 1024 * 1024


def _scan_geometry(n, tseq):
    rb = min(SCAN_ROWS, n)
    if tseq < rb:
        max_states = STATE_BLOCK_BYTES // (SCAN_WIDTH * LANES * LANES * 4)
        rb = min(rb, max(CHUNK, max_states * tseq))
    assert rb % CHUNK == 0 and n % rb == 0
    assert (tseq % rb == 0) or (rb % tseq == 0 and CHUNK % tseq == 0)
    n_states = max(1, rb // tseq)
    steps_per_seq = max(1, tseq // rb)
    return rb, n_states, steps_per_seq


def _layer_state_call(kernel_fn, name, heads, state_tail, grid0, tok_width, tok_inputs, consts, const_specs,
                      s_all, layer, s_new_all, tseq):
    n = tok_inputs[0].shape[0]
    layers, nb = s_all.shape[0], s_all.shape[1]
    groups = SCAN_GROUPS if tseq >= CHUNK else 1
    assert n % groups == 0 and nb % groups == 0
    rb, n_states, steps_per_seq = _scan_geometry(n // groups, tseq)
    grouped = (layers, groups, nb // groups) + s_all.shape[2:]
    tok = pl.BlockSpec((groups, rb, tok_width), lambda h, i: (0, i, h))
    state = pl.BlockSpec((None, groups, n_states, heads) + state_tail,
                         lambda h, i: (layer, 0, i // steps_per_seq, h) + (0,) * len(state_tail))
    in_specs = [tok] * len(tok_inputs) + [state] + const_specs
    args = [t.reshape(groups, n // groups, t.shape[1]) for t in tok_inputs]
    args += [s_all.reshape(grouped)] + consts
    aliases = {}
    if s_new_all is not None:
        in_specs.append(pl.BlockSpec(memory_space=pl.ANY))
        args.append(s_new_all.reshape(grouped))
        aliases = {len(args) - 1: 1}
        state_out = state
    else:
        state_out = pl.BlockSpec((layers, groups, n_states, heads) + state_tail,
                                 lambda h, i: (0, 0, i // steps_per_seq, h) + (0,) * len(state_tail))
    out, s_new = pl.pallas_call(
        functools.partial(kernel_fn, rb=rb, tseq=tseq, layer=layer, aliased=s_new_all is not None),
        grid=(grid0, n // groups // rb),
        in_specs=in_specs,
        out_specs=[tok, state_out],
        out_shape=[jax.ShapeDtypeStruct((groups, n // groups, MIX_W), BF16),
                   jax.ShapeDtypeStruct(grouped, F32)],
        scratch_shapes=[pltpu.VMEM((groups, n_states, SCAN_WIDTH, LANES, LANES), F32)],
        input_output_aliases=aliases,
        compiler_params=_params("arbitrary", "arbitrary"),
        name=name,
    )(*args)
    return out.reshape(n, MIX_W), s_new.reshape(s_all.shape)


def _hgrn_scan(q, k, v, gl, s_all, layer, s_new_all, og, tseq):
    return _layer_state_call(
        _hgrn_scan_kernel, "hgrn_scan", SCAN_WIDTH, (HG_D, HG_D), HG_HEADS // SCAN_WIDTH, SCAN_WIDTH * HG_D,
        [q, k, v, gl], [og.reshape(1, HG_D)], [_resident((1, HG_D), lambda h, i: (0, 0))],
        s_all, layer, s_new_all, tseq)


def _rwkv_proj_kernel(*refs, tseq, long_seq, has_vres):
    it = iter(refs)
    x_ref, prev_ref, ng_ref, mu_ref, w_ref = (next(it) for _ in range(5))
    w0_ref, w1_ref, w2_ref, a0_ref, a1_ref, a2_ref = (next(it) for _ in range(6))
    if has_vres:
        v0_ref, v1_ref, v2_ref, vf_ref = (next(it) for _ in range(4))
    r_ref, k_ref, v_ref, lw_ref, as_ref, xq_ref, sg_ref, h_ref = (next(it) for _ in range(8))
    carry_ref = next(it) if long_seq else None

    x = x_ref[0] if long_seq else x_ref[...]
    tm = x.shape[0]
    h = _rms(x, ng_ref[...])
    shifted = pltpu.roll(h, 1, 0)
    row = lax.broadcasted_iota(jnp.int32, (tm, 1), 0)
    if long_seq:
        tb = pl.program_id(1)

        @pl.when(tb == 0)
        def _():
            carry_ref[0:1, :] = prev_ref[0]

        h_prev = jnp.where(row == 0, carry_ref[0:1, :], shifted)
        carry_ref[0:1, :] = h[tm - 1:tm, :]

        @pl.when(tb == pl.num_programs(1) - 1)
        def _():
            h_ref[0] = h[tm - 1:tm, :]
    else:
        h_prev = jnp.where((row & (tseq - 1)) == 0, prev_ref[...], shifted)
        h_ref[...] = h

    dx = h_prev - h
    hb = h.astype(BF16)
    xr = (h + dx * mu_ref[0:1, :]).astype(BF16)
    xw = (h + dx * mu_ref[1:2, :]).astype(BF16)
    xk = (h + dx * mu_ref[2:3, :]).astype(BF16)
    xv = (h + dx * mu_ref[3:4, :]).astype(BF16)
    xa = (h + dx * mu_ref[4:5, :]).astype(BF16)

    w_mid = jnp.tanh(jnp.dot(xw, w1_ref[...], preferred_element_type=F32)).astype(BF16)
    a_mid = jnp.dot(xa, a1_ref[...], preferred_element_type=F32).astype(BF16)
    if has_vres:
        v_mid = jnp.dot(xv, v1_ref[...], preferred_element_type=F32).astype(BF16)

    for j in range(MIX_W // COL_CHUNK):
        c0 = j * COL_CHUNK
        cols = slice(c0, c0 + COL_CHUNK)
        r_ref[:, cols] = jnp.dot(xr, w_ref[:, c0:c0 + COL_CHUNK], preferred_element_type=F32).astype(r_ref.dtype)
        k_ref[:, cols] = jnp.dot(xk, w_ref[:, MIX_W + c0:MIX_W + c0 + COL_CHUNK],
                                 preferred_element_type=F32).astype(k_ref.dtype)
        vz = jnp.dot(xv, w_ref[:, 2 * MIX_W + c0:2 * MIX_W + c0 + COL_CHUNK], preferred_element_type=F32)
        if has_vres:
            gate = _sigmoid(v0_ref[:, cols] + jnp.dot(v_mid, v2_ref[:, cols], preferred_element_type=F32))
            vz = vz + (vf_ref[:, cols].astype(F32) - vz) * gate
        v_ref[:, cols] = vz.astype(v_ref.dtype)
        u = w0_ref[:, cols] + jnp.dot(w_mid, w2_ref[:, cols], preferred_element_type=F32)
        softplus = jnp.maximum(-u, 0.0) + jnp.log(1.0 + jnp.exp(-jnp.abs(u)))
        lw_ref[:, cols] = -jnp.exp(-softplus - 0.5)
        as_ref[:, cols] = _sigmoid(a0_ref[:, cols] + jnp.dot(a_mid, a2_ref[:, cols], preferred_element_type=F32))

    base = 3 * MIX_W
    for j in range(CROSS_W // COL_CHUNK):
        c0 = j * COL_CHUNK
        xq_ref[:, c0:c0 + COL_CHUNK] = jnp.dot(hb, w_ref[:, base + c0:base + c0 + COL_CHUNK],
                                                preferred_element_type=F32)
    base = 3 * MIX_W + CROSS_W
    for j in range(BRANCH_W // COL_CHUNK):
        c0 = j * COL_CHUNK
        z = jnp.dot(hb, w_ref[:, base + c0:base + c0 + COL_CHUNK], preferred_element_type=F32)
        sg_ref[:, c0:c0 + COL_CHUNK] = (z * _sigmoid(z)).astype(sg_ref.dtype)


def _rwkv_proj(x2, shift, tseq, ng, mu, w, w0, w1, w2, a0, a1, a2, vres):
    n = x2.shape[0]
    nb = n // tseq
    tm = min(512, n)
    long_seq = tseq >= tm
    has_vres = vres is not None
    vec = lambda a: a.reshape(1, -1)
    mu8 = jnp.zeros((8, D_MODEL), F32).at[:5].set(mu)
    consts = [vec(ng), mu8, w, vec(w0), w1, w2, vec(a0), a1, a2]
    if has_vres:
        v0, v1, v2, v_first = vres
        consts += [vec(v0), v1, v2]
    if long_seq:
        assert tseq % tm == 0
        steps = tseq // tm
        grid = (nb, steps)
        cmap = lambda b, t: (0, 0)
        row = lambda width: pl.BlockSpec((tm, width), lambda b, t: (b * steps + t, 0))
        x_in = x2.reshape(nb, tseq, D_MODEL)
        x_spec = pl.BlockSpec((1, tm, D_MODEL), lambda b, t: (b, t, 0))
        prev_in = shift.reshape(nb, 1, D_MODEL)
        prev_spec = pl.BlockSpec((1, 1, D_MODEL), lambda b, t: (b, 0, 0))
        h_shape = jax.ShapeDtypeStruct((nb, 1, D_MODEL), F32)
        h_spec = pl.BlockSpec((1, 1, D_MODEL), lambda b, t: (b, 0, 0))
        scratch = [pltpu.VMEM((8, D_MODEL), F32)]
        sem = ("arbitrary", "arbitrary")
    else:
        assert tm % tseq == 0 and tseq & (tseq - 1) == 0
        grid = (n // tm,)
        cmap = lambda i: (0, 0)
        row = lambda width: pl.BlockSpec((tm, width), lambda i: (i, 0))
        x_in, x_spec = x2, row(D_MODEL)
        prev_in = jnp.zeros((nb, tseq, D_MODEL), F32).at[:, 0].set(shift).reshape(n, D_MODEL)
        prev_spec = row(D_MODEL)
        h_shape = jax.ShapeDtypeStruct((n, D_MODEL), F32)
        h_spec = row(D_MODEL)
        scratch = []
        sem = ("arbitrary",)
    in_specs = [x_spec, prev_spec] + [_resident(c.shape, cmap) for c in consts]
    args = [x_in, prev_in] + consts
    if has_vres:
        in_specs.append(row(MIX_W))
        args.append(v_first)
    sd = lambda width, dt: jax.ShapeDtypeStruct((n, width), dt)
    outs = pl.pallas_call(
        functools.partial(_rwkv_proj_kernel, tseq=tseq, long_seq=long_seq, has_vres=has_vres),
        grid=grid,
        in_specs=in_specs,
        out_specs=[row(MIX_W)] * 5 + [row(CROSS_W), row(BRANCH_W), h_spec],
        out_shape=[sd(MIX_W, BF16)] * 3 + [sd(MIX_W, F32)] * 2 + [sd(CROSS_W, F32), sd(BRANCH_W, BF16), h_shape],
        scratch_shapes=scratch,
        compiler_params=_params(*sem),
        name="rwkv_proj",
    )(*args)
    h_out = outs[7]
    new_shift = h_out.reshape(nb, D_MODEL) if long_seq else h_out.reshape(nb, tseq, D_MODEL)[:, -1]
    return list(outs[:7]) + [new_shift]


def _rwkv_scan_kernel(r_ref, k_ref, v_ref, lw_ref, as_ref, s0_ref, par_ref, *rest, rb, tseq, layer, aliased):
    y_ref, sout_all_ref, st_ref = rest[1:] if aliased else rest
    sout_ref = _own_layer(sout_all_ref, layer, aliased)
    tc = min(tseq, CHUNK)
    gseq = CHUNK // tc
    n_chunks = rb // CHUNK
    steps_per_seq = max(1, tseq // rb)
    groups, n_states, width = st_ref.shape[0], st_ref.shape[1], st_ref.shape[2]
    n_levels = _log2(tc)
    narrow = _narrow(tc)
    i = pl.program_id(1)

    units = [(g, p) for g in range(groups) for p in range(width)]
    pairs = range(len(units))
    lane = lax.broadcasted_iota(jnp.int32, (1, LANES), 1)

    def state_rows(head, parity):
        return pl.ds(head * RW_N + parity, RW_N // 2, stride=2)

    @pl.when(i % steps_per_seq == 0)
    def _():
        def load(s, c):
            for g, p in units:
                e, f = s0_ref[g, s, 2 * p], s0_ref[g, s, 2 * p + 1]
                st_ref[g, s, p, state_rows(0, 0), :] = jnp.where(lane < RW_N, e, 0.0)
                st_ref[g, s, p, state_rows(0, 1), :] = jnp.where(lane < RW_N, pltpu.roll(e, RW_N, 1), 0.0)
                st_ref[g, s, p, state_rows(1, 0), :] = jnp.where(lane >= RW_N, pltpu.roll(f, RW_N, 1), 0.0)
                st_ref[g, s, p, state_rows(1, 1), :] = jnp.where(lane >= RW_N, f, 0.0)
            return c
        lax.fori_loop(0, n_states, load, 0)

    t_i = lax.broadcasted_iota(jnp.int32, (CHUNK, CHUNK), 0)
    s_i = lax.broadcasted_iota(jnp.int32, (CHUNK, CHUNK), 1)
    same = (t_i >> _log2(tc)) == (s_i >> _log2(tc))
    tri = jnp.where(same & (s_i <= t_i), 1.0, 0.0).astype(BF16)
    ones_seq = jnp.where(same, 1.0, 0.0).astype(BF16)
    n_i = lax.broadcasted_iota(jnp.int32, (CHUNK, 2 * CHUNK), 0)
    m_i = lax.broadcasted_iota(jnp.int32, (CHUNK, 2 * CHUNK), 1) & (CHUNK - 1)
    same2 = (n_i >> _log2(tc)) == (m_i >> _log2(tc))
    strict2 = jnp.where(same2 & (m_i < n_i), 1.0, 0.0).astype(F32)
    incl2 = jnp.where(same2 & (m_i <= n_i), 1.0, 0.0).astype(F32)
    head0 =jnp.where(lane < RW_N, 1.0, 0.0).astype(F32)
    head1 = 1.0 - head0
    ones_head = jnp.where((lax.broadcasted_iota(jnp.int32, (LANES, LANES), 0) >> _log2(RW_N))
                          == (lax.broadcasted_iota(jnp.int32, (LANES, LANES), 1) >> _log2(RW_N)),
                          1.0, 0.0).astype(F32)
    head0_b, head1_b = head0.astype(BF16), head1.astype(BF16)

    def stack(a):
        ab = a.astype(BF16)
        return jnp.concatenate([ab * head0_b, ab * head1_b], axis=0)

    def head_sum(a):
        s0 = jnp.sum(a * head0, axis=-1, keepdims=True)
        s1 = jnp.sum(a * head1, axis=-1, keepdims=True)
        return jnp.where(lane < RW_N, s0, s1)

    def tokens(ref, rows):
        return [ref[g, rows, p * LANES:(p + 1) * LANES].astype(F32) for g, p in units]

    def par(u, row):
        return par_ref[units[u][1], row:row + 1, :]

    def chunk(j, carry):
        rows = pl.ds(pl.multiple_of(j * CHUNK, CHUNK), CHUNK)
        r, k, v = tokens(r_ref, rows), tokens(k_ref, rows), tokens(v_ref, rows)
        lw, a_gate = tokens(lw_ref, rows), tokens(as_ref, rows)

        c = [_sum_rows(tri, x) for x in lw]
        c_end = [x[CHUNK - 1:CHUNK, :] for x in c] if gseq == 1 else [_sum_rows(ones_seq, x) for x in lw]
        kk = [k[p] * par(p, 0) for p in pairs]
        kk = [x / jnp.maximum(jnp.sqrt(head_sum(x * x)), 1e-12) for x in kk]
        b_vec = [kk[p] * a_gate[p] for p in pairs]
        k_mod = [k[p] * (1.0 + (a_gate[p] - 1.0) * par(p, 1)) for p in pairs]
        e_neg = [jnp.exp(-x) for x in c]
        a_t = [-kk[p] * jnp.exp(c[p] - lw[p]) for p in pairs]
        r_t = [r[p] * jnp.exp(c[p]) for p in pairs]

        lhs = [jnp.concatenate([a_t[p], r_t[p]], axis=0).astype(BF16) for p in pairs]
        rhs = [jnp.concatenate([stack(b_vec[p] * e_neg[p]), stack(k_mod[p] * e_neg[p])], axis=0) for p in pairs]
        big = [lax.dot_general(lhs[p], rhs[p], NT, preferred_element_type=F32) for p in pairs]
        l_ab = [x[0:CHUNK, 0:LANES] * strict2 for x in big]
        l_ak = [x[0:CHUNK, LANES:2 * LANES] * strict2 for x in big]
        m_rb = [x[CHUNK:2 * CHUNK, 0:LANES] * incl2 for x in big]
        m_rk = [x[CHUNK:2 * CHUNK, LANES:2 * LANES] * incl2 for x in big]
        v_s = [stack(x) for x in v]

        ah0, rh0, states = [], [], []
        for gi in range(gseq):
            sl = slice(gi * tc, (gi + 1) * tc)
            sidx = j * gseq + gi if n_states > 1 else 0
            st = [st_ref[g, sidx, p] for g, p in units]
            states.append((sidx, st))
            ar = [jnp.concatenate([a_t[p][sl], r_t[p][sl]], axis=0) for p in pairs]
            prod = [lax.dot_general(narrow(ar[p]), narrow(st[p]), NT, preferred_element_type=F32) for p in pairs]
            ah0.append([x[0:tc] for x in prod])
            rh0.append([x[tc:2 * tc] for x in prod])
        ah0 = [ah0[0][p] if gseq == 1 else jnp.concatenate([part[p] for part in ah0], axis=0) for p in pairs]
        rh0 = [rh0[0][p] if gseq == 1 else jnp.concatenate([part[p] for part in rh0], axis=0) for p in pairs]

        u = [ah0[p] + jnp.dot(l_ak[p].astype(BF16), v_s[p], preferred_element_type=F32) for p in pairs]
        pw = l_ab
        for lev in range(n_levels):
            pb = [m.astype(BF16) for m in pw]
            u = [u[p] + jnp.dot(pb[p], stack(u[p]), preferred_element_type=F32) for p in pairs]
            if lev < n_levels - 1:
                pw = [jnp.dot(pb[p], stack(pw[p]), preferred_element_type=F32) for p in pairs]
        y = [rh0[p] + jnp.dot(jnp.concatenate([m_rb[p], m_rk[p]], axis=1).astype(BF16),
                              jnp.concatenate([stack(u[p]), v_s[p]], axis=0),
                              preferred_element_type=F32) for p in pairs]

        e_end = [jnp.exp(c_end[p] - c[p]) for p in pairs]
        b_hat = [b_vec[p] * e_end[p] for p in pairs]
        k_hat = [k_mod[p] * e_end[p] for p in pairs]
        for gi, (sidx, st) in enumerate(states):
            sl = slice(gi * tc, (gi + 1) * tc)
            uv = [jnp.concatenate([u[p][sl], v[p][sl]], axis=0) for p in pairs]
            bk = [jnp.concatenate([b_hat[p][sl], k_hat[p][sl]], axis=0) for p in pairs]
            upd = [lax.dot_general(narrow(uv[p]), narrow(bk[p]), TN, preferred_element_type=F32) for p in pairs]
            for p in pairs:
                decay = c_end[p] if gseq == 1 else c_end[p][gi * tc:gi * tc + 1, :]
                st_ref[units[p][0], sidx, units[p][1]] = st[p] * jnp.exp(decay) + upd[p] * ones_head

        mean = [head_sum(m) * (1.0 / RW_N) for m in y]
        d = [y[p] - mean[p] for p in pairs]
        var = [head_sum(m * m) * (1.0 / RW_N) for m in d]
        bonus = [head_sum(r[p] * k_mod[p] * par(p, 2)) for p in pairs]
        for p in pairs:
            out = d[p] * lax.rsqrt(var[p] + RW_LNX_EPS) * par(p, 3) + par(p, 4)
            g, pp = units[p]
            y_ref[g, rows, pp * LANES:(pp + 1) * LANES] = (out + bonus[p] * v[p]).astype(y_ref.dtype)
        return carry

    lax.fori_loop(0, n_chunks, chunk, 0)

    @pl.when(i % steps_per_seq == steps_per_seq - 1)
    def _():
        def store(s, c):
            for g, p in units:
                even0, odd0 = st_ref[g, s, p, state_rows(0, 0), :], st_ref[g, s, p, state_rows(0, 1), :]
                even1, odd1 = st_ref[g, s, p, state_rows(1, 0), :], st_ref[g, s, p, state_rows(1, 1), :]
                sout_ref[g, s, 2 * p] = jnp.where(lane < RW_N, even0, pltpu.roll(odd0, RW_N, 1))
                sout_ref[g, s, 2 * p + 1] = jnp.where(lane < RW_N, pltpu.roll(even1, RW_N, 1), odd1)
            _zero_other_layers(sout_all_ref, layer, aliased, s)
            return c
        lax.fori_loop(0, n_states, store, 0)


def _rwkv_scan(r, k, v, lw, a_gate, s_all, layer, s_new_all, par, tseq):
    return _layer_state_call(
        _rwkv_scan_kernel, "rwkv_scan", 2 * SCAN_WIDTH, (RW_N // 2, LANES), RW_PAIRS // SCAN_WIDTH, SCAN_WIDTH * LANES,
        [r, k, v, lw, a_gate], [par], [pl.BlockSpec((SCAN_WIDTH, 8, LANES), lambda p, i: (p, 0, 0))],
        s_all, layer, s_new_all, tseq)


def _cross_attention(q_ref, k_ref, v_ref, rows_per_seq, interleaved):
    n_seq = k_ref.shape[0]
    scale = X_DH ** -0.5
    units = [(s, hh) for s in range(n_seq) for hh in range(X_HEADS)]

    def memory(ref, s, hh):
        if interleaved:
            return ref[s, pl.ds(hh, MEM_LEN, stride=X_HEADS), :].astype(BF16)
        return ref[s, :, hh * X_DH:(hh + 1) * X_DH].astype(BF16)

    q = [q_ref[s * rows_per_seq:(s + 1) * rows_per_seq, hh * X_DH:(hh + 1) * X_DH].astype(BF16) for s, hh in units]
    sc = [lax.dot_general(q[u], memory(k_ref, s, hh), NT, preferred_element_type=F32) * scale
          for u, (s, hh) in enumerate(units)]
    p = [jnp.exp(x - jnp.max(x, axis=-1, keepdims=True)) for x in sc]
    denom = [jnp.sum(x, axis=-1, keepdims=True) for x in p]
    o = [jnp.dot(p[u].astype(BF16), memory(v_ref, s, hh), preferred_element_type=F32) / denom[u]
         for u, (s, hh) in enumerate(units)]
    return [jnp.concatenate(o[s * X_HEADS:(s + 1) * X_HEADS], axis=-1) for s in range(n_seq)]


def _out_kernel(x_ref, mix_ref, xq_ref, k_ref, v_ref, sg_ref, w_ref, fg_ref, o_ref, *, rows_per_seq, interleaved,
                final):
    xo = _cross_attention(xq_ref, k_ref, v_ref, rows_per_seq, interleaved)
    xo = xo[0] if len(xo) == 1 else jnp.concatenate(xo, axis=0)
    sg = sg_ref[...].astype(F32)
    left = (mix_ref[...].astype(F32) * sg[:, :MIX_W]).astype(BF16)
    right = (xo * sg[:, MIX_W:]).astype(BF16)
    x = x_ref[...] + jnp.dot(left, w_ref[0:MIX_W, :], preferred_element_type=F32) \
        + jnp.dot(right, w_ref[MIX_W:BRANCH_W, :], preferred_element_type=F32)
    o_ref[...] = _rms(x, fg_ref[...]) if final else x


def _out_proj(x2, mix, xq, mk_all, mv_all, layer, tseq, sg, w, final_g, final):
    n = x2.shape[0]
    interleaved = mk_all.shape[-1] == X_DH
    rb = min(512, n, 8 * tseq)
    n_seq = max(1, rb // tseq)
    steps_per_seq = max(1, tseq // rb)
    row = lambda width: pl.BlockSpec((rb, width), lambda i: (i, 0))
    mem = pl.BlockSpec((None, n_seq) + mk_all.shape[2:], lambda i: (layer, i // steps_per_seq, 0, 0))
    return pl.pallas_call(
        functools.partial(_out_kernel, rows_per_seq=rb // n_seq, interleaved=interleaved, final=final),
        grid=(n // rb,),
        in_specs=[row(D_MODEL), row(MIX_W), row(CROSS_W), mem, mem, row(BRANCH_W),
                  _resident((BRANCH_W, D_MODEL), lambda i: (0, 0)),
                  _resident((1, D_MODEL), lambda i: (0, 0))],
        out_specs=row(D_MODEL),
        out_shape=jax.ShapeDtypeStruct((n, D_MODEL), F32),
        compiler_params=_params("arbitrary"),
        name="out_proj",
    )(x2, mix, xq, mk_all, mv_all, sg, w, final_g.reshape(1, D_MODEL))


def _trunk(x, mem_k, mem_v, s_hgrn, s_rwkv, s_shift, p):
    nb, tseq, _ = x.shape
    x2 = x.reshape(nb * tseq, D_MODEL)
    rwkv_shape = s_rwkv.shape
    s_rwkv = s_rwkv.reshape(rwkv_shape[:3] + (RW_N // 2, 2 * RW_N))
    new_h, new_r, new_s = None, None, []
    v_first = None
    for i in range(DEPTH):
        j = i // 2
        if i % 2 == 0:
            q, k, gl, v, xq, sg = _hgrn_proj(x2, p["norm_g"][i], p["w_in"][i], p["lbs"][j])
            mix, new_h = _hgrn_scan(q, k, v, gl, s_hgrn, j, new_h, p["hg_onorm_g"][j], tseq)
        else:
            vres = None if v_first is None else (p["rw_v0"][j - 1], p["rw_v1"][j - 1], p["rw_v2"][j - 1], v_first)
            r, k, v, lw, a_gate, xq, sg, shift = _rwkv_proj(
                x2, s_shift[j], tseq, p["norm_g"][i], p["rw_mu"][j], p["w_in"][i],
                p["rw_w0"][j], p["rw_w1"][j], p["rw_w2"][j], p["rw_a0"][j], p["rw_a1"][j], p["rw_a2"][j], vres)
            if v_first is None:
                v_first = v
            mix, new_r = _rwkv_scan(r, k, v, lw, a_gate, s_rwkv, j, new_r, p["rw_par"][j], tseq)
            new_s.append(shift)
        x2 = _out_proj(x2, mix, xq, mem_k, mem_v, i, tseq, sg, p["w_out"][i], p["final_g"], final=(i == DEPTH - 1))
    return x2.reshape(nb, tseq, D_MODEL), new_h, new_r.reshape(rwkv_shape), jnp.stack(new_s)


def kernel(x_prompt, x_sample, mem_prompt, state_hgrn, state_rwkv, state_shift, cache_mem_k, cache_mem_v,
           norm_g, w_in, w_out, mem_norm_g, w_mem_kv, hg_lb, hg_onorm_g, rw_mu, rw_w0, rw_w1, rw_w2,
           rw_a0, rw_a1, rw_a2, rw_v0, rw_v1, rw_v2, rw_kk, rw_ka, rw_rk, rw_lnx_g, rw_lnx_b, final_g):
    n_rwkv = rw_mu.shape[0]
    lbs = jax.nn.softmax(hg_lb.astype(F32), axis=0)
    lbs = jnp.cumsum(lbs, axis=0) - lbs[0]
    par = jnp.stack([rw_kk, rw_ka, rw_rk.reshape(n_rwkv, MIX_W), rw_lnx_g, rw_lnx_b], axis=1)
    par = jnp.concatenate([par, jnp.zeros((n_rwkv, 3, MIX_W), F32)], axis=1)
    par = par.reshape(n_rwkv, 8, RW_PAIRS, LANES).transpose(0, 2, 1, 3)
    p = dict(norm_g=norm_g, w_in=w_in.astype(BF16), w_out=w_out.astype(BF16), lbs=lbs, hg_onorm_g=hg_onorm_g,
             rw_mu=rw_mu, rw_w0=rw_w0, rw_w1=rw_w1.astype(BF16), rw_w2=rw_w2.astype(BF16),
             rw_a0=rw_a0, rw_a1=rw_a1.astype(BF16), rw_a2=rw_a2.astype(BF16),
             rw_v0=rw_v0, rw_v1=rw_v1.astype(BF16), rw_v2=rw_v2.astype(BF16), rw_par=par, final_g=final_g)

    nb, mem_len, _ = mem_prompt.shape
    mk, mv = _memory_kv(mem_prompt.reshape(nb * mem_len, D_MODEL), mem_norm_g, w_mem_kv.astype(BF16))
    mk = mk.reshape(DEPTH, nb, mem_len, CROSS_W)
    mv = mv.reshape(DEPTH, nb, mem_len, CROSS_W)
    z_h = jnp.zeros((state_hgrn.shape[0], nb) + state_hgrn.shape[2:], F32)
    z_r = jnp.zeros((state_rwkv.shape[0], nb) + state_rwkv.shape[2:], F32)
    z_s = jnp.zeros((state_shift.shape[0], nb, D_MODEL), F32)
    y_p, sh_p, sr_p, ss_p = _trunk(x_prompt, mk, mv, z_h, z_r, z_s, p)

    nbs = x_sample.shape[0]
    cmk = cache_mem_k.reshape(DEPTH, nbs, mem_len * X_HEADS, X_DH)
    cmv = cache_mem_v.reshape(DEPTH, nbs, mem_len * X_HEADS, X_DH)
    y_s, sh_s, sr_s, ss_s = _trunk(x_sample, cmk, cmv, state_hgrn, state_rwkv, state_shift, p)
    return (y_p, y_s, sh_p, sr_p, ss_p,
            mk.reshape(DEPTH, nb, mem_len, X_HEADS, X_DH), mv.reshape(DEPTH, nb, mem_len, X_HEADS, X_DH),
            sh_s, sr_s, ss_s)
```

```python
import functools
import math

import jax
import jax.numpy as jnp
from jax import lax
from jax.experimental import pallas as pl
from jax.experimental.pallas import tpu as pltpu

F32 = jnp.float32
BF16 = jnp.bfloat16

D_MODEL = 1024
DEPTH = 4
MIX_W = D_MODEL
HG_HEADS = 8
HG_D = MIX_W // HG_HEADS
RW_N = 64
RW_HEADS = MIX_W // RW_N
RW_PAIRS = RW_HEADS // 2
RW_LNX_EPS = 64e-5
MEM_LEN = 256
X_HEADS = 4
X_DH = 128
CROSS_W = X_HEADS * X_DH
BRANCH_W = MIX_W + CROSS_W
IN_COLS = 3 * MIX_W + CROSS_W + BRANCH_W
NORM_EPS = 1e-6
LOG_FLOOR = 1e-30
NEG_BIG = -1e30

LANES = 128
SUBLANES = 8
CHUNK = 64
COL_CHUNK = 512
VMEM_LIMIT = 56 * 1024 * 1024

NT = (((1,), (1,)), ((), ()))
TN = (((0,), (0,)), ((), ()))


def _params(*sem):
    return pltpu.CompilerParams(dimension_semantics=sem, vmem_limit_bytes=VMEM_LIMIT)


def _resident(shape, index_map):
    return pl.BlockSpec(shape, index_map, pipeline_mode=pl.Buffered(1))


def _rms(x, g):
    return x * lax.rsqrt(jnp.mean(x * x, axis=-1, keepdims=True) + NORM_EPS) * g


def _sigmoid(z):
    return 1.0 / (1.0 + jnp.exp(-z))


def _sum_rows(sel, x):
    hi = x.astype(BF16)
    rest = x - hi.astype(F32)
    mid = rest.astype(BF16)
    low = (rest - mid.astype(F32)).astype(BF16)
    parts = jnp.dot(sel, jnp.concatenate([hi, mid, low], axis=1), preferred_element_type=F32)
    return parts[:, 0:LANES] + parts[:, LANES:2 * LANES] + parts[:, 2 * LANES:3 * LANES]


def _narrow(rows):
    if rows % 16 == 0:
        return lambda a: a.astype(BF16)
    return lambda a: a.astype(BF16).astype(F32)


def _log2(n):
    l = int(math.log2(n))
    assert 1 << l == n, n
    return l


def _memkv_kernel(x_ref, g_ref, w_ref, k_ref, v_ref):
    hb = _rms(x_ref[...], g_ref[...]).astype(BF16)
    kv = jnp.dot(hb, w_ref[...], preferred_element_type=F32)
    k_ref[...] = kv[:, :CROSS_W]
    v_ref[...] = kv[:, CROSS_W:]


def _memory_kv(mem2d, g, w):
    n = mem2d.shape[0]
    tm = min(512, n)
    out = jax.ShapeDtypeStruct((DEPTH, n, CROSS_W), F32)
    return pl.pallas_call(
        _memkv_kernel,
        grid=(DEPTH, n // tm),
        in_specs=[
            pl.BlockSpec((tm, D_MODEL), lambda l, i: (i, 0)),
            pl.BlockSpec((None, 1, D_MODEL), lambda l, i: (l, 0, 0)),
            pl.BlockSpec((None, D_MODEL, 2 * CROSS_W), lambda l, i: (l, 0, 0)),
        ],
        out_specs=[
            pl.BlockSpec((None, tm, CROSS_W), lambda l, i: (l, i, 0)),
            pl.BlockSpec((None, tm, CROSS_W), lambda l, i: (l, i, 0)),
        ],
        out_shape=[out, out],
        compiler_params=_params("arbitrary", "arbitrary"),
        name="memory_kv",
    )(mem2d, g.reshape(DEPTH, 1, D_MODEL), w)


def _hgrn_proj_kernel(x_ref, g_ref, w_ref, lb_ref, q_ref, k_ref, gl_ref, v_ref, xq_ref, sg_ref):
    hb = _rms(x_ref[...], g_ref[...]).astype(BF16)
    for j in range(IN_COLS // COL_CHUNK):
        c0 = j * COL_CHUNK
        z = jnp.dot(hb, w_ref[:, c0:c0 + COL_CHUNK], preferred_element_type=F32)
        if c0 < MIX_W:
            q_ref[:, c0:c0 + COL_CHUNK] = (z * _sigmoid(z)).astype(q_ref.dtype)
        elif c0 < 2 * MIX_W:
            d0 = c0 - MIX_W
            lb = lb_ref[:, d0:d0 + COL_CHUNK]
            f = lb + (1.0 - lb) * _sigmoid(z)
            gl_ref[:, d0:d0 + COL_CHUNK] = jnp.log(jnp.maximum(f, LOG_FLOOR))
            k_ref[:, d0:d0 + COL_CHUNK] = ((1.0 - lb) * _sigmoid(-z)).astype(k_ref.dtype)
        elif c0 < 3 * MIX_W:
            d0 = c0 - 2 * MIX_W
            v_ref[:, d0:d0 + COL_CHUNK] = z.astype(v_ref.dtype)
        elif c0 < 3 * MIX_W + CROSS_W:
            d0 = c0 - 3 * MIX_W
            xq_ref[:, d0:d0 + COL_CHUNK] = z.astype(xq_ref.dtype)
        else:
            d0 = c0 - 3 * MIX_W - CROSS_W
            sg_ref[:, d0:d0 + COL_CHUNK] = (z * _sigmoid(z)).astype(sg_ref.dtype)


def _xq_dtype(tseq):
    return BF16 if tseq % 16 == 0 else F32


def _hgrn_proj(x2, g, w, lb, tseq):
    n = x2.shape[0]
    tm = min(1024, n)
    row = lambda width: pl.BlockSpec((tm, width), lambda i: (i, 0))
    sd = lambda width, dt: jax.ShapeDtypeStruct((n, width), dt)
    return pl.pallas_call(
        _hgrn_proj_kernel,
        grid=(n // tm,),
        in_specs=[
            row(D_MODEL),
            _resident((1, D_MODEL), lambda i: (0, 0)),
            _resident((D_MODEL, IN_COLS), lambda i: (0, 0)),
            _resident((1, MIX_W), lambda i: (0, 0)),
        ],
        out_specs=[row(MIX_W), row(MIX_W), row(MIX_W), row(MIX_W), row(CROSS_W), row(BRANCH_W)],
        out_shape=[sd(MIX_W, BF16), sd(MIX_W, BF16), sd(MIX_W, F32), sd(MIX_W, BF16),
                   sd(CROSS_W, _xq_dtype(tseq)), sd(BRANCH_W, BF16)],
        compiler_params=_params("arbitrary"),
        name="hgrn_proj",
    )(x2, g.reshape(1, D_MODEL), w, lb.reshape(1, MIX_W))


def _own_layer(sout_all_ref, layer, aliased):
    return sout_all_ref if aliased else sout_all_ref.at[layer]


def _zero_other_layers(sout_all_ref, layer, aliased, s):
    if aliased:
        return
    for other in range(sout_all_ref.shape[0]):
        if other != layer:
            for g in range(sout_all_ref.shape[1]):
                sout_all_ref[other, g, s] = jnp.zeros(sout_all_ref.shape[3:], F32)


def _hgrn_scan_kernel(q_ref, k_ref, v_ref, g_ref, s0_ref, og_ref, *rest, rb, tseq, layer, aliased):
    o_ref, sout_all_ref, st_ref = rest[1:] if aliased else rest
    sout_ref = _own_layer(sout_all_ref, layer, aliased)
    tc = min(tseq, CHUNK)
    gseq = CHUNK // tc
    narrow = _narrow(tc)
    n_chunks = rb // CHUNK
    steps_per_seq = max(1, tseq // rb)
    groups, n_states, width = st_ref.shape[0], st_ref.shape[1], st_ref.shape[2]
    i = pl.program_id(1)
    units = [(g, hh) for g in range(groups) for hh in range(width)]

    @pl.when(i % steps_per_seq == 0)
    def _():
        def load(s, c):
            for g, hh in units:
                st_ref[g, s, hh] = s0_ref[g, s, hh].T
            return c
        lax.fori_loop(0, n_states, load, 0)

    t_i = lax.broadcasted_iota(jnp.int32, (CHUNK, CHUNK), 0)
    s_i = lax.broadcasted_iota(jnp.int32, (CHUNK, CHUNK), 1)
    same = (t_i >> _log2(tc)) == (s_i >> _log2(tc))
    tri = jnp.where(same & (s_i <= t_i), 1.0, 0.0).astype(BF16)
    ones_seq = jnp.where(same, 1.0, 0.0).astype(BF16)
    row = lax.broadcasted_iota(jnp.int32, (CHUNK, 1), 0)
    og = og_ref[...]
    halves = [1 << lev for lev in range(_log2(tc))]
    late = [(row & half) != 0 for half in halves]
    pair_mask = [jnp.where(((t_i >> _log2(2 * half)) == (s_i >> _log2(2 * half)))
                           & ((t_i & half) != 0) & ((s_i & half) == 0), 1.0, 0.0).astype(F32) for half in halves]
    small = [half for half in halves if 2 * half < SUBLANES]
    pick = jnp.concatenate([jnp.where(s_i == (t_i & ~(2 * half - 1)) + half - 1, 1.0, 0.0)
                            for half in small], axis=0).astype(BF16)

    heads = range(len(units))

    def tokens(ref, rows):
        return [ref[g, rows, hh * HG_D:(hh + 1) * HG_D] for g, hh in units]

    def chunk(j, carry):
        rows = pl.ds(pl.multiple_of(j * CHUNK, CHUNK), CHUNK)
        q = [x.astype(F32) for x in tokens(q_ref, rows)]
        k = [x.astype(F32) for x in tokens(k_ref, rows)]
        v = [x.astype(F32) for x in tokens(v_ref, rows)]
        g = tokens(g_ref, rows)
        b = [_sum_rows(tri, x) for x in g]
        b_end = [x[CHUNK - 1:CHUNK, :] for x in b] if gseq == 1 else [_sum_rows(ones_seq, x) for x in g]
        q_in = [q[h] * jnp.exp(b[h]) for h in heads]
        k_dec = [k[h] * jnp.exp(b_end[h] - b[h]) for h in heads]
        vb = [x.astype(BF16) for x in v]

        o = []
        for gi in range(gseq):
            sl = slice(gi * tc, (gi + 1) * tc)
            sidx = j * gseq + gi if n_states > 1 else 0
            st = [st_ref[g, sidx, hh] for g, hh in units]
            o.append([lax.dot_general(narrow(q_in[h][sl]), narrow(st[h]), NT, preferred_element_type=F32)
                      for h in heads])
            upd = [lax.dot_general(narrow(v[h][sl]), narrow(k_dec[h][sl]), TN, preferred_element_type=F32)
                   for h in heads]
            for h in heads:
                decay = b_end[h] if gseq == 1 else b_end[h][gi * tc:gi * tc + 1, :]
                st_ref[units[h][0], sidx, units[h][1]] = st[h] * jnp.exp(decay) + upd[h]
        o = [o[0][h] if gseq == 1 else jnp.concatenate([part[h] for part in o], axis=0) for h in heads]

        beta_small = [_sum_rows(pick, x) for x in b]
        att = [None for _ in heads]
        for lev, (half, is_late, mask) in enumerate(zip(halves, late, pair_mask)):
            for h in heads:
                if half in small:
                    beta = beta_small[h][lev * CHUNK:(lev + 1) * CHUNK]
                else:
                    blocks = b[h].reshape(CHUNK // (2 * half), 2 * half, HG_D)
                    beta = jnp.broadcast_to(blocks[:, half - 1:half, :], blocks.shape).reshape(CHUNK, HG_D)
                q_l = (q[h] * jnp.exp(jnp.where(is_late, b[h] - beta, NEG_BIG))).astype(BF16)
                k_l = (k[h] * jnp.exp(jnp.where(is_late, NEG_BIG, beta - b[h]))).astype(BF16)
                part = lax.dot_general(q_l, k_l, NT, preferred_element_type=F32) * mask
                att[h] = part if att[h] is None else att[h] + part
        o = [o[h] + jnp.dot(att[h].astype(BF16), vb[h], preferred_element_type=F32) for h in heads]
        o = [o[h] + jnp.sum(q[h] * k[h], axis=-1, keepdims=True) * v[h] for h in heads]

        for h in heads:
            out = o[h] * lax.rsqrt(jnp.mean(o[h] * o[h], axis=-1, keepdims=True) + NORM_EPS) * og
            g, hh = units[h]
            o_ref[g, rows, hh * HG_D:(hh + 1) * HG_D] = out.astype(o_ref.dtype)
        return carry

    lax.fori_loop(0, n_chunks, chunk, 0)

    @pl.when(i % steps_per_seq == steps_per_seq - 1)
    def _():
        def store(s, c):
            for g, hh in units:
                sout_ref[g, s, hh] = st_ref[g, s, hh].T
            _zero_other_layers(sout_all_ref, layer, aliased, s)
            return c
        lax.fori_loop(0, n_states, store, 0)


SCAN_WIDTH = 8
SCAN_GROUPS = 2
SCAN_ROWS = 256
STATE_BLOCK_BYTES = 4 * 1024 * 1024


def _scan_geometry(n, tseq):
    rb = min(SCAN_ROWS, n)
    if tseq < rb:
        max_states = STATE_BLOCK_BYTES // (SCAN_WIDTH * LANES * LANES * 4)
        rb = min(rb, max(CHUNK, max_states * tseq))
    assert rb % CHUNK == 0 and n % rb == 0
    assert (tseq % rb == 0) or (rb % tseq == 0 and CHUNK % tseq == 0)
    n_states = max(1, rb // tseq)
    steps_per_seq = max(1, tseq // rb)
    return rb, n_states, steps_per_seq


def _layer_state_call(kernel_fn, name, heads, state_tail, grid0, tok_width, tok_inputs, consts, const_specs,
                      s_all, layer, s_new_all, tseq):
    n = tok_inputs[0].shape[0]
    layers, nb = s_all.shape[0], s_all.shape[1]
    groups = SCAN_GROUPS if tseq >= CHUNK else 1
    assert n % groups == 0 and nb % groups == 0
    rb, n_states, steps_per_seq = _scan_geometry(n // groups, tseq)
    grouped = (layers, groups, nb // groups) + s_all.shape[2:]
    tok = pl.BlockSpec((groups, rb, tok_width), lambda h, i: (0, i, h))
    state = pl.BlockSpec((None, groups, n_states, heads) + state_tail,
                         lambda h, i: (layer, 0, i // steps_per_seq, h) + (0,) * len(state_tail))
    in_specs = [tok] * len(tok_inputs) + [state] + const_specs
    args = [t.reshape(groups, n // groups, t.shape[1]) for t in tok_inputs]
    args += [s_all.reshape(grouped)] + consts
    aliases = {}
    if s_new_all is not None:
        in_specs.append(pl.BlockSpec(memory_space=pl.ANY))
        args.append(s_new_all.reshape(grouped))
        aliases = {len(args) - 1: 1}
        state_out = state
    else:
        state_out = pl.BlockSpec((layers, groups, n_states, heads) + state_tail,
                                 lambda h, i: (0, 0, i // steps_per_seq, h) + (0,) * len(state_tail))
    out, s_new = pl.pallas_call(
        functools.partial(kernel_fn, rb=rb, tseq=tseq, layer=layer, aliased=s_new_all is not None),
        grid=(grid0, n // groups // rb),
        in_specs=in_specs,
        out_specs=[tok, state_out],
        out_shape=[jax.ShapeDtypeStruct((groups, n // groups, MIX_W), BF16),
                   jax.ShapeDtypeStruct(grouped, F32)],
        scratch_shapes=[pltpu.VMEM((groups, n_states, SCAN_WIDTH, LANES, LANES), F32)],
        input_output_aliases=aliases,
        compiler_params=_params("arbitrary", "arbitrary"),
        name=name,
    )(*args)
    return out.reshape(n, MIX_W), s_new.reshape(s_all.shape)


def _hgrn_scan(q, k, v, gl, s_all, layer, s_new_all, og, tseq):
    return _layer_state_call(
        _hgrn_scan_kernel, "hgrn_scan", SCAN_WIDTH, (HG_D, HG_D), HG_HEADS // SCAN_WIDTH, SCAN_WIDTH * HG_D,
        [q, k, v, gl], [og.reshape(1, HG_D)], [_resident((1, HG_D), lambda h, i: (0, 0))],
        s_all, layer, s_new_all, tseq)


def _rwkv_proj_kernel(*refs, tseq, long_seq, has_vres):
    it = iter(refs)
    x_ref, prev_ref, ng_ref, mu_ref, w_ref = (next(it) for _ in range(5))
    w0_ref, w1_ref, w2_ref, a0_ref, a1_ref, a2_ref = (next(it) for _ in range(6))
    if has_vres:
        v0_ref, v1_ref, v2_ref, vf_ref = (next(it) for _ in range(4))
    r_ref, k_ref, v_ref, lw_ref, as_ref, xq_ref, sg_ref, h_ref = (next(it) for _ in range(8))
    carry_ref = next(it) if long_seq else None

    x = x_ref[0] if long_seq else x_ref[...]
    tm = x.shape[0]
    h = _rms(x, ng_ref[...])
    shifted = pltpu.roll(h, 1, 0)
    row = lax.broadcasted_iota(jnp.int32, (tm, 1), 0)
    if long_seq:
        tb = pl.program_id(1)

        @pl.when(tb == 0)
        def _():
            carry_ref[0:1, :] = prev_ref[0]

        h_prev = jnp.where(row == 0, carry_ref[0:1, :], shifted)
        carry_ref[0:1, :] = h[tm - 1:tm, :]

        @pl.when(tb == pl.num_programs(1) - 1)
        def _():
            h_ref[0] = h[tm - 1:tm, :]
    else:
        h_prev = jnp.where((row & (tseq - 1)) == 0, prev_ref[...], shifted)
        h_ref[...] = h

    dx = h_prev - h
    hb = h.astype(BF16)
    xr = (h + dx * mu_ref[0:1, :]).astype(BF16)
    xw = (h + dx * mu_ref[1:2, :]).astype(BF16)
    xk = (h + dx * mu_ref[2:3, :]).astype(BF16)
    xv = (h + dx * mu_ref[3:4, :]).astype(BF16)
    xa = (h + dx * mu_ref[4:5, :]).astype(BF16)

    w_mid = jnp.tanh(jnp.dot(xw, w1_ref[...], preferred_element_type=F32)).astype(BF16)
    a_mid = jnp.dot(xa, a1_ref[...], preferred_element_type=F32).astype(BF16)
    if has_vres:
        v_mid = jnp.dot(xv, v1_ref[...], preferred_element_type=F32).astype(BF16)

    for j in range(MIX_W // COL_CHUNK):
        c0 = j * COL_CHUNK
        cols = slice(c0, c0 + COL_CHUNK)
        r_ref[:, cols] = jnp.dot(xr, w_ref[:, c0:c0 + COL_CHUNK], preferred_element_type=F32).astype(r_ref.dtype)
        k_ref[:, cols] = jnp.dot(xk, w_ref[:, MIX_W + c0:MIX_W + c0 + COL_CHUNK],
                                 preferred_element_type=F32).astype(k_ref.dtype)
        vz = jnp.dot(xv, w_ref[:, 2 * MIX_W + c0:2 * MIX_W + c0 + COL_CHUNK], preferred_element_type=F32)
        if has_vres:
            gate = _sigmoid(v0_ref[:, cols] + jnp.dot(v_mid, v2_ref[:, cols], preferred_element_type=F32))
            vz = vz + (vf_ref[:, cols].astype(F32) - vz) * gate
        v_ref[:, cols] = vz.astype(v_ref.dtype)
        u = w0_ref[:, cols] + jnp.dot(w_mid, w2_ref[:, cols], preferred_element_type=F32)
        softplus = jnp.maximum(-u, 0.0) + jnp.log(1.0 + jnp.exp(-jnp.abs(u)))
        lw_ref[:, cols] = -jnp.exp(-softplus - 0.5)
        as_ref[:, cols] = _sigmoid(a0_ref[:, cols] + jnp.dot(a_mid, a2_ref[:, cols], preferred_element_type=F32))

    base = 3 * MIX_W
    for j in range(CROSS_W // COL_CHUNK):
        c0 = j * COL_CHUNK
        xq_ref[:, c0:c0 + COL_CHUNK] = jnp.dot(hb, w_ref[:, base + c0:base + c0 + COL_CHUNK],
                                                preferred_element_type=F32).astype(xq_ref.dtype)
    base = 3 * MIX_W + CROSS_W
    for j in range(BRANCH_W // COL_CHUNK):
        c0 = j * COL_CHUNK
        z = jnp.dot(hb, w_ref[:, base + c0:base + c0 + COL_CHUNK], preferred_element_type=F32)
        sg_ref[:, c0:c0 + COL_CHUNK] = (z * _sigmoid(z)).astype(sg_ref.dtype)


def _rwkv_proj(x2, shift, tseq, ng, mu, w, w0, w1, w2, a0, a1, a2, vres):
    n = x2.shape[0]
    nb = n // tseq
    tm = min(512, n)
    long_seq = tseq >= tm
    has_vres = vres is not None
    vec = lambda a: a.reshape(1, -1)
    mu8 = jnp.zeros((8, D_MODEL), F32).at[:5].set(mu)
    consts = [vec(ng), mu8, w, vec(w0), w1, w2, vec(a0), a1, a2]
    if has_vres:
        v0, v1, v2, v_first = vres
        consts += [vec(v0), v1, v2]
    if long_seq:
        assert tseq % tm == 0
        steps = tseq // tm
        grid = (nb, steps)
        cmap = lambda b, t: (0, 0)
        row = lambda width: pl.BlockSpec((tm, width), lambda b, t: (b * steps + t, 0))
        x_in = x2.reshape(nb, tseq, D_MODEL)
        x_spec = pl.BlockSpec((1, tm, D_MODEL), lambda b, t: (b, t, 0))
        prev_in = shift.reshape(nb, 1, D_MODEL)
        prev_spec = pl.BlockSpec((1, 1, D_MODEL), lambda b, t: (b, 0, 0))
        h_shape = jax.ShapeDtypeStruct((nb, 1, D_MODEL), F32)
        h_spec = pl.BlockSpec((1, 1, D_MODEL), lambda b, t: (b, 0, 0))
        scratch = [pltpu.VMEM((8, D_MODEL), F32)]
        sem = ("arbitrary", "arbitrary")
    else:
        assert tm % tseq == 0 and tseq & (tseq - 1) == 0
        grid = (n // tm,)
        cmap = lambda i: (0, 0)
        row = lambda width: pl.BlockSpec((tm, width), lambda i: (i, 0))
        x_in, x_spec = x2, row(D_MODEL)
        prev_in = jnp.zeros((nb, tseq, D_MODEL), F32).at[:, 0].set(shift).reshape(n, D_MODEL)
        prev_spec = row(D_MODEL)
        h_shape = jax.ShapeDtypeStruct((n, D_MODEL), F32)
        h_spec = row(D_MODEL)
        scratch = []
        sem = ("arbitrary",)
    in_specs = [x_spec, prev_spec] + [_resident(c.shape, cmap) for c in consts]
    args = [x_in, prev_in] + consts
    if has_vres:
        in_specs.append(row(MIX_W))
        args.append(v_first)
    sd = lambda width, dt: jax.ShapeDtypeStruct((n, width), dt)
    outs = pl.pallas_call(
        functools.partial(_rwkv_proj_kernel, tseq=tseq, long_seq=long_seq, has_vres=has_vres),
        grid=grid,
        in_specs=in_specs,
        out_specs=[row(MIX_W)] * 5 + [row(CROSS_W), row(BRANCH_W), h_spec],
        out_shape=[sd(MIX_W, BF16)] * 3 + [sd(MIX_W, F32)] * 2
        + [sd(CROSS_W, _xq_dtype(tseq)), sd(BRANCH_W, BF16), h_shape],
        scratch_shapes=scratch,
        compiler_params=_params(*sem),
        name="rwkv_proj",
    )(*args)
    h_out = outs[7]
    new_shift = h_out.reshape(nb, D_MODEL) if long_seq else h_out.reshape(nb, tseq, D_MODEL)[:, -1]
    return list(outs[:7]) + [new_shift]


def _rwkv_scan_kernel(r_ref, k_ref, v_ref, lw_ref, as_ref, s0_ref, par_ref, *rest, rb, tseq, layer, aliased):
    y_ref, sout_all_ref, st_ref = rest[1:] if aliased else rest
    sout_ref = _own_layer(sout_all_ref, layer, aliased)
    tc = min(tseq, CHUNK)
    gseq = CHUNK // tc
    n_chunks = rb // CHUNK
    steps_per_seq = max(1, tseq // rb)
    groups, n_states, width = st_ref.shape[0], st_ref.shape[1], st_ref.shape[2]
    n_levels = _log2(tc)
    narrow = _narrow(tc)
    i = pl.program_id(1)

    units = [(g, p) for g in range(groups) for p in range(width)]
    pairs = range(len(units))
    lane = lax.broadcasted_iota(jnp.int32, (1, LANES), 1)

    def state_rows(head, parity):
        return pl.ds(head * RW_N + parity, RW_N // 2, stride=2)

    @pl.when(i % steps_per_seq == 0)
    def _():
        def load(s, c):
            for g, p in units:
                e, f = s0_ref[g, s, 2 * p], s0_ref[g, s, 2 * p + 1]
                st_ref[g, s, p, state_rows(0, 0), :] = jnp.where(lane < RW_N, e, 0.0)
                st_ref[g, s, p, state_rows(0, 1), :] = jnp.where(lane < RW_N, pltpu.roll(e, RW_N, 1), 0.0)
                st_ref[g, s, p, state_rows(1, 0), :] = jnp.where(lane >= RW_N, pltpu.roll(f, RW_N, 1), 0.0)
                st_ref[g, s, p, state_rows(1, 1), :] = jnp.where(lane >= RW_N, f, 0.0)
            return c
        lax.fori_loop(0, n_states, load, 0)

    t_i = lax.broadcasted_iota(jnp.int32, (CHUNK, CHUNK), 0)
    s_i = lax.broadcasted_iota(jnp.int32, (CHUNK, CHUNK), 1)
    same = (t_i >> _log2(tc)) == (s_i >> _log2(tc))
    tri = jnp.where(same & (s_i <= t_i), 1.0, 0.0).astype(BF16)
    ones_seq = jnp.where(same, 1.0, 0.0).astype(BF16)
    n_i = lax.broadcasted_iota(jnp.int32, (CHUNK, 2 * CHUNK), 0)
    m_i = lax.broadcasted_iota(jnp.int32, (CHUNK, 2 * CHUNK), 1) & (CHUNK - 1)
    same2 = (n_i >> _log2(tc)) == (m_i >> _log2(tc))
    strict2 = jnp.where(same2 & (m_i < n_i), 1.0, 0.0).astype(F32)
    incl2 = jnp.where(same2 & (m_i <= n_i), 1.0, 0.0).astype(F32)
    incl_cat = jnp.concatenate([incl2, incl2], axis=1)
    head0 =jnp.where(lane < RW_N, 1.0, 0.0).astype(F32)
    head1 = 1.0 - head0
    ones_head = jnp.where((lax.broadcasted_iota(jnp.int32, (LANES, LANES), 0) >> _log2(RW_N))
                          == (lax.broadcasted_iota(jnp.int32, (LANES, LANES), 1) >> _log2(RW_N)),
                          1.0, 0.0).astype(F32)
    head0_b, head1_b = head0.astype(BF16), head1.astype(BF16)

    def stack(a):
        ab = a.astype(BF16)
        return jnp.concatenate([ab * head0_b, ab * head1_b], axis=0)

    def head_sum(a):
        s0 = jnp.sum(a * head0, axis=-1, keepdims=True)
        s1 = jnp.sum(a * head1, axis=-1, keepdims=True)
        return jnp.where(lane < RW_N, s0, s1)

    def tokens(ref, rows):
        return [ref[g, rows, p * LANES:(p + 1) * LANES].astype(F32) for g, p in units]

    def par(u, row):
        return par_ref[units[u][1], row:row + 1, :]

    def chunk(j, carry):
        rows = pl.ds(pl.multiple_of(j * CHUNK, CHUNK), CHUNK)
        r, k, v = tokens(r_ref, rows), tokens(k_ref, rows), tokens(v_ref, rows)
        lw, a_gate = tokens(lw_ref, rows), tokens(as_ref, rows)

        c = [_sum_rows(tri, x) for x in lw]
        c_end = [x[CHUNK - 1:CHUNK, :] for x in c] if gseq == 1 else [_sum_rows(ones_seq, x) for x in lw]
        kk = [k[p] * par(p, 0) for p in pairs]
        kk = [x * lax.rsqrt(jnp.maximum(head_sum(x * x), 1e-24)) for x in kk]
        b_vec = [kk[p] * a_gate[p] for p in pairs]
        k_mod = [k[p] * (1.0 + (a_gate[p] - 1.0) * par(p, 1)) for p in pairs]
        bonus = [head_sum(r[p] * k_mod[p] * par(p, 2)) for p in pairs]
        e_neg = [jnp.exp(-x) for x in c]
        e_end = [jnp.exp(c_end[p] - c[p]) for p in pairs]
        b_hat = [narrow(b_vec[p] * e_end[p]) for p in pairs]
        k_hat = [narrow(k_mod[p] * e_end[p]) for p in pairs]
        a_t = [-kk[p] * jnp.exp(c[p] - lw[p]) for p in pairs]
        r_t = [r[p] * jnp.exp(c[p]) for p in pairs]

        lhs = [jnp.concatenate([a_t[p], r_t[p]], axis=0).astype(BF16) for p in pairs]
        rhs = [jnp.concatenate([stack(b_vec[p] * e_neg[p]), stack(k_mod[p] * e_neg[p])], axis=0) for p in pairs]
        big = [lax.dot_general(lhs[p], rhs[p], NT, preferred_element_type=F32) for p in pairs]
        l_ab = [x[0:CHUNK, 0:LANES] * strict2 for x in big]
        l_ak = [(x[0:CHUNK, LANES:2 * LANES] * strict2).astype(BF16) for x in big]
        m_cat = [(x[CHUNK:2 * CHUNK, :] * incl_cat).astype(BF16) for x in big]
        v_s = [stack(x) for x in v]

        ah0, rh0 = [], []
        for gi in range(gseq):
            sl = slice(gi * tc, (gi + 1) * tc)
            sidx = j * gseq + gi if n_states > 1 else 0
            st = [st_ref[g, sidx, p] for g, p in units]
            ar =[jnp.concatenate([a_t[p][sl], r_t[p][sl]], axis=0) for p in pairs]
            prod = [lax.dot_general(narrow(ar[p]), narrow(st[p]), NT, preferred_element_type=F32) for p in pairs]
            ah0.append([x[0:tc] for x in prod])
            rh0.append([x[tc:2 * tc] for x in prod])
        ah0 = [ah0[0][p] if gseq == 1 else jnp.concatenate([part[p] for part in ah0], axis=0) for p in pairs]
        rh0 = [rh0[0][p] if gseq == 1 else jnp.concatenate([part[p] for part in rh0], axis=0) for p in pairs]

        u = [ah0[p] + jnp.dot(l_ak[p], v_s[p], preferred_element_type=F32) for p in pairs]
        pw = l_ab
        for lev in range(n_levels):
            pb = [m.astype(BF16) for m in pw]
            u = [u[p] + jnp.dot(pb[p], stack(u[p]), preferred_element_type=F32) for p in pairs]
            if lev < n_levels - 1:
                pw = [jnp.dot(pb[p], stack(pw[p]), preferred_element_type=F32) for p in pairs]
        y = [rh0[p] + jnp.dot(m_cat[p], jnp.concatenate([stack(u[p]), v_s[p]], axis=0),
                              preferred_element_type=F32) for p in pairs]

        v = tokens(v_ref, rows)
        for gi in range(gseq):
            sl = slice(gi * tc, (gi + 1) * tc)
            sidx = j * gseq + gi if n_states > 1 else 0
            uv = [jnp.concatenate([u[p][sl], v[p][sl]], axis=0) for p in pairs]
            bk = [jnp.concatenate([b_hat[p][sl], k_hat[p][sl]], axis=0) for p in pairs]
            upd = [lax.dot_general(narrow(uv[p]), bk[p], TN, preferred_element_type=F32) for p in pairs]
            for p, (g, pp) in enumerate(units):
                decay = c_end[p] if gseq == 1 else c_end[p][gi * tc:gi * tc + 1, :]
                st_ref[g, sidx, pp] = st_ref[g, sidx, pp] * jnp.exp(decay) + upd[p] * ones_head

        mean = [head_sum(m) * (1.0 / RW_N) for m in y]
        d = [y[p] - mean[p] for p in pairs]
        var = [head_sum(m * m) * (1.0 / RW_N) for m in d]
        for p in pairs:
            out = d[p] * lax.rsqrt(var[p] + RW_LNX_EPS) * par(p, 3) + par(p, 4)
            g, pp = units[p]
            y_ref[g, rows, pp * LANES:(pp + 1) * LANES] = (out + bonus[p] * v[p]).astype(y_ref.dtype)
        return carry

    lax.fori_loop(0, n_chunks, chunk, 0)

    @pl.when(i % steps_per_seq == steps_per_seq - 1)
    def _():
        def store(s, c):
            for g, p in units:
                even0, odd0 = st_ref[g, s, p, state_rows(0, 0), :], st_ref[g, s, p, state_rows(0, 1), :]
                even1, odd1 = st_ref[g, s, p, state_rows(1, 0), :], st_ref[g, s, p, state_rows(1, 1), :]
                sout_ref[g, s, 2 * p] = jnp.where(lane < RW_N, even0, pltpu.roll(odd0, RW_N, 1))
                sout_ref[g, s, 2 * p + 1] = jnp.where(lane < RW_N, pltpu.roll(even1, RW_N, 1), odd1)
            _zero_other_layers(sout_all_ref, layer, aliased, s)
            return c
        lax.fori_loop(0, n_states, store, 0)


def _rwkv_scan(r, k, v, lw, a_gate, s_all, layer, s_new_all, par, tseq):
    return _layer_state_call(
        _rwkv_scan_kernel, "rwkv_scan", 2 * SCAN_WIDTH, (RW_N // 2, LANES), RW_PAIRS // SCAN_WIDTH, SCAN_WIDTH * LANES,
        [r, k, v, lw, a_gate], [par], [pl.BlockSpec((SCAN_WIDTH, 8, LANES), lambda p, i: (p, 0, 0))],
        s_all, layer, s_new_all, tseq)


def _cross_attention(q_ref, k_ref, v_ref, rows_per_seq, interleaved):
    n_seq = k_ref.shape[0]
    scale = X_DH ** -0.5
    units = [(s, hh) for s in range(n_seq) for hh in range(X_HEADS)]

    def memory(ref, s, hh):
        if interleaved:
            return ref[s, pl.ds(hh, MEM_LEN, stride=X_HEADS), :].astype(BF16)
        return ref[s, :, hh * X_DH:(hh + 1) * X_DH].astype(BF16)

    q = [q_ref[s * rows_per_seq:(s + 1) * rows_per_seq, hh * X_DH:(hh + 1) * X_DH].astype(BF16) for s, hh in units]
    sc = [lax.dot_general(q[u], memory(k_ref, s, hh), NT, preferred_element_type=F32) * scale
          for u, (s, hh) in enumerate(units)]
    p = [jnp.exp(x - jnp.max(x, axis=-1, keepdims=True)) for x in sc]
    denom = [jnp.sum(x, axis=-1, keepdims=True) for x in p]
    o = [jnp.dot(p[u].astype(BF16), memory(v_ref, s, hh), preferred_element_type=F32) / denom[u]
         for u, (s, hh) in enumerate(units)]
    return [jnp.concatenate(o[s * X_HEADS:(s + 1) * X_HEADS], axis=-1) for s in range(n_seq)]


def _out_kernel(x_ref, mix_ref, xq_ref, k_ref, v_ref, sg_ref, w_ref, fg_ref, o_ref, *, rows_per_seq, interleaved,
                final):
    xo = _cross_attention(xq_ref, k_ref, v_ref, rows_per_seq, interleaved)
    xo = xo[0] if len(xo) == 1 else jnp.concatenate(xo, axis=0)
    sg = sg_ref[...].astype(F32)
    left = (mix_ref[...].astype(F32) * sg[:, :MIX_W]).astype(BF16)
    right = (xo * sg[:, MIX_W:]).astype(BF16)
    x = x_ref[...] + jnp.dot(left, w_ref[0:MIX_W, :], preferred_element_type=F32) \
        + jnp.dot(right, w_ref[MIX_W:BRANCH_W, :], preferred_element_type=F32)
    o_ref[...] = _rms(x, fg_ref[...]) if final else x


def _out_proj(x2, mix, xq, mk_all, mv_all, layer, tseq, sg, w, final_g, final):
    n = x2.shape[0]
    interleaved = mk_all.shape[-1] == X_DH
    rb = min(512, n, 8 * tseq)
    n_seq = max(1, rb // tseq)
    steps_per_seq = max(1, tseq // rb)
    row = lambda width: pl.BlockSpec((rb, width), lambda i: (i, 0))
    mem = pl.BlockSpec((None, n_seq) + mk_all.shape[2:], lambda i: (layer, i // steps_per_seq, 0, 0))
    return pl.pallas_call(
        functools.partial(_out_kernel, rows_per_seq=rb // n_seq, interleaved=interleaved, final=final),
        grid=(n // rb,),
        in_specs=[row(D_MODEL), row(MIX_W), row(CROSS_W), mem, mem, row(BRANCH_W),
                  _resident((BRANCH_W, D_MODEL), lambda i: (0, 0)),
                  _resident((1, D_MODEL), lambda i: (0, 0))],
        out_specs=row(D_MODEL),
        out_shape=jax.ShapeDtypeStruct((n, D_MODEL), F32),
        compiler_params=_params("arbitrary"),
        name="out_proj",
    )(x2, mix, xq, mk_all, mv_all, sg, w, final_g.reshape(1, D_MODEL))


def _trunk(x, mem_k, mem_v, s_hgrn, s_rwkv, s_shift, p):
    nb, tseq, _ = x.shape
    x2 = x.reshape(nb * tseq, D_MODEL)
    rwkv_shape = s_rwkv.shape
    s_rwkv = s_rwkv.reshape(rwkv_shape[:3] + (RW_N // 2, 2 * RW_N))
    new_h, new_r, new_s = None, None, []
    v_first = None
    for i in range(DEPTH):
        j = i // 2
        if i % 2 == 0:
            q, k, gl, v, xq, sg = _hgrn_proj(x2, p["norm_g"][i], p["w_in"][i], p["lbs"][j], tseq)
            mix, new_h = _hgrn_scan(q, k, v, gl, s_hgrn, j, new_h, p["hg_onorm_g"][j], tseq)
        else:
            vres = None if v_first is None else (p["rw_v0"][j - 1], p["rw_v1"][j - 1], p["rw_v2"][j - 1], v_first)
            r, k, v, lw, a_gate, xq, sg, shift = _rwkv_proj(
                x2, s_shift[j], tseq, p["norm_g"][i], p["rw_mu"][j], p["w_in"][i],
                p["rw_w0"][j], p["rw_w1"][j], p["rw_w2"][j], p["rw_a0"][j], p["rw_a1"][j], p["rw_a2"][j], vres)
            if v_first is None:
                v_first = v
            mix, new_r = _rwkv_scan(r, k, v, lw, a_gate, s_rwkv, j, new_r, p["rw_par"][j], tseq)
            new_s.append(shift)
        x2 = _out_proj(x2, mix, xq, mem_k, mem_v, i, tseq, sg, p["w_out"][i], p["final_g"], final=(i == DEPTH - 1))
    return x2.reshape(nb, tseq, D_MODEL), new_h, new_r.reshape(rwkv_shape), jnp.stack(new_s)


def kernel(x_prompt, x_sample, mem_prompt, state_hgrn, state_rwkv, state_shift, cache_mem_k, cache_mem_v,
           norm_g, w_in, w_out, mem_norm_g, w_mem_kv, hg_lb, hg_onorm_g, rw_mu, rw_w0, rw_w1, rw_w2,
           rw_a0, rw_a1, rw_a2, rw_v0, rw_v1, rw_v2, rw_kk, rw_ka, rw_rk, rw_lnx_g, rw_lnx_b, final_g):
    n_rwkv = rw_mu.shape[0]
    lbs = jax.nn.softmax(hg_lb.astype(F32), axis=0)
    lbs = jnp.cumsum(lbs, axis=0) - lbs[0]
    par = jnp.stack([rw_kk, rw_ka, rw_rk.reshape(n_rwkv, MIX_W), rw_lnx_g, rw_lnx_b], axis=1)
    par = jnp.concatenate([par, jnp.zeros((n_rwkv, 3, MIX_W), F32)], axis=1)
    par = par.reshape(n_rwkv, 8, RW_PAIRS, LANES).transpose(0, 2, 1, 3)
    p = dict(norm_g=norm_g, w_in=w_in.astype(BF16), w_out=w_out.astype(BF16), lbs=lbs, hg_onorm_g=hg_onorm_g,
             rw_mu=rw_mu, rw_w0=rw_w0, rw_w1=rw_w1.astype(BF16), rw_w2=rw_w2.astype(BF16),
             rw_a0=rw_a0, rw_a1=rw_a1.astype(BF16), rw_a2=rw_a2.astype(BF16),
             rw_v0=rw_v0, rw_v1=rw_v1.astype(BF16), rw_v2=rw_v2.astype(BF16), rw_par=par, final_g=final_g)

    nb, mem_len, _ = mem_prompt.shape
    mk, mv = _memory_kv(mem_prompt.reshape(nb * mem_len, D_MODEL), mem_norm_g, w_mem_kv.astype(BF16))
    mk = mk.reshape(DEPTH, nb, mem_len, CROSS_W)
    mv = mv.reshape(DEPTH, nb, mem_len, CROSS_W)
    z_h = jnp.zeros((state_hgrn.shape[0], nb) + state_hgrn.shape[2:], F32)
    z_r = jnp.zeros((state_rwkv.shape[0], nb) + state_rwkv.shape[2:], F32)
    z_s = jnp.zeros((state_shift.shape[0], nb, D_MODEL), F32)
    y_p, sh_p, sr_p, ss_p = _trunk(x_prompt, mk, mv, z_h, z_r, z_s, p)

    nbs = x_sample.shape[0]
    cmk = cache_mem_k.reshape(DEPTH, nbs, mem_len * X_HEADS, X_DH)
    cmv = cache_mem_v.reshape(DEPTH, nbs, mem_len * X_HEADS, X_DH)
    y_s, sh_s, sr_s, ss_s = _trunk(x_sample, cmk, cmv, state_hgrn, state_rwkv, state_shift, p)
    return (y_p, y_s, sh_p, sr_p, ss_p,
            mk.reshape(DEPTH, nb, mem_len, X_HEADS, X_DH), mv.reshape(DEPTH, nb, mem_len, X_HEADS, X_DH),
            sh_s, sr_s, ss_s)
```

```python
import functools
import math

import jax
import jax.numpy as jnp
from jax import lax
from jax.experimental import pallas as pl
from jax.experimental.pallas import tpu as pltpu

F32 = jnp.float32
BF16 = jnp.bfloat16

D_MODEL = 1024
DEPTH = 4
MIX_W = D_MODEL
HG_HEADS = 8
HG_D = MIX_W // HG_HEADS
RW_N = 64
RW_HEADS = MIX_W // RW_N
RW_PAIRS = RW_HEADS // 2
RW_LNX_EPS = 64e-5
MEM_LEN = 256
X_HEADS = 4
X_DH = 128
CROSS_W = X_HEADS * X_DH
BRANCH_W = MIX_W + CROSS_W
IN_COLS = 3 * MIX_W + CROSS_W + BRANCH_W
NORM_EPS = 1e-6
LOG_FLOOR = 1e-30
NEG_BIG = -1e30

LANES = 128
SUBLANES = 8
CHUNK = 64
COL_CHUNK = 512
VMEM_LIMIT = 56 * 1024 * 1024

NT = (((1,), (1,)), ((), ()))
TN = (((0,), (0,)), ((), ()))


def _params(*sem):
    return pltpu.CompilerParams(dimension_semantics=sem, vmem_limit_bytes=VMEM_LIMIT)


def _resident(shape, index_map):
    return pl.BlockSpec(shape, index_map, pipeline_mode=pl.Buffered(1))


def _rms(x, g):
    return x * lax.rsqrt(jnp.mean(x * x, axis=-1, keepdims=True) + NORM_EPS) * g


def _sigmoid(z):
    return 1.0 / (1.0 + jnp.exp(-z))


def _split3(x):
    hi = x.astype(BF16)
    rest = x - hi.astype(F32)
    mid = rest.astype(BF16)
    low = (rest - mid.astype(F32)).astype(BF16)
    return jnp.concatenate([hi, mid, low], axis=1)


def _sum_rows(sel, x):
    parts = jnp.dot(sel, _split3(x), preferred_element_type=F32)
    return parts[:, 0:LANES] + parts[:, LANES:2 * LANES] + parts[:, 2 * LANES:3 * LANES]


def _narrow(rows):
    if rows % 16 == 0:
        return lambda a: a.astype(BF16)
    return lambda a: a.astype(BF16).astype(F32)


def _log2(n):
    l = int(math.log2(n))
    assert 1 << l == n, n
    return l


def _memkv_kernel(x_ref, g_ref, w_ref, k_ref, v_ref):
    hb = _rms(x_ref[...], g_ref[...]).astype(BF16)
    kv = jnp.dot(hb, w_ref[...], preferred_element_type=F32)
    k_ref[...] = kv[:, :CROSS_W]
    v_ref[...] = kv[:, CROSS_W:]


def _memory_kv(mem2d, g, w):
    n = mem2d.shape[0]
    tm = min(512, n)
    out = jax.ShapeDtypeStruct((DEPTH, n, CROSS_W), F32)
    return pl.pallas_call(
        _memkv_kernel,
        grid=(DEPTH, n // tm),
        in_specs=[
            pl.BlockSpec((tm, D_MODEL), lambda l, i: (i, 0)),
            pl.BlockSpec((None, 1, D_MODEL), lambda l, i: (l, 0, 0)),
            pl.BlockSpec((None, D_MODEL, 2 * CROSS_W), lambda l, i: (l, 0, 0)),
        ],
        out_specs=[
            pl.BlockSpec((None, tm, CROSS_W), lambda l, i: (l, i, 0)),
            pl.BlockSpec((None, tm, CROSS_W), lambda l, i: (l, i, 0)),
        ],
        out_shape=[out, out],
        compiler_params=_params("arbitrary", "arbitrary"),
        name="memory_kv",
    )(mem2d, g.reshape(DEPTH, 1, D_MODEL), w)


def _hgrn_proj_kernel(x_ref, g_ref, w_ref, lb_ref, q_ref, k_ref, gl_ref, v_ref, xq_ref, sg_ref):
    hb = _rms(x_ref[...], g_ref[...]).astype(BF16)
    for j in range(IN_COLS // COL_CHUNK):
        c0 = j * COL_CHUNK
        z = jnp.dot(hb, w_ref[:, c0:c0 + COL_CHUNK], preferred_element_type=F32)
        if c0 < MIX_W:
            q_ref[:, c0:c0 + COL_CHUNK] = (z * _sigmoid(z)).astype(q_ref.dtype)
        elif c0 < 2 * MIX_W:
            d0 = c0 - MIX_W
            lb = lb_ref[:, d0:d0 + COL_CHUNK]
            f = lb + (1.0 - lb) * _sigmoid(z)
            gl_ref[:, d0:d0 + COL_CHUNK] = jnp.log(jnp.maximum(f, LOG_FLOOR))
            k_ref[:, d0:d0 + COL_CHUNK] = ((1.0 - lb) * _sigmoid(-z)).astype(k_ref.dtype)
        elif c0 < 3 * MIX_W:
            d0 = c0 - 2 * MIX_W
            v_ref[:, d0:d0 + COL_CHUNK] = z.astype(v_ref.dtype)
        elif c0 < 3 * MIX_W + CROSS_W:
            d0 = c0 - 3 * MIX_W
            xq_ref[:, d0:d0 + COL_CHUNK] = z.astype(xq_ref.dtype)
        else:
            d0 = c0 - 3 * MIX_W - CROSS_W
            sg_ref[:, d0:d0 + COL_CHUNK] = (z * _sigmoid(z)).astype(sg_ref.dtype)


def _xq_dtype(tseq):
    return BF16 if tseq % 16 == 0 else F32


def _hgrn_proj(x2, g, w, lb, tseq):
    n = x2.shape[0]
    tm = min(1024, n)
    row = lambda width: pl.BlockSpec((tm, width), lambda i: (i, 0))
    sd = lambda width, dt: jax.ShapeDtypeStruct((n, width), dt)
    return pl.pallas_call(
        _hgrn_proj_kernel,
        grid=(n // tm,),
        in_specs=[
            row(D_MODEL),
            _resident((1, D_MODEL), lambda i: (0, 0)),
            _resident((D_MODEL, IN_COLS), lambda i: (0, 0)),
            _resident((1, MIX_W), lambda i: (0, 0)),
        ],
        out_specs=[row(MIX_W), row(MIX_W), row(MIX_W), row(MIX_W), row(CROSS_W), row(BRANCH_W)],
        out_shape=[sd(MIX_W, BF16), sd(MIX_W, BF16), sd(MIX_W, F32), sd(MIX_W, BF16),
                   sd(CROSS_W, _xq_dtype(tseq)), sd(BRANCH_W, BF16)],
        compiler_params=_params("arbitrary"),
        name="hgrn_proj",
    )(x2, g.reshape(1, D_MODEL), w, lb.reshape(1, MIX_W))


def _own_layer(sout_all_ref, layer, aliased):
    return sout_all_ref if aliased else sout_all_ref.at[layer]


def _zero_other_layers(sout_all_ref, layer, aliased, s):
    if aliased:
        return
    for other in range(sout_all_ref.shape[0]):
        if other != layer:
            for g in range(sout_all_ref.shape[1]):
                sout_all_ref[other, g, s] = jnp.zeros(sout_all_ref.shape[3:], F32)


def _hgrn_scan_kernel(q_ref, k_ref, v_ref, g_ref, s0_ref, og_ref, *rest, rb, tseq, layer, aliased):
    o_ref, sout_all_ref, st_ref = rest[1:] if aliased else rest
    sout_ref = _own_layer(sout_all_ref, layer, aliased)
    tc = min(tseq, CHUNK)
    gseq = CHUNK // tc
    narrow = _narrow(tc)
    n_chunks = rb // CHUNK
    steps_per_seq = max(1, tseq // rb)
    groups, n_states, width = st_ref.shape[0], st_ref.shape[1], st_ref.shape[2]
    i = pl.program_id(1)
    units = [(g, hh) for g in range(groups) for hh in range(width)]
    single = tseq < CHUNK

    if not single:
        @pl.when(i % steps_per_seq == 0)
        def _():
            def load(s, c):
                for g, hh in units:
                    st_ref[g, s, hh] = s0_ref[g, s, hh].T
                return c
            lax.fori_loop(0, n_states, load, 0)

    t_i = lax.broadcasted_iota(jnp.int32, (CHUNK, CHUNK), 0)
    s_i = lax.broadcasted_iota(jnp.int32, (CHUNK, CHUNK), 1)
    same = (t_i >> _log2(tc)) == (s_i >> _log2(tc))
    tri = jnp.where(same & (s_i <= t_i), 1.0, 0.0).astype(BF16)
    ones_seq = jnp.where(same, 1.0, 0.0).astype(BF16)
    row = lax.broadcasted_iota(jnp.int32, (CHUNK, 1), 0)
    og = og_ref[...]
    halves = [1 << lev for lev in range(_log2(tc))]
    late = [(row & half) != 0 for half in halves]
    pair_mask = [jnp.where(((t_i >> _log2(2 * half)) == (s_i >> _log2(2 * half)))
                           & ((t_i & half) != 0) & ((s_i & half) == 0), 1.0, 0.0).astype(F32) for half in halves]
    small = [half for half in halves if 2 * half < SUBLANES]
    pick = jnp.concatenate([jnp.where(s_i == (t_i & ~(2 * half - 1)) + half - 1, 1.0, 0.0)
                            for half in small], axis=0).astype(BF16)

    if single:
        piece_row = lax.broadcasted_iota(jnp.int32, (3 * HG_D, HG_D), 0) & (HG_D - 1)
        spread = [jnp.where(piece_row == gi * tc, 1.0, 0.0).astype(BF16) for gi in range(gseq)]

    heads = range(len(units))

    def tokens(ref, rows):
        return [ref[g, rows, hh * HG_D:(hh + 1) * HG_D] for g, hh in units]

    def chunk(j, carry):
        rows = pl.ds(pl.multiple_of(j * CHUNK, CHUNK), CHUNK)
        q = [x.astype(F32) for x in tokens(q_ref, rows)]
        k = [x.astype(F32) for x in tokens(k_ref, rows)]
        v = [x.astype(F32) for x in tokens(v_ref, rows)]
        g = tokens(g_ref, rows)
        b = [_sum_rows(tri, x) for x in g]
        b_end = [x[CHUNK - 1:CHUNK, :] for x in b] if gseq == 1 else [_sum_rows(ones_seq, x) for x in g]
        q_in = [q[h] * jnp.exp(b[h]) for h in heads]
        k_dec = [k[h] * jnp.exp(b_end[h] - b[h]) for h in heads]
        vb = [x.astype(BF16) for x in v]

        o = []
        if single:
            decay_t = [_split3(jnp.concatenate([jnp.exp(b_end[h]), jnp.zeros((HG_D - CHUNK, HG_D), F32)], axis=0).T)
                       for h in heads]
        for gi in range(gseq):
            sl = slice(gi * tc, (gi + 1) * tc)
            sidx = j * gseq + gi if n_states > 1 else 0
            if single:
                st = [s0_ref[g, sidx, hh] for g, hh in units]
                o.append([jnp.dot(narrow(q_in[h][sl]), narrow(st[h]), preferred_element_type=F32) for h in heads])
                upd = [lax.dot_general(narrow(k_dec[h][sl]), narrow(v[h][sl]), TN, preferred_element_type=F32)
                       for h in heads]
                decay = [jnp.dot(decay_t[h], spread[gi], preferred_element_type=F32) for h in heads]
                for h, (g, hh) in enumerate(units):
                    sout_ref[g, sidx, hh] = st[h] * decay[h] + upd[h]
                _zero_other_layers(sout_all_ref, layer, aliased, sidx)
                continue
            st = [st_ref[g, sidx, hh] for g, hh in units]
            o.append([lax.dot_general(narrow(q_in[h][sl]), narrow(st[h]), NT, preferred_element_type=F32)
                      for h in heads])
            upd = [lax.dot_general(narrow(v[h][sl]), narrow(k_dec[h][sl]), TN, preferred_element_type=F32)
                   for h in heads]
            for h in heads:
                decay = b_end[h] if gseq == 1 else b_end[h][gi * tc:gi * tc + 1, :]
                st_ref[units[h][0], sidx, units[h][1]] = st[h] * jnp.exp(decay) + upd[h]
        o = [o[0][h] if gseq == 1 else jnp.concatenate([part[h] for part in o], axis=0) for h in heads]

        beta_small = [_sum_rows(pick, x) for x in b]
        att = [None for _ in heads]
        for lev, (half, is_late, mask) in enumerate(zip(halves, late, pair_mask)):
            for h in heads:
                if half in small:
                    beta = beta_small[h][lev * CHUNK:(lev + 1) * CHUNK]
                else:
                    blocks = b[h].reshape(CHUNK // (2 * half), 2 * half, HG_D)
                    beta = jnp.broadcast_to(blocks[:, half - 1:half, :], blocks.shape).reshape(CHUNK, HG_D)
                q_l = (q[h] * jnp.exp(jnp.where(is_late, b[h] - beta, NEG_BIG))).astype(BF16)
                k_l = (k[h] * jnp.exp(jnp.where(is_late, NEG_BIG, beta - b[h]))).astype(BF16)
                part = lax.dot_general(q_l, k_l, NT, preferred_element_type=F32) * mask
                att[h] = part if att[h] is None else att[h] + part
        o = [o[h] + jnp.dot(att[h].astype(BF16), vb[h], preferred_element_type=F32) for h in heads]
        o = [o[h] + jnp.sum(q[h] * k[h], axis=-1, keepdims=True) * v[h] for h in heads]

        for h in heads:
            out = o[h] * lax.rsqrt(jnp.mean(o[h] * o[h], axis=-1, keepdims=True) + NORM_EPS) * og
            g, hh = units[h]
            o_ref[g, rows, hh * HG_D:(hh + 1) * HG_D] = out.astype(o_ref.dtype)
        return carry

    lax.fori_loop(0, n_chunks, chunk, 0)

    if not single:
        @pl.when(i % steps_per_seq == steps_per_seq - 1)
        def _():
            def store(s, c):
                for g, hh in units:
                    sout_ref[g, s, hh] = st_ref[g, s, hh].T
                _zero_other_layers(sout_all_ref, layer, aliased, s)
                return c
            lax.fori_loop(0, n_states, store, 0)


SCAN_WIDTH = 8
SCAN_GROUPS = 2
SCAN_ROWS = 256
STATE_BLOCK_BYTES = 4 * 1024 * 1024


def _scan_geometry(n, tseq):
    rb = min(SCAN_ROWS, n)
    if tseq < rb:
        max_states = STATE_BLOCK_BYTES // (SCAN_WIDTH * LANES * LANES * 4)
        rb = min(rb, max(CHUNK, max_states * tseq))
    assert rb % CHUNK == 0 and n % rb == 0
    assert (tseq % rb == 0) or (rb % tseq == 0 and CHUNK % tseq == 0)
    n_states = max(1, rb // tseq)
    steps_per_seq = max(1, tseq // rb)
    return rb, n_states, steps_per_seq


def _layer_state_call(kernel_fn, name, heads, state_tail, grid0, tok_width, tok_inputs, consts, const_specs,
                      s_all, layer, s_new_all, tseq):
    n = tok_inputs[0].shape[0]
    layers, nb = s_all.shape[0], s_all.shape[1]
    groups = SCAN_GROUPS if tseq >= CHUNK else 1
    assert n % groups == 0 and nb % groups == 0
    rb, n_states, steps_per_seq = _scan_geometry(n // groups, tseq)
    grouped = (layers, groups, nb // groups) + s_all.shape[2:]
    tok = pl.BlockSpec((groups, rb, tok_width), lambda h, i: (0, i, h))
    state = pl.BlockSpec((None, groups, n_states, heads) + state_tail,
                         lambda h, i: (layer, 0, i // steps_per_seq, h) + (0,) * len(state_tail))
    in_specs = [tok] * len(tok_inputs) + [state] + const_specs
    args = [t.reshape(groups, n // groups, t.shape[1]) for t in tok_inputs]
    args += [s_all.reshape(grouped)] + consts
    aliases = {}
    if s_new_all is not None:
        in_specs.append(pl.BlockSpec(memory_space=pl.ANY))
        args.append(s_new_all.reshape(grouped))
        aliases = {len(args) - 1: 1}
        state_out = state
    else:
        state_out = pl.BlockSpec((layers, groups, n_states, heads) + state_tail,
                                 lambda h, i: (0, 0, i // steps_per_seq, h) + (0,) * len(state_tail))
    out, s_new = pl.pallas_call(
        functools.partial(kernel_fn, rb=rb, tseq=tseq, layer=layer, aliased=s_new_all is not None),
        grid=(grid0, n // groups // rb),
        in_specs=in_specs,
        out_specs=[tok, state_out],
        out_shape=[jax.ShapeDtypeStruct((groups, n // groups, MIX_W), BF16),
                   jax.ShapeDtypeStruct(grouped, F32)],
        scratch_shapes=[pltpu.VMEM((groups, n_states, SCAN_WIDTH, LANES, LANES), F32)],
        input_output_aliases=aliases,
        compiler_params=_params("arbitrary", "arbitrary"),
        name=name,
    )(*args)
    return out.reshape(n, MIX_W), s_new.reshape(s_all.shape)


def _hgrn_scan(q, k, v, gl, s_all, layer, s_new_all, og, tseq):
    return _layer_state_call(
        _hgrn_scan_kernel, "hgrn_scan", SCAN_WIDTH, (HG_D, HG_D), HG_HEADS // SCAN_WIDTH, SCAN_WIDTH * HG_D,
        [q, k, v, gl], [og.reshape(1, HG_D)], [_resident((1, HG_D), lambda h, i: (0, 0))],
        s_all, layer, s_new_all, tseq)


def _rwkv_proj_kernel(*refs, tseq, long_seq, has_vres):
    it = iter(refs)
    x_ref, prev_ref, ng_ref, mu_ref, w_ref = (next(it) for _ in range(5))
    w0_ref, w1_ref, w2_ref, a0_ref, a1_ref, a2_ref = (next(it) for _ in range(6))
    if has_vres:
        v0_ref, v1_ref, v2_ref, vf_ref = (next(it) for _ in range(4))
    r_ref, k_ref, v_ref, lw_ref, as_ref, xq_ref, sg_ref, h_ref = (next(it) for _ in range(8))
    carry_ref = next(it) if long_seq else None

    x = x_ref[0] if long_seq else x_ref[...]
    tm = x.shape[0]
    h = _rms(x, ng_ref[...])
    shifted = pltpu.roll(h, 1, 0)
    row = lax.broadcasted_iota(jnp.int32, (tm, 1), 0)
    if long_seq:
        tb = pl.program_id(1)

        @pl.when(tb == 0)
        def _():
            carry_ref[0:1, :] = prev_ref[0]

        h_prev = jnp.where(row == 0, carry_ref[0:1, :], shifted)
        carry_ref[0:1, :] = h[tm - 1:tm, :]

        @pl.when(tb == pl.num_programs(1) - 1)
        def _():
            h_ref[0] = h[tm - 1:tm, :]
    else:
        h_prev = jnp.where((row & (tseq - 1)) == 0, prev_ref[...], shifted)
        h_ref[...] = h

    dx = h_prev - h
    hb = h.astype(BF16)
    xr = (h + dx * mu_ref[0:1, :]).astype(BF16)
    xw = (h + dx * mu_ref[1:2, :]).astype(BF16)
    xk = (h + dx * mu_ref[2:3, :]).astype(BF16)
    xv = (h + dx * mu_ref[3:4, :]).astype(BF16)
    xa = (h + dx * mu_ref[4:5, :]).astype(BF16)

    w_mid = jnp.tanh(jnp.dot(xw, w1_ref[...], preferred_element_type=F32)).astype(BF16)
    a_mid = jnp.dot(xa, a1_ref[...], preferred_element_type=F32).astype(BF16)
    if has_vres:
        v_mid = jnp.dot(xv, v1_ref[...], preferred_element_type=F32).astype(BF16)

    for j in range(MIX_W // COL_CHUNK):
        c0 = j * COL_CHUNK
        cols = slice(c0, c0 + COL_CHUNK)
        r_ref[:, cols] = jnp.dot(xr, w_ref[:, c0:c0 + COL_CHUNK], preferred_element_type=F32).astype(r_ref.dtype)
        k_ref[:, cols] = jnp.dot(xk, w_ref[:, MIX_W + c0:MIX_W + c0 + COL_CHUNK],
                                 preferred_element_type=F32).astype(k_ref.dtype)
        vz = jnp.dot(xv, w_ref[:, 2 * MIX_W + c0:2 * MIX_W + c0 + COL_CHUNK], preferred_element_type=F32)
        if has_vres:
            gate = _sigmoid(v0_ref[:, cols] + jnp.dot(v_mid, v2_ref[:, cols], preferred_element_type=F32))
            vz = vz + (vf_ref[:, cols].astype(F32) - vz) * gate
        v_ref[:, cols] = vz.astype(v_ref.dtype)
        u = w0_ref[:, cols] + jnp.dot(w_mid, w2_ref[:, cols], preferred_element_type=F32)
        softplus = jnp.maximum(-u, 0.0) + jnp.log(1.0 + jnp.exp(-jnp.abs(u)))
        lw_ref[:, cols] = -jnp.exp(-softplus - 0.5)
        as_ref[:, cols] = _sigmoid(a0_ref[:, cols] + jnp.dot(a_mid, a2_ref[:, cols], preferred_element_type=F32))

    base = 3 * MIX_W
    for j in range(CROSS_W // COL_CHUNK):
        c0 = j * COL_CHUNK
        xq_ref[:, c0:c0 + COL_CHUNK] = jnp.dot(hb, w_ref[:, base + c0:base + c0 + COL_CHUNK],
                                                preferred_element_type=F32).astype(xq_ref.dtype)
    base = 3 * MIX_W + CROSS_W
    for j in range(BRANCH_W // COL_CHUNK):
        c0 = j * COL_CHUNK
        z = jnp.dot(hb, w_ref[:, base + c0:base + c0 + COL_CHUNK], preferred_element_type=F32)
        sg_ref[:, c0:c0 + COL_CHUNK] = (z * _sigmoid(z)).astype(sg_ref.dtype)


def _rwkv_proj(x2, shift, tseq, ng, mu, w, w0, w1, w2, a0, a1, a2, vres):
    n = x2.shape[0]
    nb = n // tseq
    tm = min(512, n)
    long_seq = tseq >= tm
    has_vres = vres is not None
    vec = lambda a: a.reshape(1, -1)
    mu8 = jnp.zeros((8, D_MODEL), F32).at[:5].set(mu)
    consts = [vec(ng), mu8, w, vec(w0), w1, w2, vec(a0), a1, a2]
    if has_vres:
        v0, v1, v2, v_first = vres
        consts += [vec(v0), v1, v2]
    if long_seq:
        assert tseq % tm == 0
        steps = tseq // tm
        grid = (nb, steps)
        cmap = lambda b, t: (0, 0)
        row = lambda width: pl.BlockSpec((tm, width), lambda b, t: (b * steps + t, 0))
        x_in = x2.reshape(nb, tseq, D_MODEL)
        x_spec = pl.BlockSpec((1, tm, D_MODEL), lambda b, t: (b, t, 0))
        prev_in = shift.reshape(nb, 1, D_MODEL)
        prev_spec = pl.BlockSpec((1, 1, D_MODEL), lambda b, t: (b, 0, 0))
        h_shape = jax.ShapeDtypeStruct((nb, 1, D_MODEL), F32)
        h_spec = pl.BlockSpec((1, 1, D_MODEL), lambda b, t: (b, 0, 0))
        scratch = [pltpu.VMEM((8, D_MODEL), F32)]
        sem = ("arbitrary", "arbitrary")
    else:
        assert tm % tseq == 0 and tseq & (tseq - 1) == 0
        grid = (n // tm,)
        cmap = lambda i: (0, 0)
        row = lambda width: pl.BlockSpec((tm, width), lambda i: (i, 0))
        x_in, x_spec = x2, row(D_MODEL)
        prev_in = jnp.zeros((nb, tseq, D_MODEL), F32).at[:, 0].set(shift).reshape(n, D_MODEL)
        prev_spec = row(D_MODEL)
        h_shape = jax.ShapeDtypeStruct((n, D_MODEL), F32)
        h_spec = row(D_MODEL)
        scratch = []
        sem = ("arbitrary",)
    in_specs = [x_spec, prev_spec] + [_resident(c.shape, cmap) for c in consts]
    args = [x_in, prev_in] + consts
    if has_vres:
        in_specs.append(row(MIX_W))
        args.append(v_first)
    sd = lambda width, dt: jax.ShapeDtypeStruct((n, width), dt)
    outs = pl.pallas_call(
        functools.partial(_rwkv_proj_kernel, tseq=tseq, long_seq=long_seq, has_vres=has_vres),
        grid=grid,
        in_specs=in_specs,
        out_specs=[row(MIX_W)] * 5 + [row(CROSS_W), row(BRANCH_W), h_spec],
        out_shape=[sd(MIX_W, BF16)] * 3 + [sd(MIX_W, F32)] * 2
        + [sd(CROSS_W, _xq_dtype(tseq)), sd(BRANCH_W, BF16), h_shape],
        scratch_shapes=scratch,
        compiler_params=_params(*sem),
        name="rwkv_proj",
    )(*args)
    h_out = outs[7]
    new_shift = h_out.reshape(nb, D_MODEL) if long_seq else h_out.reshape(nb, tseq, D_MODEL)[:, -1]
    return list(outs[:7]) + [new_shift]


def _rwkv_scan_kernel(r_ref, k_ref, v_ref, lw_ref, as_ref, s0_ref, par_ref, *rest, rb, tseq, layer, aliased):
    y_ref, sout_all_ref, st_ref = rest[1:] if aliased else rest
    sout_ref = _own_layer(sout_all_ref, layer, aliased)
    tc = min(tseq, CHUNK)
    gseq = CHUNK // tc
    n_chunks = rb // CHUNK
    steps_per_seq = max(1, tseq // rb)
    groups, n_states, width = st_ref.shape[0], st_ref.shape[1], st_ref.shape[2]
    n_levels = _log2(tc)
    narrow = _narrow(tc)
    i = pl.program_id(1)

    units = [(g, p) for g in range(groups) for p in range(width)]
    pairs = range(len(units))
    lane = lax.broadcasted_iota(jnp.int32, (1, LANES), 1)

    def state_rows(head, parity):
        return pl.ds(head * RW_N + parity, RW_N // 2, stride=2)

    @pl.when(i % steps_per_seq == 0)
    def _():
        def load(s, c):
            for g, p in units:
                e, f = s0_ref[g, s, 2 * p], s0_ref[g, s, 2 * p + 1]
                st_ref[g, s, p, state_rows(0, 0), :] = jnp.where(lane < RW_N, e, 0.0)
                st_ref[g, s, p, state_rows(0, 1), :] = jnp.where(lane < RW_N, pltpu.roll(e, RW_N, 1), 0.0)
                st_ref[g, s, p, state_rows(1, 0), :] = jnp.where(lane >= RW_N, pltpu.roll(f, RW_N, 1), 0.0)
                st_ref[g, s, p, state_rows(1, 1), :] = jnp.where(lane >= RW_N, f, 0.0)
            return c
        lax.fori_loop(0, n_states, load, 0)

    t_i = lax.broadcasted_iota(jnp.int32, (CHUNK, CHUNK), 0)
    s_i = lax.broadcasted_iota(jnp.int32, (CHUNK, CHUNK), 1)
    same = (t_i >> _log2(tc)) == (s_i >> _log2(tc))
    tri = jnp.where(same & (s_i <= t_i), 1.0, 0.0).astype(BF16)
    ones_seq = jnp.where(same, 1.0, 0.0).astype(BF16)
    n_i = lax.broadcasted_iota(jnp.int32, (CHUNK, 2 * CHUNK), 0)
    m_i = lax.broadcasted_iota(jnp.int32, (CHUNK, 2 * CHUNK), 1) & (CHUNK - 1)
    same2 = (n_i >> _log2(tc)) == (m_i >> _log2(tc))
    strict2 = jnp.where(same2 & (m_i < n_i), 1.0, 0.0).astype(F32)
    incl2 = jnp.where(same2 & (m_i <= n_i), 1.0, 0.0).astype(F32)
    incl_cat = jnp.concatenate([incl2, incl2], axis=1)
    head0 =jnp.where(lane < RW_N, 1.0, 0.0).astype(F32)
    head1 = 1.0 - head0
    ones_head = jnp.where((lax.broadcasted_iota(jnp.int32, (LANES, LANES), 0) >> _log2(RW_N))
                          == (lax.broadcasted_iota(jnp.int32, (LANES, LANES), 1) >> _log2(RW_N)),
                          1.0, 0.0).astype(F32)
    head0_b, head1_b = head0.astype(BF16), head1.astype(BF16)

    def stack(a):
        ab = a.astype(BF16)
        return jnp.concatenate([ab * head0_b, ab * head1_b], axis=0)

    def head_sum(a):
        s0 = jnp.sum(a * head0, axis=-1, keepdims=True)
        s1 = jnp.sum(a * head1, axis=-1, keepdims=True)
        return jnp.where(lane < RW_N, s0, s1)

    def tokens(ref, rows):
        return [ref[g, rows, p * LANES:(p + 1) * LANES].astype(F32) for g, p in units]

    def par(u, row):
        return par_ref[units[u][1], row:row + 1, :]

    def chunk(j, carry):
        rows = pl.ds(pl.multiple_of(j * CHUNK, CHUNK), CHUNK)
        r, k, v = tokens(r_ref, rows), tokens(k_ref, rows), tokens(v_ref, rows)
        lw, a_gate = tokens(lw_ref, rows), tokens(as_ref, rows)

        c = [_sum_rows(tri, x) for x in lw]
        c_end = [x[CHUNK - 1:CHUNK, :] for x in c] if gseq == 1 else [_sum_rows(ones_seq, x) for x in lw]
        kk = [k[p] * par(p, 0) for p in pairs]
        kk = [x * lax.rsqrt(jnp.maximum(head_sum(x * x), 1e-24)) for x in kk]
        b_vec = [kk[p] * a_gate[p] for p in pairs]
        k_mod = [k[p] * (1.0 + (a_gate[p] - 1.0) * par(p, 1)) for p in pairs]
        bonus = [head_sum(r[p] * k_mod[p] * par(p, 2)) for p in pairs]
        e_neg = [jnp.exp(-x) for x in c]
        e_end = [jnp.exp(c_end[p] - c[p]) for p in pairs]
        b_hat = [narrow(b_vec[p] * e_end[p]) for p in pairs]
        k_hat = [narrow(k_mod[p] * e_end[p]) for p in pairs]
        a_t = [-kk[p] * jnp.exp(c[p] - lw[p]) for p in pairs]
        r_t = [r[p] * jnp.exp(c[p]) for p in pairs]

        lhs = [jnp.concatenate([a_t[p], r_t[p]], axis=0).astype(BF16) for p in pairs]
        rhs = [jnp.concatenate([stack(b_vec[p] * e_neg[p]), stack(k_mod[p] * e_neg[p])], axis=0) for p in pairs]
        big = [lax.dot_general(lhs[p], rhs[p], NT, preferred_element_type=F32) for p in pairs]
        l_ab = [x[0:CHUNK, 0:LANES] * strict2 for x in big]
        l_ak = [(x[0:CHUNK, LANES:2 * LANES] * strict2).astype(BF16) for x in big]
        m_cat = [(x[CHUNK:2 * CHUNK, :] * incl_cat).astype(BF16) for x in big]
        v_s = [stack(x) for x in v]

        ah0, rh0 = [], []
        for gi in range(gseq):
            sl = slice(gi * tc, (gi + 1) * tc)
            sidx = j * gseq + gi if n_states > 1 else 0
            st = [st_ref[g, sidx, p] for g, p in units]
            ar =[jnp.concatenate([a_t[p][sl], r_t[p][sl]], axis=0) for p in pairs]
            prod = [lax.dot_general(narrow(ar[p]), narrow(st[p]), NT, preferred_element_type=F32) for p in pairs]
            ah0.append([x[0:tc] for x in prod])
            rh0.append([x[tc:2 * tc] for x in prod])
        ah0 = [ah0[0][p] if gseq == 1 else jnp.concatenate([part[p] for part in ah0], axis=0) for p in pairs]
        rh0 = [rh0[0][p] if gseq == 1 else jnp.concatenate([part[p] for part in rh0], axis=0) for p in pairs]

        u = [ah0[p] + jnp.dot(l_ak[p], v_s[p], preferred_element_type=F32) for p in pairs]
        pw = l_ab
        for lev in range(n_levels):
            pb = [m.astype(BF16) for m in pw]
            u = [u[p] + jnp.dot(pb[p], stack(u[p]), preferred_element_type=F32) for p in pairs]
            if lev < n_levels - 1:
                pw = [jnp.dot(pb[p], stack(pw[p]), preferred_element_type=F32) for p in pairs]
        y = [rh0[p] + jnp.dot(m_cat[p], jnp.concatenate([stack(u[p]), v_s[p]], axis=0),
                              preferred_element_type=F32) for p in pairs]

        v = tokens(v_ref, rows)
        for gi in range(gseq):
            sl = slice(gi * tc, (gi + 1) * tc)
            sidx = j * gseq + gi if n_states > 1 else 0
            uv = [jnp.concatenate([u[p][sl], v[p][sl]], axis=0) for p in pairs]
            bk = [jnp.concatenate([b_hat[p][sl], k_hat[p][sl]], axis=0) for p in pairs]
            upd = [lax.dot_general(narrow(uv[p]), bk[p], TN, preferred_element_type=F32) for p in pairs]
            for p, (g, pp) in enumerate(units):
                decay = c_end[p] if gseq == 1 else c_end[p][gi * tc:gi * tc + 1, :]
                st_ref[g, sidx, pp] = st_ref[g, sidx, pp] * jnp.exp(decay) + upd[p] * ones_head

        mean = [head_sum(m) * (1.0 / RW_N) for m in y]
        d = [y[p] - mean[p] for p in pairs]
        var = [head_sum(m * m) * (1.0 / RW_N) for m in d]
        for p in pairs:
            out = d[p] * lax.rsqrt(var[p] + RW_LNX_EPS) * par(p, 3) + par(p, 4)
            g, pp = units[p]
            y_ref[g, rows, pp * LANES:(pp + 1) * LANES] = (out + bonus[p] * v[p]).astype(y_ref.dtype)
        return carry

    lax.fori_loop(0, n_chunks, chunk, 0)

    @pl.when(i % steps_per_seq == steps_per_seq - 1)
    def _():
        def store(s, c):
            for g, p in units:
                even0, odd0 = st_ref[g, s, p, state_rows(0, 0), :], st_ref[g, s, p, state_rows(0, 1), :]
                even1, odd1 = st_ref[g, s, p, state_rows(1, 0), :], st_ref[g, s, p, state_rows(1, 1), :]
                sout_ref[g, s, 2 * p] = jnp.where(lane < RW_N, even0, pltpu.roll(odd0, RW_N, 1))
                sout_ref[g, s, 2 * p + 1] = jnp.where(lane < RW_N, pltpu.roll(even1, RW_N, 1), odd1)
            _zero_other_layers(sout_all_ref, layer, aliased, s)
            return c
        lax.fori_loop(0, n_states, store, 0)


def _rwkv_scan(r, k, v, lw, a_gate, s_all, layer, s_new_all, par, tseq):
    return _layer_state_call(
        _rwkv_scan_kernel, "rwkv_scan", 2 * SCAN_WIDTH, (RW_N // 2, LANES), RW_PAIRS // SCAN_WIDTH, SCAN_WIDTH * LANES,
        [r, k, v, lw, a_gate], [par], [pl.BlockSpec((SCAN_WIDTH, 8, LANES), lambda p, i: (p, 0, 0))],
        s_all, layer, s_new_all, tseq)


def _cross_attention(q_ref, k_ref, v_ref, rows_per_seq, interleaved):
    n_seq = k_ref.shape[0]
    scale = X_DH ** -0.5
    units = [(s, hh) for s in range(n_seq) for hh in range(X_HEADS)]

    def memory(ref, s, hh):
        if interleaved:
            return ref[s, pl.ds(hh, MEM_LEN, stride=X_HEADS), :].astype(BF16)
        return ref[s, :, hh * X_DH:(hh + 1) * X_DH].astype(BF16)

    q = [q_ref[s * rows_per_seq:(s + 1) * rows_per_seq, hh * X_DH:(hh + 1) * X_DH].astype(BF16) for s, hh in units]
    sc = [lax.dot_general(q[u], memory(k_ref, s, hh), NT, preferred_element_type=F32) * scale
          for u, (s, hh) in enumerate(units)]
    p = [jnp.exp(x - jnp.max(x, axis=-1, keepdims=True)) for x in sc]
    denom = [jnp.sum(x, axis=-1, keepdims=True) for x in p]
    o = [jnp.dot(p[u].astype(BF16), memory(v_ref, s, hh), preferred_element_type=F32) / denom[u]
         for u, (s, hh) in enumerate(units)]
    return [jnp.concatenate(o[s * X_HEADS:(s + 1) * X_HEADS], axis=-1) for s in range(n_seq)]


def _out_kernel(x_ref, mix_ref, xq_ref, k_ref, v_ref, sg_ref, w_ref, fg_ref, o_ref, *, rows_per_seq, interleaved,
                final):
    xo = _cross_attention(xq_ref, k_ref, v_ref, rows_per_seq, interleaved)
    xo = xo[0] if len(xo) == 1 else jnp.concatenate(xo, axis=0)
    sg = sg_ref[...].astype(F32)
    left = (mix_ref[...].astype(F32) * sg[:, :MIX_W]).astype(BF16)
    right = (xo * sg[:, MIX_W:]).astype(BF16)
    x = x_ref[...] + jnp.dot(left, w_ref[0:MIX_W, :], preferred_element_type=F32) \
        + jnp.dot(right, w_ref[MIX_W:BRANCH_W, :], preferred_element_type=F32)
    o_ref[...] = _rms(x, fg_ref[...]) if final else x


def _out_proj(x2, mix, xq, mk_all, mv_all, layer, tseq, sg, w, final_g, final):
    n = x2.shape[0]
    interleaved = mk_all.shape[-1] == X_DH
    rb = min(512, n, 8 * tseq)
    n_seq = max(1, rb // tseq)
    steps_per_seq = max(1, tseq // rb)
    row = lambda width: pl.BlockSpec((rb, width), lambda i: (i, 0))
    mem = pl.BlockSpec((None, n_seq) + mk_all.shape[2:], lambda i: (layer, i // steps_per_seq, 0, 0))
    return pl.pallas_call(
        functools.partial(_out_kernel, rows_per_seq=rb // n_seq, interleaved=interleaved, final=final),
        grid=(n // rb,),
        in_specs=[row(D_MODEL), row(MIX_W), row(CROSS_W), mem, mem, row(BRANCH_W),
                  _resident((BRANCH_W, D_MODEL), lambda i: (0, 0)),
                  _resident((1, D_MODEL), lambda i: (0, 0))],
        out_specs=row(D_MODEL),
        out_shape=jax.ShapeDtypeStruct((n, D_MODEL), F32),
        compiler_params=_params("arbitrary"),
        name="out_proj",
    )(x2, mix, xq, mk_all, mv_all, sg, w, final_g.reshape(1, D_MODEL))


def _trunk(x, mem_k, mem_v, s_hgrn, s_rwkv, s_shift, p):
    nb, tseq, _ = x.shape
    x2 = x.reshape(nb * tseq, D_MODEL)
    rwkv_shape = s_rwkv.shape
    s_rwkv = s_rwkv.reshape(rwkv_shape[:3] + (RW_N // 2, 2 * RW_N))
    new_h, new_r, new_s = None, None, []
    v_first = None
    for i in range(DEPTH):
        j = i // 2
        if i % 2 == 0:
            q, k, gl, v, xq, sg = _hgrn_proj(x2, p["norm_g"][i], p["w_in"][i], p["lbs"][j], tseq)
            mix, new_h = _hgrn_scan(q, k, v, gl, s_hgrn, j, new_h, p["hg_onorm_g"][j], tseq)
        else:
            vres = None if v_first is None else (p["rw_v0"][j - 1], p["rw_v1"][j - 1], p["rw_v2"][j - 1], v_first)
            r, k, v, lw, a_gate, xq, sg, shift = _rwkv_proj(
                x2, s_shift[j], tseq, p["norm_g"][i], p["rw_mu"][j], p["w_in"][i],
                p["rw_w0"][j], p["rw_w1"][j], p["rw_w2"][j], p["rw_a0"][j], p["rw_a1"][j], p["rw_a2"][j], vres)
            if v_first is None:
                v_first = v
            mix, new_r = _rwkv_scan(r, k, v, lw, a_gate, s_rwkv, j, new_r, p["rw_par"][j], tseq)
            new_s.append(shift)
        x2 = _out_proj(x2, mix, xq, mem_k, mem_v, i, tseq, sg, p["w_out"][i], p["final_g"], final=(i == DEPTH - 1))
    return x2.reshape(nb, tseq, D_MODEL), new_h, new_r.reshape(rwkv_shape), jnp.stack(new_s)


def kernel(x_prompt, x_sample, mem_prompt, state_hgrn, state_rwkv, state_shift, cache_mem_k, cache_mem_v,
           norm_g, w_in, w_out, mem_norm_g, w_mem_kv, hg_lb, hg_onorm_g, rw_mu, rw_w0, rw_w1, rw_w2,
           rw_a0, rw_a1, rw_a2, rw_v0, rw_v1, rw_v2, rw_kk, rw_ka, rw_rk, rw_lnx_g, rw_lnx_b, final_g):
    n_rwkv = rw_mu.shape[0]
    lbs = jax.nn.softmax(hg_lb.astype(F32), axis=0)
    lbs = jnp.cumsum(lbs, axis=0) - lbs[0]
    par = jnp.stack([rw_kk, rw_ka, rw_rk.reshape(n_rwkv, MIX_W), rw_lnx_g, rw_lnx_b], axis=1)
    par = jnp.concatenate([par, jnp.zeros((n_rwkv, 3, MIX_W), F32)], axis=1)
    par = par.reshape(n_rwkv, 8, RW_PAIRS, LANES).transpose(0, 2, 1, 3)
    p = dict(norm_g=norm_g, w_in=w_in.astype(BF16), w_out=w_out.astype(BF16), lbs=lbs, hg_onorm_g=hg_onorm_g,
             rw_mu=rw_mu, rw_w0=rw_w0, rw_w1=rw_w1.astype(BF16), rw_w2=rw_w2.astype(BF16),
             rw_a0=rw_a0, rw_a1=rw_a1.astype(BF16), rw_a2=rw_a2.astype(BF16),
             rw_v0=rw_v0, rw_v1=rw_v1.astype(BF16), rw_v2=rw_v2.astype(BF16), rw_par=par, final_g=final_g)

    nb, mem_len, _ = mem_prompt.shape
    mk, mv = _memory_kv(mem_prompt.reshape(nb * mem_len, D_MODEL), mem_norm_g, w_mem_kv.astype(BF16))
    mk = mk.reshape(DEPTH, nb, mem_len, CROSS_W)
    mv = mv.reshape(DEPTH, nb, mem_len, CROSS_W)
    z_h = jnp.zeros((state_hgrn.shape[0], nb) + state_hgrn.shape[2:], F32)
    z_r = jnp.zeros((state_rwkv.shape[0], nb) + state_rwkv.shape[2:], F32)
    z_s = jnp.zeros((state_shift.shape[0], nb, D_MODEL), F32)
    y_p, sh_p, sr_p, ss_p = _trunk(x_prompt, mk, mv, z_h, z_r, z_s, p)

    nbs = x_sample.shape[0]
    cmk = cache_mem_k.reshape(DEPTH, nbs, mem_len * X_HEADS, X_DH)
    cmv = cache_mem_v.reshape(DEPTH, nbs, mem_len * X_HEADS, X_DH)
    y_s, sh_s, sr_s, ss_s = _trunk(x_sample, cmk, cmv, state_hgrn, state_rwkv, state_shift, p)
    return (y_p, y_s, sh_p, sr_p, ss_p,
            mk.reshape(DEPTH, nb, mem_len, X_HEADS, X_DH), mv.reshape(DEPTH, nb, mem_len, X_HEADS, X_DH),
            sh_s, sr_s, ss_s)
```

```python
import functools
import math

import jax
import jax.numpy as jnp
from jax import lax
from jax.experimental import pallas as pl
from jax.experimental.pallas import tpu as pltpu

F32 = jnp.float32
BF16 = jnp.bfloat16

D_MODEL = 1024
DEPTH = 4
MIX_W = D_MODEL
HG_HEADS = 8
HG_D = MIX_W // HG_HEADS
RW_N = 64
RW_HEADS = MIX_W // RW_N
RW_PAIRS = RW_HEADS // 2
RW_LNX_EPS = 64e-5
MEM_LEN = 256
X_HEADS = 4
X_DH = 128
CROSS_W = X_HEADS * X_DH
BRANCH_W = MIX_W + CROSS_W
IN_COLS = 3 * MIX_W + CROSS_W + BRANCH_W
NORM_EPS = 1e-6
LOG_FLOOR = 1e-30
NEG_BIG = -1e30

LANES = 128
SUBLANES = 8
CHUNK = 64
COL_CHUNK = 512
VMEM_LIMIT = 56 * 1024 * 1024

NT = (((1,), (1,)), ((), ()))
TN = (((0,), (0,)), ((), ()))


def _params(*sem):
    return pltpu.CompilerParams(dimension_semantics=sem, vmem_limit_bytes=VMEM_LIMIT)


def _resident(shape, index_map):
    return pl.BlockSpec(shape, index_map, pipeline_mode=pl.Buffered(1))


def _rms(x, g):
    return x * lax.rsqrt(jnp.mean(x * x, axis=-1, keepdims=True) + NORM_EPS) * g


def _sigmoid(z):
    return 1.0 / (1.0 + jnp.exp(-z))


def _split3(x):
    hi = x.astype(BF16)
    rest = x - hi.astype(F32)
    mid = rest.astype(BF16)
    low = (rest - mid.astype(F32)).astype(BF16)
    return jnp.concatenate([hi, mid, low], axis=1)


def _sum_rows(sel, x):
    parts = jnp.dot(sel, _split3(x), preferred_element_type=F32)
    return parts[:, 0:LANES] + parts[:, LANES:2 * LANES] + parts[:, 2 * LANES:3 * LANES]


def _narrow(rows):
    if rows % 16 == 0:
        return lambda a: a.astype(BF16)
    return lambda a: a.astype(BF16).astype(F32)


def _log2(n):
    l = int(math.log2(n))
    assert 1 << l == n, n
    return l


def _memkv_kernel(x_ref, g_ref, w_ref, k_ref, v_ref):
    hb = _rms(x_ref[...], g_ref[...]).astype(BF16)
    kv = jnp.dot(hb, w_ref[...], preferred_element_type=F32)
    k_ref[...] = kv[:, :CROSS_W]
    v_ref[...] = kv[:, CROSS_W:]


def _memory_kv(mem2d, g, w):
    n = mem2d.shape[0]
    tm = min(512, n)
    out = jax.ShapeDtypeStruct((DEPTH, n, CROSS_W), F32)
    return pl.pallas_call(
        _memkv_kernel,
        grid=(DEPTH, n // tm),
        in_specs=[
            pl.BlockSpec((tm, D_MODEL), lambda l, i: (i, 0)),
            pl.BlockSpec((None, 1, D_MODEL), lambda l, i: (l, 0, 0)),
            pl.BlockSpec((None, D_MODEL, 2 * CROSS_W), lambda l, i: (l, 0, 0)),
        ],
        out_specs=[
            pl.BlockSpec((None, tm, CROSS_W), lambda l, i: (l, i, 0)),
            pl.BlockSpec((None, tm, CROSS_W), lambda l, i: (l, i, 0)),
        ],
        out_shape=[out, out],
        compiler_params=_params("arbitrary", "arbitrary"),
        name="memory_kv",
    )(mem2d, g.reshape(DEPTH, 1, D_MODEL), w)


def _hgrn_proj_kernel(x_ref, g_ref, w_ref, lb_ref, q_ref, k_ref, gl_ref, v_ref, xq_ref, sg_ref):
    hb = _rms(x_ref[...], g_ref[...]).astype(BF16)
    for j in range(IN_COLS // COL_CHUNK):
        c0 = j * COL_CHUNK
        z = jnp.dot(hb, w_ref[:, c0:c0 + COL_CHUNK], preferred_element_type=F32)
        if c0 < MIX_W:
            q_ref[:, c0:c0 + COL_CHUNK] = (z * _sigmoid(z)).astype(q_ref.dtype)
        elif c0 < 2 * MIX_W:
            d0 = c0 - MIX_W
            lb = lb_ref[:, d0:d0 + COL_CHUNK]
            f = lb + (1.0 - lb) * _sigmoid(z)
            gl_ref[:, d0:d0 + COL_CHUNK] = jnp.log(jnp.maximum(f, LOG_FLOOR))
            k_ref[:, d0:d0 + COL_CHUNK] = ((1.0 - lb) * _sigmoid(-z)).astype(k_ref.dtype)
        elif c0 < 3 * MIX_W:
            d0 = c0 - 2 * MIX_W
            v_ref[:, d0:d0 + COL_CHUNK] = z.astype(v_ref.dtype)
        elif c0 < 3 * MIX_W + CROSS_W:
            d0 = c0 - 3 * MIX_W
            xq_ref[:, d0:d0 + COL_CHUNK] = z.astype(xq_ref.dtype)
        else:
            d0 = c0 - 3 * MIX_W - CROSS_W
            sg_ref[:, d0:d0 + COL_CHUNK] = (z * _sigmoid(z)).astype(sg_ref.dtype)


def _xq_dtype(tseq):
    return BF16 if tseq % 16 == 0 else F32


def _hgrn_proj(x2, g, w, lb, tseq):
    n = x2.shape[0]
    tm = min(1024, n)
    row = lambda width: pl.BlockSpec((tm, width), lambda i: (i, 0))
    sd = lambda width, dt: jax.ShapeDtypeStruct((n, width), dt)
    return pl.pallas_call(
        _hgrn_proj_kernel,
        grid=(n // tm,),
        in_specs=[
            row(D_MODEL),
            _resident((1, D_MODEL), lambda i: (0, 0)),
            _resident((D_MODEL, IN_COLS), lambda i: (0, 0)),
            _resident((1, MIX_W), lambda i: (0, 0)),
        ],
        out_specs=[row(MIX_W), row(MIX_W), row(MIX_W), row(MIX_W), row(CROSS_W), row(BRANCH_W)],
        out_shape=[sd(MIX_W, BF16), sd(MIX_W, BF16), sd(MIX_W, F32), sd(MIX_W, BF16),
                   sd(CROSS_W, _xq_dtype(tseq)), sd(BRANCH_W, BF16)],
        compiler_params=_params("arbitrary"),
        name="hgrn_proj",
    )(x2, g.reshape(1, D_MODEL), w, lb.reshape(1, MIX_W))


def _own_layer(sout_all_ref, layer, aliased):
    return sout_all_ref if aliased else sout_all_ref.at[layer]


def _zero_other_layers(sout_all_ref, layer, aliased, s):
    if aliased:
        return
    for other in range(sout_all_ref.shape[0]):
        if other != layer:
            for g in range(sout_all_ref.shape[1]):
                sout_all_ref[other, g, s] = jnp.zeros(sout_all_ref.shape[3:], F32)


def _hgrn_scan_kernel(q_ref, k_ref, v_ref, g_ref, s0_ref, og_ref, *rest, rb, tseq, layer, aliased):
    o_ref, sout_all_ref, st_ref = rest[1:] if aliased else rest
    sout_ref = _own_layer(sout_all_ref, layer, aliased)
    tc = min(tseq, CHUNK)
    gseq = CHUNK // tc
    narrow = _narrow(tc)
    n_chunks = rb // CHUNK
    steps_per_seq = max(1, tseq // rb)
    groups, n_states, width = st_ref.shape[0], st_ref.shape[1], st_ref.shape[2]
    i = pl.program_id(1)
    units = [(g, hh) for g in range(groups) for hh in range(width)]
    single = tseq < CHUNK

    if not single:
        @pl.when(i % steps_per_seq == 0)
        def _():
            def load(s, c):
                for g, hh in units:
                    st_ref[g, s, hh] = s0_ref[g, s, hh].T
                return c
            lax.fori_loop(0, n_states, load, 0)

    t_i = lax.broadcasted_iota(jnp.int32, (CHUNK, CHUNK), 0)
    s_i = lax.broadcasted_iota(jnp.int32, (CHUNK, CHUNK), 1)
    same = (t_i >> _log2(tc)) == (s_i >> _log2(tc))
    tri = jnp.where(same & (s_i <= t_i), 1.0, 0.0).astype(BF16)
    ones_seq = jnp.where(same, 1.0, 0.0).astype(BF16)
    row = lax.broadcasted_iota(jnp.int32, (CHUNK, 1), 0)
    og = og_ref[...]
    halves = [1 << lev for lev in range(_log2(tc))]
    late = [(row & half) != 0 for half in halves]
    pair_mask = [jnp.where(((t_i >> _log2(2 * half)) == (s_i >> _log2(2 * half)))
                           & ((t_i & half) != 0) & ((s_i & half) == 0), 1.0, 0.0).astype(F32) for half in halves]
    small = [half for half in halves if 2 * half < SUBLANES]
    pick = jnp.concatenate([jnp.where(s_i == (t_i & ~(2 * half - 1)) + half - 1, 1.0, 0.0)
                            for half in small], axis=0).astype(BF16)

    if single:
        piece_row = lax.broadcasted_iota(jnp.int32, (3 * HG_D, HG_D), 0) & (HG_D - 1)
        spread = [jnp.where(piece_row == gi * tc, 1.0, 0.0).astype(BF16) for gi in range(gseq)]

    heads = range(len(units))

    def tokens(ref, rows):
        return [ref[g, rows, hh * HG_D:(hh + 1) * HG_D] for g, hh in units]

    def chunk(j, carry):
        rows = pl.ds(pl.multiple_of(j * CHUNK, CHUNK), CHUNK)
        q = [x.astype(F32) for x in tokens(q_ref, rows)]
        k = [x.astype(F32) for x in tokens(k_ref, rows)]
        v = [x.astype(F32) for x in tokens(v_ref, rows)]
        g = tokens(g_ref, rows)
        b = [_sum_rows(tri, x) for x in g]
        b_end = [x[CHUNK - 1:CHUNK, :] for x in b] if gseq == 1 else [_sum_rows(ones_seq, x) for x in g]
        q_in = [q[h] * jnp.exp(b[h]) for h in heads]
        k_dec = [k[h] * jnp.exp(b_end[h] - b[h]) for h in heads]
        vb = [x.astype(BF16) for x in v]

        o = []
        if single:
            decay_t = [_split3(jnp.concatenate([jnp.exp(b_end[h]), jnp.zeros((HG_D - CHUNK, HG_D), F32)], axis=0).T)
                       for h in heads]
        for gi in range(gseq):
            sl = slice(gi * tc, (gi + 1) * tc)
            sidx = j * gseq + gi if n_states > 1 else 0
            if single:
                st = [s0_ref[g, sidx, hh] for g, hh in units]
                o.append([jnp.dot(narrow(q_in[h][sl]), narrow(st[h]), preferred_element_type=F32) for h in heads])
                upd = [lax.dot_general(narrow(k_dec[h][sl]), narrow(v[h][sl]), TN, preferred_element_type=F32)
                       for h in heads]
                decay = [jnp.dot(decay_t[h], spread[gi], preferred_element_type=F32) for h in heads]
                for h, (g, hh) in enumerate(units):
                    sout_ref[g, sidx, hh] = st[h] * decay[h] + upd[h]
                _zero_other_layers(sout_all_ref, layer, aliased, sidx)
                continue
            st = [st_ref[g, sidx, hh] for g, hh in units]
            o.append([lax.dot_general(narrow(q_in[h][sl]), narrow(st[h]), NT, preferred_element_type=F32)
                      for h in heads])
            upd = [lax.dot_general(narrow(v[h][sl]), narrow(k_dec[h][sl]), TN, preferred_element_type=F32)
                   for h in heads]
            for h in heads:
                decay = b_end[h] if gseq == 1 else b_end[h][gi * tc:gi * tc + 1, :]
                st_ref[units[h][0], sidx, units[h][1]] = st[h] * jnp.exp(decay) + upd[h]
        o = [o[0][h] if gseq == 1 else jnp.concatenate([part[h] for part in o], axis=0) for h in heads]

        beta_small = [_sum_rows(pick, x) for x in b]
        att = [None for _ in heads]
        for lev, (half, is_late, mask) in enumerate(zip(halves, late, pair_mask)):
            for h in heads:
                if half in small:
                    beta = beta_small[h][lev * CHUNK:(lev + 1) * CHUNK]
                else:
                    blocks = b[h].reshape(CHUNK // (2 * half), 2 * half, HG_D)
                    beta = jnp.broadcast_to(blocks[:, half - 1:half, :], blocks.shape).reshape(CHUNK, HG_D)
                q_l = (q[h] * jnp.exp(jnp.where(is_late, b[h] - beta, NEG_BIG))).astype(BF16)
                k_l = (k[h] * jnp.exp(jnp.where(is_late, NEG_BIG, beta - b[h]))).astype(BF16)
                part = lax.dot_general(q_l, k_l, NT, preferred_element_type=F32) * mask
                att[h] = part if att[h] is None else att[h] + part
        o = [o[h] + jnp.dot(att[h].astype(BF16), vb[h], preferred_element_type=F32) for h in heads]
        o = [o[h] + jnp.sum(q[h] * k[h], axis=-1, keepdims=True) * v[h] for h in heads]

        for h in heads:
            out = o[h] * lax.rsqrt(jnp.mean(o[h] * o[h], axis=-1, keepdims=True) + NORM_EPS) * og
            g, hh = units[h]
            o_ref[g, rows, hh * HG_D:(hh + 1) * HG_D] = out.astype(o_ref.dtype)
        return carry

    lax.fori_loop(0, n_chunks, chunk, 0)

    if not single:
        @pl.when(i % steps_per_seq == steps_per_seq - 1)
        def _():
            def store(s, c):
                for g, hh in units:
                    sout_ref[g, s, hh] = st_ref[g, s, hh].T
                _zero_other_layers(sout_all_ref, layer, aliased, s)
                return c
            lax.fori_loop(0, n_states, store, 0)


SCAN_WIDTH = 8
SCAN_GROUPS = 2
SCAN_ROWS = 256
STATE_BLOCK_BYTES = 4 * 1024 * 1024


def _scan_geometry(n, tseq):
    rb = min(SCAN_ROWS, n)
    if tseq < rb:
        max_states = STATE_BLOCK_BYTES // (SCAN_WIDTH * LANES * LANES * 4)
        rb = min(rb, max(CHUNK, max_states * tseq))
    assert rb % CHUNK == 0 and n % rb == 0
    assert (tseq % rb == 0) or (rb % tseq == 0 and CHUNK % tseq == 0)
    n_states = max(1, rb // tseq)
    steps_per_seq = max(1, tseq // rb)
    return rb, n_states, steps_per_seq


def _layer_state_call(kernel_fn, name, heads, state_tail, grid0, tok_width, tok_inputs, consts, const_specs,
                      s_all, layer, s_new_all, tseq):
    n = tok_inputs[0].shape[0]
    layers, nb = s_all.shape[0], s_all.shape[1]
    groups = SCAN_GROUPS if tseq >= CHUNK else 1
    assert n % groups == 0 and nb % groups == 0
    rb, n_states, steps_per_seq = _scan_geometry(n // groups, tseq)
    grouped = (layers, groups, nb // groups) + s_all.shape[2:]
    tok = pl.BlockSpec((groups, rb, tok_width), lambda h, i: (0, i, h))
    state = pl.BlockSpec((None, groups, n_states, heads) + state_tail,
                         lambda h, i: (layer, 0, i // steps_per_seq, h) + (0,) * len(state_tail))
    in_specs = [tok] * len(tok_inputs) + [state] + const_specs
    args = [t.reshape(groups, n // groups, t.shape[1]) for t in tok_inputs]
    args += [s_all.reshape(grouped)] + consts
    aliases = {}
    if s_new_all is not None:
        in_specs.append(pl.BlockSpec(memory_space=pl.ANY))
        args.append(s_new_all.reshape(grouped))
        aliases = {len(args) - 1: 1}
        state_out = state
    else:
        state_out = pl.BlockSpec((layers, groups, n_states, heads) + state_tail,
                                 lambda h, i: (0, 0, i // steps_per_seq, h) + (0,) * len(state_tail))
    out, s_new = pl.pallas_call(
        functools.partial(kernel_fn, rb=rb, tseq=tseq, layer=layer, aliased=s_new_all is not None),
        grid=(grid0, n // groups // rb),
        in_specs=in_specs,
        out_specs=[tok, state_out],
        out_shape=[jax.ShapeDtypeStruct((groups, n // groups, MIX_W), BF16),
                   jax.ShapeDtypeStruct(grouped, F32)],
        scratch_shapes=[pltpu.VMEM((groups, n_states, SCAN_WIDTH, LANES, LANES), F32)],
        input_output_aliases=aliases,
        compiler_params=_params("arbitrary", "arbitrary"),
        name=name,
    )(*args)
    return out.reshape(n, MIX_W), s_new.reshape(s_all.shape)


def _hgrn_scan(q, k, v, gl, s_all, layer, s_new_all, og, tseq):
    return _layer_state_call(
        _hgrn_scan_kernel, "hgrn_scan", SCAN_WIDTH, (HG_D, HG_D), HG_HEADS // SCAN_WIDTH, SCAN_WIDTH * HG_D,
        [q, k, v, gl], [og.reshape(1, HG_D)], [_resident((1, HG_D), lambda h, i: (0, 0))],
        s_all, layer, s_new_all, tseq)


def _rwkv_proj_kernel(*refs, tseq, long_seq, has_vres):
    it = iter(refs)
    x_ref, prev_ref, ng_ref, mu_ref, w_ref = (next(it) for _ in range(5))
    w0_ref, w1_ref, w2_ref, a0_ref, a1_ref, a2_ref = (next(it) for _ in range(6))
    if has_vres:
        v0_ref, v1_ref, v2_ref, vf_ref = (next(it) for _ in range(4))
    r_ref, k_ref, v_ref, lw_ref, as_ref, xq_ref, sg_ref, h_ref = (next(it) for _ in range(8))
    carry_ref = next(it) if long_seq else None

    x = x_ref[0] if long_seq else x_ref[...]
    tm = x.shape[0]
    h = _rms(x, ng_ref[...])
    shifted = pltpu.roll(h, 1, 0)
    row = lax.broadcasted_iota(jnp.int32, (tm, 1), 0)
    if long_seq:
        tb = pl.program_id(1)

        @pl.when(tb == 0)
        def _():
            carry_ref[0:1, :] = prev_ref[0]

        h_prev = jnp.where(row == 0, carry_ref[0:1, :], shifted)
        carry_ref[0:1, :] = h[tm - 1:tm, :]

        @pl.when(tb == pl.num_programs(1) - 1)
        def _():
            h_ref[0] = h[tm - 1:tm, :]
    else:
        h_prev = jnp.where((row & (tseq - 1)) == 0, prev_ref[...], shifted)
        h_ref[...] = h

    dx = h_prev - h
    hb = h.astype(BF16)
    xr = (h + dx * mu_ref[0:1, :]).astype(BF16)
    xw = (h + dx * mu_ref[1:2, :]).astype(BF16)
    xk = (h + dx * mu_ref[2:3, :]).astype(BF16)
    xv = (h + dx * mu_ref[3:4, :]).astype(BF16)
    xa = (h + dx * mu_ref[4:5, :]).astype(BF16)

    w_mid = jnp.tanh(jnp.dot(xw, w1_ref[...], preferred_element_type=F32)).astype(BF16)
    a_mid = jnp.dot(xa, a1_ref[...], preferred_element_type=F32).astype(BF16)
    if has_vres:
        v_mid = jnp.dot(xv, v1_ref[...], preferred_element_type=F32).astype(BF16)

    for j in range(MIX_W // COL_CHUNK):
        c0 = j * COL_CHUNK
        cols = slice(c0, c0 + COL_CHUNK)
        r_ref[:, cols] = jnp.dot(xr, w_ref[:, c0:c0 + COL_CHUNK], preferred_element_type=F32).astype(r_ref.dtype)
        k_ref[:, cols] = jnp.dot(xk, w_ref[:, MIX_W + c0:MIX_W + c0 + COL_CHUNK],
                                 preferred_element_type=F32).astype(k_ref.dtype)
        vz = jnp.dot(xv, w_ref[:, 2 * MIX_W + c0:2 * MIX_W + c0 + COL_CHUNK], preferred_element_type=F32)
        if has_vres:
            gate = _sigmoid(v0_ref[:, cols] + jnp.dot(v_mid, v2_ref[:, cols], preferred_element_type=F32))
            vz = vz + (vf_ref[:, cols].astype(F32) - vz) * gate
        v_ref[:, cols] = vz.astype(v_ref.dtype)
        u = w0_ref[:, cols] + jnp.dot(w_mid, w2_ref[:, cols], preferred_element_type=F32)
        softplus = jnp.maximum(-u, 0.0) + jnp.log(1.0 + jnp.exp(-jnp.abs(u)))
        lw_ref[:, cols] = -jnp.exp(-softplus - 0.5)
        as_ref[:, cols] = _sigmoid(a0_ref[:, cols] + jnp.dot(a_mid, a2_ref[:, cols], preferred_element_type=F32))

    base = 3 * MIX_W
    for j in range(CROSS_W // COL_CHUNK):
        c0 = j * COL_CHUNK
        xq_ref[:, c0:c0 + COL_CHUNK] = jnp.dot(hb, w_ref[:, base + c0:base + c0 + COL_CHUNK],
                                                preferred_element_type=F32).astype(xq_ref.dtype)
    base = 3 * MIX_W + CROSS_W
    for j in range(BRANCH_W // COL_CHUNK):
        c0 = j * COL_CHUNK
        z = jnp.dot(hb, w_ref[:, base + c0:base + c0 + COL_CHUNK], preferred_element_type=F32)
        sg_ref[:, c0:c0 + COL_CHUNK] = (z * _sigmoid(z)).astype(sg_ref.dtype)


def _rwkv_proj(x2, shift, tseq, ng, mu, w, w0, w1, w2, a0, a1, a2, vres):
    n = x2.shape[0]
    nb = n // tseq
    tm = min(512, n)
    long_seq = tseq >= tm
    has_vres = vres is not None
    vec = lambda a: a.reshape(1, -1)
    mu8 = jnp.zeros((8, D_MODEL), F32).at[:5].set(mu)
    consts = [vec(ng), mu8, w, vec(w0), w1, w2, vec(a0), a1, a2]
    if has_vres:
        v0, v1, v2, v_first = vres
        consts += [vec(v0), v1, v2]
    if long_seq:
        assert tseq % tm == 0
        steps = tseq // tm
        grid = (nb, steps)
        cmap = lambda b, t: (0, 0)
        row = lambda width: pl.BlockSpec((tm, width), lambda b, t: (b * steps + t, 0))
        x_in = x2.reshape(nb, tseq, D_MODEL)
        x_spec = pl.BlockSpec((1, tm, D_MODEL), lambda b, t: (b, t, 0))
        prev_in = shift.reshape(nb, 1, D_MODEL)
        prev_spec = pl.BlockSpec((1, 1, D_MODEL), lambda b, t: (b, 0, 0))
        h_shape = jax.ShapeDtypeStruct((nb, 1, D_MODEL), F32)
        h_spec = pl.BlockSpec((1, 1, D_MODEL), lambda b, t: (b, 0, 0))
        scratch = [pltpu.VMEM((8, D_MODEL), F32)]
        sem = ("arbitrary", "arbitrary")
    else:
        assert tm % tseq == 0 and tseq & (tseq - 1) == 0
        grid = (n // tm,)
        cmap = lambda i: (0, 0)
        row = lambda width: pl.BlockSpec((tm, width), lambda i: (i, 0))
        x_in, x_spec = x2, row(D_MODEL)
        prev_in = jnp.zeros((nb, tseq, D_MODEL), F32).at[:, 0].set(shift).reshape(n, D_MODEL)
        prev_spec = row(D_MODEL)
        h_shape = jax.ShapeDtypeStruct((n, D_MODEL), F32)
        h_spec = row(D_MODEL)
        scratch = []
        sem = ("arbitrary",)
    in_specs = [x_spec, prev_spec] + [_resident(c.shape, cmap) for c in consts]
    args = [x_in, prev_in] + consts
    if has_vres:
        in_specs.append(row(MIX_W))
        args.append(v_first)
    sd = lambda width, dt: jax.ShapeDtypeStruct((n, width), dt)
    outs = pl.pallas_call(
        functools.partial(_rwkv_proj_kernel, tseq=tseq, long_seq=long_seq, has_vres=has_vres),
        grid=grid,
        in_specs=in_specs,
        out_specs=[row(MIX_W)] * 5 + [row(CROSS_W), row(BRANCH_W), h_spec],
        out_shape=[sd(MIX_W, BF16)] * 3 + [sd(MIX_W, F32)] * 2
        + [sd(CROSS_W, _xq_dtype(tseq)), sd(BRANCH_W, BF16), h_shape],
        scratch_shapes=scratch,
        compiler_params=_params(*sem),
        name="rwkv_proj",
    )(*args)
    h_out = outs[7]
    new_shift = h_out.reshape(nb, D_MODEL) if long_seq else h_out.reshape(nb, tseq, D_MODEL)[:, -1]
    return list(outs[:7]) + [new_shift]


def _rwkv_scan_kernel(r_ref, k_ref, v_ref, lw_ref, as_ref, s0_ref, par_ref, *rest, rb, tseq, layer, aliased):
    y_ref, sout_all_ref, st_ref = rest[1:] if aliased else rest
    sout_ref = _own_layer(sout_all_ref, layer, aliased)
    tc = min(tseq, CHUNK)
    gseq = CHUNK // tc
    n_chunks = rb // CHUNK
    steps_per_seq = max(1, tseq // rb)
    groups, n_states, width = st_ref.shape[0], st_ref.shape[1], st_ref.shape[2]
    n_levels = _log2(tc)
    narrow = _narrow(tc)
    i = pl.program_id(1)

    units = [(g, p) for g in range(groups) for p in range(width)]
    pairs = range(len(units))
    lane = lax.broadcasted_iota(jnp.int32, (1, LANES), 1)

    def state_rows(head, parity):
        return pl.ds(head * RW_N + parity, RW_N // 2, stride=2)

    def unpack_states(s):
        for g, p in units:
            e, f = s0_ref[g, s, 2 * p], s0_ref[g, s, 2 * p + 1]
            st_ref[g, s, p, state_rows(0, 0), :] = jnp.where(lane < RW_N, e, 0.0)
            st_ref[g, s, p, state_rows(0, 1), :] = jnp.where(lane < RW_N, pltpu.roll(e, RW_N, 1), 0.0)
            st_ref[g, s, p, state_rows(1, 0), :] = jnp.where(lane >= RW_N, pltpu.roll(f, RW_N, 1), 0.0)
            st_ref[g, s, p, state_rows(1, 1), :] = jnp.where(lane >= RW_N, f, 0.0)

    def pack_states(s):
        for g, p in units:
            even0, odd0 = st_ref[g, s, p, state_rows(0, 0), :], st_ref[g, s, p, state_rows(0, 1), :]
            even1, odd1 = st_ref[g, s, p, state_rows(1, 0), :], st_ref[g, s, p, state_rows(1, 1), :]
            sout_ref[g, s, 2 * p] = jnp.where(lane < RW_N, even0, pltpu.roll(odd0, RW_N, 1))
            sout_ref[g, s, 2 * p + 1] = jnp.where(lane < RW_N, pltpu.roll(even1, RW_N, 1), odd1)
        _zero_other_layers(sout_all_ref, layer, aliased, s)

    single = tseq < CHUNK

    if not single:
        @pl.when(i % steps_per_seq == 0)
        def _():
            def load(s, c):
                unpack_states(s)
                return c
            lax.fori_loop(0, n_states, load, 0)

    t_i = lax.broadcasted_iota(jnp.int32, (CHUNK, CHUNK), 0)
    s_i = lax.broadcasted_iota(jnp.int32, (CHUNK, CHUNK), 1)
    same = (t_i >> _log2(tc)) == (s_i >> _log2(tc))
    tri = jnp.where(same & (s_i <= t_i), 1.0, 0.0).astype(BF16)
    ones_seq = jnp.where(same, 1.0, 0.0).astype(BF16)
    n_i = lax.broadcasted_iota(jnp.int32, (CHUNK, 2 * CHUNK), 0)
    m_i = lax.broadcasted_iota(jnp.int32, (CHUNK, 2 * CHUNK), 1) & (CHUNK - 1)
    same2 = (n_i >> _log2(tc)) == (m_i >> _log2(tc))
    strict2 = jnp.where(same2 & (m_i < n_i), 1.0, 0.0).astype(F32)
    incl2 = jnp.where(same2 & (m_i <= n_i), 1.0, 0.0).astype(F32)
    incl_cat = jnp.concatenate([incl2, incl2], axis=1)
    head0 =jnp.where(lane < RW_N, 1.0, 0.0).astype(F32)
    head1 = 1.0 - head0
    ones_head = jnp.where((lax.broadcasted_iota(jnp.int32, (LANES, LANES), 0) >> _log2(RW_N))
                          == (lax.broadcasted_iota(jnp.int32, (LANES, LANES), 1) >> _log2(RW_N)),
                          1.0, 0.0).astype(F32)
    head0_b, head1_b = head0.astype(BF16), head1.astype(BF16)

    def stack(a):
        ab = a.astype(BF16)
        return jnp.concatenate([ab * head0_b, ab * head1_b], axis=0)

    def head_sum(a):
        s0 = jnp.sum(a * head0, axis=-1, keepdims=True)
        s1 = jnp.sum(a * head1, axis=-1, keepdims=True)
        return jnp.where(lane < RW_N, s0, s1)

    def tokens(ref, rows):
        return [ref[g, rows, p * LANES:(p + 1) * LANES].astype(F32) for g, p in units]

    def par(u, row):
        return par_ref[units[u][1], row:row + 1, :]

    def chunk(j, carry):
        rows = pl.ds(pl.multiple_of(j * CHUNK, CHUNK), CHUNK)
        r, k, v = tokens(r_ref, rows), tokens(k_ref, rows), tokens(v_ref, rows)
        lw, a_gate = tokens(lw_ref, rows), tokens(as_ref, rows)

        c = [_sum_rows(tri, x) for x in lw]
        c_end = [x[CHUNK - 1:CHUNK, :] for x in c] if gseq == 1 else [_sum_rows(ones_seq, x) for x in lw]
        kk = [k[p] * par(p, 0) for p in pairs]
        kk = [x * lax.rsqrt(jnp.maximum(head_sum(x * x), 1e-24)) for x in kk]
        b_vec = [kk[p] * a_gate[p] for p in pairs]
        k_mod = [k[p] * (1.0 + (a_gate[p] - 1.0) * par(p, 1)) for p in pairs]
        bonus = [head_sum(r[p] * k_mod[p] * par(p, 2)) for p in pairs]
        e_neg = [jnp.exp(-x) for x in c]
        e_end = [jnp.exp(c_end[p] - c[p]) for p in pairs]
        b_hat = [narrow(b_vec[p] * e_end[p]) for p in pairs]
        k_hat = [narrow(k_mod[p] * e_end[p]) for p in pairs]
        a_t = [-kk[p] * jnp.exp(c[p] - lw[p]) for p in pairs]
        r_t = [r[p] * jnp.exp(c[p]) for p in pairs]

        lhs = [jnp.concatenate([a_t[p], r_t[p]], axis=0).astype(BF16) for p in pairs]
        rhs = [jnp.concatenate([stack(b_vec[p] * e_neg[p]), stack(k_mod[p] * e_neg[p])], axis=0) for p in pairs]
        big = [lax.dot_general(lhs[p], rhs[p], NT, preferred_element_type=F32) for p in pairs]
        l_ab = [x[0:CHUNK, 0:LANES] * strict2 for x in big]
        l_ak = [(x[0:CHUNK, LANES:2 * LANES] * strict2).astype(BF16) for x in big]
        m_cat = [(x[CHUNK:2 * CHUNK, :] * incl_cat).astype(BF16) for x in big]
        v_s = [stack(x) for x in v]

        ah0, rh0 = [], []
        for gi in range(gseq):
            sl = slice(gi * tc, (gi + 1) * tc)
            sidx = j * gseq + gi if n_states > 1 else 0
            if single:
                unpack_states(sidx)
            st = [st_ref[g, sidx, p] for g, p in units]
            ar =[jnp.concatenate([a_t[p][sl], r_t[p][sl]], axis=0) for p in pairs]
            prod = [lax.dot_general(narrow(ar[p]), narrow(st[p]), NT, preferred_element_type=F32) for p in pairs]
            ah0.append([x[0:tc] for x in prod])
            rh0.append([x[tc:2 * tc] for x in prod])
        ah0 = [ah0[0][p] if gseq == 1 else jnp.concatenate([part[p] for part in ah0], axis=0) for p in pairs]
        rh0 = [rh0[0][p] if gseq == 1 else jnp.concatenate([part[p] for part in rh0], axis=0) for p in pairs]

        u = [ah0[p] + jnp.dot(l_ak[p], v_s[p], preferred_element_type=F32) for p in pairs]
        pw = l_ab
        for lev in range(n_levels):
            pb = [m.astype(BF16) for m in pw]
            u = [u[p] + jnp.dot(pb[p], stack(u[p]), preferred_element_type=F32) for p in pairs]
            if lev < n_levels - 1:
                pw = [jnp.dot(pb[p], stack(pw[p]), preferred_element_type=F32) for p in pairs]
        y = [rh0[p] + jnp.dot(m_cat[p], jnp.concatenate([stack(u[p]), v_s[p]], axis=0),
                              preferred_element_type=F32) for p in pairs]

        v = tokens(v_ref, rows)
        for gi in range(gseq):
            sl = slice(gi * tc, (gi + 1) * tc)
            sidx = j * gseq + gi if n_states > 1 else 0
            uv = [jnp.concatenate([u[p][sl], v[p][sl]], axis=0) for p in pairs]
            bk = [jnp.concatenate([b_hat[p][sl], k_hat[p][sl]], axis=0) for p in pairs]
            upd = [lax.dot_general(narrow(uv[p]), bk[p], TN, preferred_element_type=F32) for p in pairs]
            for p, (g, pp) in enumerate(units):
                decay = c_end[p] if gseq == 1 else c_end[p][gi * tc:gi * tc + 1, :]
                st_ref[g, sidx, pp] = st_ref[g, sidx, pp] * jnp.exp(decay) + upd[p] * ones_head
            if single:
                pack_states(sidx)

        mean = [head_sum(m) * (1.0 / RW_N) for m in y]
        d = [y[p] - mean[p] for p in pairs]
        var = [head_sum(m * m) * (1.0 / RW_N) for m in d]
        for p in pairs:
            out = d[p] * lax.rsqrt(var[p] + RW_LNX_EPS) * par(p, 3) + par(p, 4)
            g, pp = units[p]
            y_ref[g, rows, pp * LANES:(pp + 1) * LANES] = (out + bonus[p] * v[p]).astype(y_ref.dtype)
        return carry

    lax.fori_loop(0, n_chunks, chunk, 0)

    if not single:
        @pl.when(i % steps_per_seq == steps_per_seq - 1)
        def _():
            def store(s, c):
                pack_states(s)
                return c
            lax.fori_loop(0, n_states, store, 0)


def _rwkv_scan(r, k, v, lw, a_gate, s_all, layer, s_new_all, par, tseq):
    return _layer_state_call(
        _rwkv_scan_kernel, "rwkv_scan", 2 * SCAN_WIDTH, (RW_N // 2, LANES), RW_PAIRS // SCAN_WIDTH, SCAN_WIDTH * LANES,
        [r, k, v, lw, a_gate], [par], [pl.BlockSpec((SCAN_WIDTH, 8, LANES), lambda p, i: (p, 0, 0))],
        s_all, layer, s_new_all, tseq)


def _cross_attention(q_ref, k_ref, v_ref, rows_per_seq, interleaved):
    n_seq = k_ref.shape[0]
    scale = X_DH ** -0.5
    units = [(s, hh) for s in range(n_seq) for hh in range(X_HEADS)]

    def memory(ref, s, hh):
        if interleaved:
            return ref[s, pl.ds(hh, MEM_LEN, stride=X_HEADS), :].astype(BF16)
        return ref[s, :, hh * X_DH:(hh + 1) * X_DH].astype(BF16)

    q = [q_ref[s * rows_per_seq:(s + 1) * rows_per_seq, hh * X_DH:(hh + 1) * X_DH].astype(BF16) for s, hh in units]
    sc = [lax.dot_general(q[u], memory(k_ref, s, hh), NT, preferred_element_type=F32) * scale
          for u, (s, hh) in enumerate(units)]
    p = [jnp.exp(x - jnp.max(x, axis=-1, keepdims=True)) for x in sc]
    denom = [jnp.sum(x, axis=-1, keepdims=True) for x in p]
    o = [jnp.dot(p[u].astype(BF16), memory(v_ref, s, hh), preferred_element_type=F32) / denom[u]
         for u, (s, hh) in enumerate(units)]
    return [jnp.concatenate(o[s * X_HEADS:(s + 1) * X_HEADS], axis=-1) for s in range(n_seq)]


def _out_kernel(x_ref, mix_ref, xq_ref, k_ref, v_ref, sg_ref, w_ref, fg_ref, o_ref, *, rows_per_seq, interleaved,
                final):
    xo = _cross_attention(xq_ref, k_ref, v_ref, rows_per_seq, interleaved)
    xo = xo[0] if len(xo) == 1 else jnp.concatenate(xo, axis=0)
    sg = sg_ref[...].astype(F32)
    left = (mix_ref[...].astype(F32) * sg[:, :MIX_W]).astype(BF16)
    right = (xo * sg[:, MIX_W:]).astype(BF16)
    x = x_ref[...] + jnp.dot(left, w_ref[0:MIX_W, :], preferred_element_type=F32) \
        + jnp.dot(right, w_ref[MIX_W:BRANCH_W, :], preferred_element_type=F32)
    o_ref[...] = _rms(x, fg_ref[...]) if final else x


def _out_proj(x2, mix, xq, mk_all, mv_all, layer, tseq, sg, w, final_g, final):
    n = x2.shape[0]
    interleaved = mk_all.shape[-1] == X_DH
    rb = min(1024, n, 8 * tseq)
    n_seq = max(1, rb // tseq)
    steps_per_seq = max(1, tseq // rb)
    row = lambda width: pl.BlockSpec((rb, width), lambda i: (i, 0))
    mem = pl.BlockSpec((None, n_seq) + mk_all.shape[2:], lambda i: (layer, i // steps_per_seq, 0, 0))
    return pl.pallas_call(
        functools.partial(_out_kernel, rows_per_seq=rb // n_seq, interleaved=interleaved, final=final),
        grid=(n // rb,),
        in_specs=[row(D_MODEL), row(MIX_W), row(CROSS_W), mem, mem, row(BRANCH_W),
                  _resident((BRANCH_W, D_MODEL), lambda i: (0, 0)),
                  _resident((1, D_MODEL), lambda i: (0, 0))],
        out_specs=row(D_MODEL),
        out_shape=jax.ShapeDtypeStruct((n, D_MODEL), F32),
        compiler_params=_params("arbitrary"),
        name="out_proj",
    )(x2, mix, xq, mk_all, mv_all, sg, w, final_g.reshape(1, D_MODEL))


def _trunk(x, mem_k, mem_v, s_hgrn, s_rwkv, s_shift, p):
    nb, tseq, _ = x.shape
    x2 = x.reshape(nb * tseq, D_MODEL)
    rwkv_shape = s_rwkv.shape
    s_rwkv = s_rwkv.reshape(rwkv_shape[:3] + (RW_N // 2, 2 * RW_N))
    new_h, new_r, new_s = None, None, []
    v_first = None
    for i in range(DEPTH):
        j = i // 2
        if i % 2 == 0:
            q, k, gl, v, xq, sg = _hgrn_proj(x2, p["norm_g"][i], p["w_in"][i], p["lbs"][j], tseq)
            mix, new_h = _hgrn_scan(q, k, v, gl, s_hgrn, j, new_h, p["hg_onorm_g"][j], tseq)
        else:
            vres = None if v_first is None else (p["rw_v0"][j - 1], p["rw_v1"][j - 1], p["rw_v2"][j - 1], v_first)
            r, k, v, lw, a_gate, xq, sg, shift = _rwkv_proj(
                x2, s_shift[j], tseq, p["norm_g"][i], p["rw_mu"][j], p["w_in"][i],
                p["rw_w0"][j], p["rw_w1"][j], p["rw_w2"][j], p["rw_a0"][j], p["rw_a1"][j], p["rw_a2"][j], vres)
            if v_first is None:
                v_first = v
            mix, new_r = _rwkv_scan(r, k, v, lw, a_gate, s_rwkv, j, new_r, p["rw_par"][j], tseq)
            new_s.append(shift)
        x2 = _out_proj(x2, mix, xq, mem_k, mem_v, i, tseq, sg, p["w_out"][i], p["final_g"], final=(i == DEPTH - 1))
    return x2.reshape(nb, tseq, D_MODEL), new_h, new_r.reshape(rwkv_shape), jnp.stack(new_s)


def kernel(x_prompt, x_sample, mem_prompt, state_hgrn, state_rwkv, state_shift, cache_mem_k, cache_mem_v,
           norm_g, w_in, w_out, mem_norm_g, w_mem_kv, hg_lb, hg_onorm_g, rw_mu, rw_w0, rw_w1, rw_w2,
           rw_a0, rw_a1, rw_a2, rw_v0, rw_v1, rw_v2, rw_kk, rw_ka, rw_rk, rw_lnx_g, rw_lnx_b, final_g):
    n_rwkv = rw_mu.shape[0]
    lbs = jax.nn.softmax(hg_lb.astype(F32), axis=0)
    lbs = jnp.cumsum(lbs, axis=0) - lbs[0]
    par = jnp.stack([rw_kk, rw_ka, rw_rk.reshape(n_rwkv, MIX_W), rw_lnx_g, rw_lnx_b], axis=1)
    par = jnp.concatenate([par, jnp.zeros((n_rwkv, 3, MIX_W), F32)], axis=1)
    par = par.reshape(n_rwkv, 8, RW_PAIRS, LANES).transpose(0, 2, 1, 3)
    p = dict(norm_g=norm_g, w_in=w_in.astype(BF16), w_out=w_out.astype(BF16), lbs=lbs, hg_onorm_g=hg_onorm_g,
             rw_mu=rw_mu, rw_w0=rw_w0, rw_w1=rw_w1.astype(BF16), rw_w2=rw_w2.astype(BF16),
             rw_a0=rw_a0, rw_a1=rw_a1.astype(BF16), rw_a2=rw_a2.astype(BF16),
             rw_v0=rw_v0, rw_v1=rw_v1.astype(BF16), rw_v2=rw_v2.astype(BF16), rw_par=par, final_g=final_g)

    nb, mem_len, _ = mem_prompt.shape
    mk, mv = _memory_kv(mem_prompt.reshape(nb * mem_len, D_MODEL), mem_norm_g, w_mem_kv.astype(BF16))
    mk = mk.reshape(DEPTH, nb, mem_len, CROSS_W)
    mv = mv.reshape(DEPTH, nb, mem_len, CROSS_W)
    z_h = jnp.zeros((state_hgrn.shape[0], nb) + state_hgrn.shape[2:], F32)
    z_r = jnp.zeros((state_rwkv.shape[0], nb) + state_rwkv.shape[2:], F32)
    z_s = jnp.zeros((state_shift.shape[0], nb, D_MODEL), F32)
    y_p, sh_p, sr_p, ss_p = _trunk(x_prompt, mk, mv, z_h, z_r, z_s, p)

    nbs = x_sample.shape[0]
    cmk = cache_mem_k.reshape(DEPTH, nbs, mem_len * X_HEADS, X_DH)
    cmv = cache_mem_v.reshape(DEPTH, nbs, mem_len * X_HEADS, X_DH)
    y_s, sh_s, sr_s, ss_s = _trunk(x_sample, cmk, cmv, state_hgrn, state_rwkv, state_shift, p)
    return (y_p, y_s, sh_p, sr_p, ss_p,
            mk.reshape(DEPTH, nb, mem_len, X_HEADS, X_DH), mv.reshape(DEPTH, nb, mem_len, X_HEADS, X_DH),
            sh_s, sr_s, ss_s)
```

```python
import functools
import math

import jax
import jax.numpy as jnp
from jax import lax
from jax.experimental import pallas as pl
from jax.experimental.pallas import tpu as pltpu

F32 = jnp.float32
BF16 = jnp.bfloat16

D_MODEL = 1024
DEPTH = 4
MIX_W = D_MODEL
HG_HEADS = 8
HG_D = MIX_W // HG_HEADS
RW_N = 64
RW_HEADS = MIX_W // RW_N
RW_PAIRS = RW_HEADS // 2
RW_LNX_EPS = 64e-5
MEM_LEN = 256
X_HEADS = 4
X_DH = 128
CROSS_W = X_HEADS * X_DH
BRANCH_W = MIX_W + CROSS_W
IN_COLS = 3 * MIX_W + CROSS_W + BRANCH_W
NORM_EPS = 1e-6
LOG_FLOOR = 1e-30
NEG_BIG = -1e30

LANES = 128
SUBLANES = 8
CHUNK = 64
COL_CHUNK = 256
VMEM_LIMIT = 56 * 1024 * 1024

NT = (((1,), (1,)), ((), ()))
TN = (((0,), (0,)), ((), ()))


def _params(*sem):
    return pltpu.CompilerParams(dimension_semantics=sem, vmem_limit_bytes=VMEM_LIMIT)


def _resident(shape, index_map):
    return pl.BlockSpec(shape, index_map, pipeline_mode=pl.Buffered(1))


def _rms(x, g):
    return x * lax.rsqrt(jnp.mean(x * x, axis=-1, keepdims=True) + NORM_EPS) * g


def _sigmoid(z):
    return 1.0 / (1.0 + jnp.exp(-z))


def _split3(x):
    hi = x.astype(BF16)
    rest = x - hi.astype(F32)
    mid = rest.astype(BF16)
    low = (rest - mid.astype(F32)).astype(BF16)
    return jnp.concatenate([hi, mid, low], axis=1)


def _sum_rows(sel, x):
    parts = jnp.dot(sel, _split3(x), preferred_element_type=F32)
    return parts[:, 0:LANES] + parts[:, LANES:2 * LANES] + parts[:, 2 * LANES:3 * LANES]


def _narrow(rows):
    if rows % 16 == 0:
        return lambda a: a.astype(BF16)
    return lambda a: a.astype(BF16).astype(F32)


def _log2(n):
    l = int(math.log2(n))
    assert 1 << l == n, n
    return l


def _memkv_kernel(x_ref, g_ref, w_ref, k_ref, v_ref):
    hb = _rms(x_ref[...], g_ref[...]).astype(BF16)
    kv = jnp.dot(hb, w_ref[...], preferred_element_type=F32)
    k_ref[...] = kv[:, :CROSS_W]
    v_ref[...] = kv[:, CROSS_W:]


def _memory_kv(mem2d, g, w):
    n = mem2d.shape[0]
    tm = min(512, n)
    out = jax.ShapeDtypeStruct((DEPTH, n, CROSS_W), F32)
    return pl.pallas_call(
        _memkv_kernel,
        grid=(DEPTH, n // tm),
        in_specs=[
            pl.BlockSpec((tm, D_MODEL), lambda l, i: (i, 0)),
            pl.BlockSpec((None, 1, D_MODEL), lambda l, i: (l, 0, 0)),
            pl.BlockSpec((None, D_MODEL, 2 * CROSS_W), lambda l, i: (l, 0, 0)),
        ],
        out_specs=[
            pl.BlockSpec((None, tm, CROSS_W), lambda l, i: (l, i, 0)),
            pl.BlockSpec((None, tm, CROSS_W), lambda l, i: (l, i, 0)),
        ],
        out_shape=[out, out],
        compiler_params=_params("arbitrary", "arbitrary"),
        name="memory_kv",
    )(mem2d, g.reshape(DEPTH, 1, D_MODEL), w)


def _hgrn_proj_kernel(x_ref, g_ref, w_ref, lb_ref, q_ref, k_ref, gl_ref, v_ref, xq_ref, sg_ref):
    hb = _rms(x_ref[...], g_ref[...]).astype(BF16)
    for j in range(IN_COLS // COL_CHUNK):
        c0 = j * COL_CHUNK
        z = jnp.dot(hb, w_ref[:, c0:c0 + COL_CHUNK], preferred_element_type=F32)
        if c0 < MIX_W:
            q_ref[:, c0:c0 + COL_CHUNK] = (z * _sigmoid(z)).astype(q_ref.dtype)
        elif c0 < 2 * MIX_W:
            d0 = c0 - MIX_W
            lb = lb_ref[:, d0:d0 + COL_CHUNK]
            f = lb + (1.0 - lb) * _sigmoid(z)
            gl_ref[:, d0:d0 + COL_CHUNK] = jnp.log(jnp.maximum(f, LOG_FLOOR))
            k_ref[:, d0:d0 + COL_CHUNK] = ((1.0 - lb) * _sigmoid(-z)).astype(k_ref.dtype)
        elif c0 < 3 * MIX_W:
            d0 = c0 - 2 * MIX_W
            v_ref[:, d0:d0 + COL_CHUNK] = z.astype(v_ref.dtype)
        elif c0 < 3 * MIX_W + CROSS_W:
            d0 = c0 - 3 * MIX_W
            xq_ref[:, d0:d0 + COL_CHUNK] = z.astype(xq_ref.dtype)
        else:
            d0 = c0 - 3 * MIX_W - CROSS_W
            sg_ref[:, d0:d0 + COL_CHUNK] = (z * _sigmoid(z)).astype(sg_ref.dtype)


def _xq_dtype(tseq):
    return BF16 if tseq % 16 == 0 else F32


def _hgrn_proj(x2, g, w, lb, tseq):
    n = x2.shape[0]
    tm = min(1024, n)
    row = lambda width: pl.BlockSpec((tm, width), lambda i: (i, 0))
    sd = lambda width, dt: jax.ShapeDtypeStruct((n, width), dt)
    return pl.pallas_call(
        _hgrn_proj_kernel,
        grid=(n // tm,),
        in_specs=[
            row(D_MODEL),
            _resident((1, D_MODEL), lambda i: (0, 0)),
            _resident((D_MODEL, IN_COLS), lambda i: (0, 0)),
            _resident((1, MIX_W), lambda i: (0, 0)),
        ],
        out_specs=[row(MIX_W), row(MIX_W), row(MIX_W), row(MIX_W), row(CROSS_W), row(BRANCH_W)],
        out_shape=[sd(MIX_W, BF16), sd(MIX_W, BF16), sd(MIX_W, F32), sd(MIX_W, BF16),
                   sd(CROSS_W, _xq_dtype(tseq)), sd(BRANCH_W, BF16)],
        compiler_params=_params("arbitrary"),
        name="hgrn_proj",
    )(x2, g.reshape(1, D_MODEL), w, lb.reshape(1, MIX_W))


def _own_layer(sout_all_ref, layer, aliased):
    return sout_all_ref if aliased else sout_all_ref.at[layer]


def _zero_other_layers(sout_all_ref, layer, aliased, s):
    if aliased:
        return
    for other in range(sout_all_ref.shape[0]):
        if other != layer:
            for g in range(sout_all_ref.shape[1]):
                sout_all_ref[other, g, s] = jnp.zeros(sout_all_ref.shape[3:], F32)


def _hgrn_scan_kernel(q_ref, k_ref, v_ref, g_ref, s0_ref, og_ref, *rest, rb, tseq, layer, aliased):
    o_ref, sout_all_ref, st_ref = rest[1:] if aliased else rest
    sout_ref = _own_layer(sout_all_ref, layer, aliased)
    tc = min(tseq, CHUNK)
    gseq = CHUNK // tc
    narrow = _narrow(tc)
    n_chunks = rb // CHUNK
    steps_per_seq = max(1, tseq // rb)
    groups, n_states, width = st_ref.shape[0], st_ref.shape[1], st_ref.shape[2]
    i = pl.program_id(1)
    units = [(g, hh) for g in range(groups) for hh in range(width)]
    single = tseq < CHUNK

    if not single:
        @pl.when(i % steps_per_seq == 0)
        def _():
            def load(s, c):
                for g, hh in units:
                    st_ref[g, s, hh] = s0_ref[g, s, hh].T
                return c
            lax.fori_loop(0, n_states, load, 0)

    t_i = lax.broadcasted_iota(jnp.int32, (CHUNK, CHUNK), 0)
    s_i = lax.broadcasted_iota(jnp.int32, (CHUNK, CHUNK), 1)
    same = (t_i >> _log2(tc)) == (s_i >> _log2(tc))
    tri = jnp.where(same & (s_i <= t_i), 1.0, 0.0).astype(BF16)
    ones_seq = jnp.where(same, 1.0, 0.0).astype(BF16)
    row = lax.broadcasted_iota(jnp.int32, (CHUNK, 1), 0)
    og = og_ref[...]
    halves = [1 << lev for lev in range(_log2(tc))]
    late = [(row & half) != 0 for half in halves]
    pair_mask = [jnp.where(((t_i >> _log2(2 * half)) == (s_i >> _log2(2 * half)))
                           & ((t_i & half) != 0) & ((s_i & half) == 0), 1.0, 0.0).astype(F32) for half in halves]
    small = [half for half in halves if 2 * half < SUBLANES]
    pick = jnp.concatenate([jnp.where(s_i == (t_i & ~(2 * half - 1)) + half - 1, 1.0, 0.0)
                            for half in small], axis=0).astype(BF16)

    if single:
        piece_row = lax.broadcasted_iota(jnp.int32, (3 * HG_D, HG_D), 0) & (HG_D - 1)
        spread = [jnp.where(piece_row == gi * tc, 1.0, 0.0).astype(BF16) for gi in range(gseq)]

    heads = range(len(units))

    def tokens(ref, rows):
        return [ref[g, rows, hh * HG_D:(hh + 1) * HG_D] for g, hh in units]

    def chunk(j, carry):
        rows = pl.ds(pl.multiple_of(j * CHUNK, CHUNK), CHUNK)
        q = [x.astype(F32) for x in tokens(q_ref, rows)]
        k = [x.astype(F32) for x in tokens(k_ref, rows)]
        v = [x.astype(F32) for x in tokens(v_ref, rows)]
        g = tokens(g_ref, rows)
        b = [_sum_rows(tri, x) for x in g]
        b_end = [x[CHUNK - 1:CHUNK, :] for x in b] if gseq == 1 else [_sum_rows(ones_seq, x) for x in g]
        q_in = [q[h] * jnp.exp(b[h]) for h in heads]
        k_dec = [k[h] * jnp.exp(b_end[h] - b[h]) for h in heads]
        vb = [x.astype(BF16) for x in v]

        o = []
        if single:
            decay_t = [_split3(jnp.concatenate([jnp.exp(b_end[h]), jnp.zeros((HG_D - CHUNK, HG_D), F32)], axis=0).T)
                       for h in heads]
        for gi in range(gseq):
            sl = slice(gi * tc, (gi + 1) * tc)
            sidx = j * gseq + gi if n_states > 1 else 0
            if single:
                st = [s0_ref[g, sidx, hh] for g, hh in units]
                o.append([jnp.dot(narrow(q_in[h][sl]), narrow(st[h]), preferred_element_type=F32) for h in heads])
                upd = [lax.dot_general(narrow(k_dec[h][sl]), narrow(v[h][sl]), TN, preferred_element_type=F32)
                       for h in heads]
                decay = [jnp.dot(decay_t[h], spread[gi], preferred_element_type=F32) for h in heads]
                for h, (g, hh) in enumerate(units):
                    sout_ref[g, sidx, hh] = st[h] * decay[h] + upd[h]
                _zero_other_layers(sout_all_ref, layer, aliased, sidx)
                continue
            st = [st_ref[g, sidx, hh] for g, hh in units]
            o.append([lax.dot_general(narrow(q_in[h][sl]), narrow(st[h]), NT, preferred_element_type=F32)
                      for h in heads])
            upd = [lax.dot_general(narrow(v[h][sl]), narrow(k_dec[h][sl]), TN, preferred_element_type=F32)
                   for h in heads]
            for h in heads:
                decay = b_end[h] if gseq == 1 else b_end[h][gi * tc:gi * tc + 1, :]
                st_ref[units[h][0], sidx, units[h][1]] = st[h] * jnp.exp(decay) + upd[h]
        o = [o[0][h] if gseq == 1 else jnp.concatenate([part[h] for part in o], axis=0) for h in heads]

        beta_small = [_sum_rows(pick, x) for x in b]
        att = [None for _ in heads]
        for lev, (half, is_late, mask) in enumerate(zip(halves, late, pair_mask)):
            for h in heads:
                if half in small:
                    beta = beta_small[h][lev * CHUNK:(lev + 1) * CHUNK]
                else:
                    blocks = b[h].reshape(CHUNK // (2 * half), 2 * half, HG_D)
                    beta = jnp.broadcast_to(blocks[:, half - 1:half, :], blocks.shape).reshape(CHUNK, HG_D)
                q_l = (q[h] * jnp.exp(jnp.where(is_late, b[h] - beta, NEG_BIG))).astype(BF16)
                k_l = (k[h] * jnp.exp(jnp.where(is_late, NEG_BIG, beta - b[h]))).astype(BF16)
                part = lax.dot_general(q_l, k_l, NT, preferred_element_type=F32) * mask
                att[h] = part if att[h] is None else att[h] + part
        o = [o[h] + jnp.dot(att[h].astype(BF16), vb[h], preferred_element_type=F32) for h in heads]
        o = [o[h] + jnp.sum(q[h] * k[h], axis=-1, keepdims=True) * v[h] for h in heads]

        for h in heads:
            out = o[h] * lax.rsqrt(jnp.mean(o[h] * o[h], axis=-1, keepdims=True) + NORM_EPS) * og
            g, hh = units[h]
            o_ref[g, rows, hh * HG_D:(hh + 1) * HG_D] = out.astype(o_ref.dtype)
        return carry

    lax.fori_loop(0, n_chunks, chunk, 0)

    if not single:
        @pl.when(i % steps_per_seq == steps_per_seq - 1)
        def _():
            def store(s, c):
                for g, hh in units:
                    sout_ref[g, s, hh] = st_ref[g, s, hh].T
                _zero_other_layers(sout_all_ref, layer, aliased, s)
                return c
            lax.fori_loop(0, n_states, store, 0)


SCAN_WIDTH = 8
SCAN_GROUPS = 2
SCAN_ROWS = 256
STATE_BLOCK_BYTES = 4 * 1024 * 1024


def _scan_geometry(n, tseq):
    rb = min(SCAN_ROWS, n)
    if tseq < rb:
        max_states = STATE_BLOCK_BYTES // (SCAN_WIDTH * LANES * LANES * 4)
        rb = min(rb, max(CHUNK, max_states * tseq))
    assert rb % CHUNK == 0 and n % rb == 0
    assert (tseq % rb == 0) or (rb % tseq == 0 and CHUNK % tseq == 0)
    n_states = max(1, rb // tseq)
    steps_per_seq = max(1, tseq // rb)
    return rb, n_states, steps_per_seq


def _layer_state_call(kernel_fn, name, heads, state_tail, grid0, tok_width, tok_inputs, consts, const_specs,
                      s_all, layer, s_new_all, tseq):
    n = tok_inputs[0].shape[0]
    layers, nb = s_all.shape[0], s_all.shape[1]
    groups = SCAN_GROUPS if tseq >= CHUNK else 1
    assert n % groups == 0 and nb % groups == 0
    rb, n_states, steps_per_seq = _scan_geometry(n // groups, tseq)
    grouped = (layers, groups, nb // groups) + s_all.shape[2:]
    tok = pl.BlockSpec((groups, rb, tok_width), lambda h, i: (0, i, h))
    state = pl.BlockSpec((None, groups, n_states, heads) + state_tail,
                         lambda h, i: (layer, 0, i // steps_per_seq, h) + (0,) * len(state_tail))
    in_specs = [tok] * len(tok_inputs) + [state] + const_specs
    args = [t.reshape(groups, n // groups, t.shape[1]) for t in tok_inputs]
    args += [s_all.reshape(grouped)] + consts
    aliases = {}
    if s_new_all is not None:
        in_specs.append(pl.BlockSpec(memory_space=pl.ANY))
        args.append(s_new_all.reshape(grouped))
        aliases = {len(args) - 1: 1}
        state_out = state
    else:
        state_out = pl.BlockSpec((layers, groups, n_states, heads) + state_tail,
                                 lambda h, i: (0, 0, i // steps_per_seq, h) + (0,) * len(state_tail))
    out, s_new = pl.pallas_call(
        functools.partial(kernel_fn, rb=rb, tseq=tseq, layer=layer, aliased=s_new_all is not None),
        grid=(grid0, n // groups // rb),
        in_specs=in_specs,
        out_specs=[tok, state_out],
        out_shape=[jax.ShapeDtypeStruct((groups, n // groups, MIX_W), BF16),
                   jax.ShapeDtypeStruct(grouped, F32)],
        scratch_shapes=[pltpu.VMEM((groups, n_states, SCAN_WIDTH, LANES, LANES), F32)],
        input_output_aliases=aliases,
        compiler_params=_params("arbitrary", "arbitrary"),
        name=name,
    )(*args)
    return out.reshape(n, MIX_W), s_new.reshape(s_all.shape)


def _hgrn_scan(q, k, v, gl, s_all, layer, s_new_all, og, tseq):
    return _layer_state_call(
        _hgrn_scan_kernel, "hgrn_scan", SCAN_WIDTH, (HG_D, HG_D), HG_HEADS // SCAN_WIDTH, SCAN_WIDTH * HG_D,
        [q, k, v, gl], [og.reshape(1, HG_D)], [_resident((1, HG_D), lambda h, i: (0, 0))],
        s_all, layer, s_new_all, tseq)


def _rwkv_proj_kernel(*refs, tseq, long_seq, has_vres):
    it = iter(refs)
    x_ref, prev_ref, ng_ref, mu_ref, w_ref = (next(it) for _ in range(5))
    w0_ref, w1_ref, w2_ref, a0_ref, a1_ref, a2_ref = (next(it) for _ in range(6))
    if has_vres:
        v0_ref, v1_ref, v2_ref, vf_ref = (next(it) for _ in range(4))
    r_ref, k_ref, v_ref, lw_ref, as_ref, xq_ref, sg_ref, h_ref = (next(it) for _ in range(8))
    carry_ref = next(it) if long_seq else None

    x = x_ref[0] if long_seq else x_ref[...]
    tm = x.shape[0]
    h = _rms(x, ng_ref[...])
    shifted = pltpu.roll(h, 1, 0)
    row = lax.broadcasted_iota(jnp.int32, (tm, 1), 0)
    if long_seq:
        tb = pl.program_id(1)

        @pl.when(tb == 0)
        def _():
            carry_ref[0:1, :] = prev_ref[0]

        h_prev = jnp.where(row == 0, carry_ref[0:1, :], shifted)
        carry_ref[0:1, :] = h[tm - 1:tm, :]

        @pl.when(tb == pl.num_programs(1) - 1)
        def _():
            h_ref[0] = h[tm - 1:tm, :]
    else:
        h_prev = jnp.where((row & (tseq - 1)) == 0, prev_ref[...], shifted)
        h_ref[...] = h

    dx = h_prev - h
    hb = h.astype(BF16)
    xr = (h + dx * mu_ref[0:1, :]).astype(BF16)
    xw = (h + dx * mu_ref[1:2, :]).astype(BF16)
    xk = (h + dx * mu_ref[2:3, :]).astype(BF16)
    xv = (h + dx * mu_ref[3:4, :]).astype(BF16)
    xa = (h + dx * mu_ref[4:5, :]).astype(BF16)

    w_mid = jnp.tanh(jnp.dot(xw, w1_ref[...], preferred_element_type=F32)).astype(BF16)
    a_mid = jnp.dot(xa, a1_ref[...], preferred_element_type=F32).astype(BF16)
    if has_vres:
        v_mid = jnp.dot(xv, v1_ref[...], preferred_element_type=F32).astype(BF16)

    for j in range(MIX_W // COL_CHUNK):
        c0 = j * COL_CHUNK
        cols = slice(c0, c0 + COL_CHUNK)
        r_ref[:, cols] = jnp.dot(xr, w_ref[:, c0:c0 + COL_CHUNK], preferred_element_type=F32).astype(r_ref.dtype)
        k_ref[:, cols] = jnp.dot(xk, w_ref[:, MIX_W + c0:MIX_W + c0 + COL_CHUNK],
                                 preferred_element_type=F32).astype(k_ref.dtype)
        vz = jnp.dot(xv, w_ref[:, 2 * MIX_W + c0:2 * MIX_W + c0 + COL_CHUNK], preferred_element_type=F32)
        if has_vres:
            gate = _sigmoid(v0_ref[:, cols] + jnp.dot(v_mid, v2_ref[:, cols], preferred_element_type=F32))
            vz = vz + (vf_ref[:, cols].astype(F32) - vz) * gate
        v_ref[:, cols] = vz.astype(v_ref.dtype)
        u = w0_ref[:, cols] + jnp.dot(w_mid, w2_ref[:, cols], preferred_element_type=F32)
        softplus = jnp.maximum(-u, 0.0) + jnp.log(1.0 + jnp.exp(-jnp.abs(u)))
        lw_ref[:, cols] = -jnp.exp(-softplus - 0.5)
        as_ref[:, cols] = _sigmoid(a0_ref[:, cols] + jnp.dot(a_mid, a2_ref[:, cols], preferred_element_type=F32))

    base = 3 * MIX_W
    for j in range(CROSS_W // COL_CHUNK):
        c0 = j * COL_CHUNK
        xq_ref[:, c0:c0 + COL_CHUNK] = jnp.dot(hb, w_ref[:, base + c0:base + c0 + COL_CHUNK],
                                                preferred_element_type=F32).astype(xq_ref.dtype)
    base = 3 * MIX_W + CROSS_W
    for j in range(BRANCH_W // COL_CHUNK):
        c0 = j * COL_CHUNK
        z = jnp.dot(hb, w_ref[:, base + c0:base + c0 + COL_CHUNK], preferred_element_type=F32)
        sg_ref[:, c0:c0 + COL_CHUNK] = (z * _sigmoid(z)).astype(sg_ref.dtype)


def _rwkv_proj(x2, shift, tseq, ng, mu, w, w0, w1, w2, a0, a1, a2, vres):
    n = x2.shape[0]
    nb = n // tseq
    tm = min(512, n)
    long_seq = tseq >= tm
    has_vres = vres is not None
    vec = lambda a: a.reshape(1, -1)
    mu8 = jnp.zeros((8, D_MODEL), F32).at[:5].set(mu)
    consts = [vec(ng), mu8, w, vec(w0), w1, w2, vec(a0), a1, a2]
    if has_vres:
        v0, v1, v2, v_first = vres
        consts += [vec(v0), v1, v2]
    if long_seq:
        assert tseq % tm == 0
        steps = tseq // tm
        grid = (nb, steps)
        cmap = lambda b, t: (0, 0)
        row = lambda width: pl.BlockSpec((tm, width), lambda b, t: (b * steps + t, 0))
        x_in = x2.reshape(nb, tseq, D_MODEL)
        x_spec = pl.BlockSpec((1, tm, D_MODEL), lambda b, t: (b, t, 0))
        prev_in = shift.reshape(nb, 1, D_MODEL)
        prev_spec = pl.BlockSpec((1, 1, D_MODEL), lambda b, t: (b, 0, 0))
        h_shape = jax.ShapeDtypeStruct((nb, 1, D_MODEL), F32)
        h_spec = pl.BlockSpec((1, 1, D_MODEL), lambda b, t: (b, 0, 0))
        scratch = [pltpu.VMEM((8, D_MODEL), F32)]
        sem = ("arbitrary", "arbitrary")
    else:
        assert tm % tseq == 0 and tseq & (tseq - 1) == 0
        grid = (n // tm,)
        cmap = lambda i: (0, 0)
        row = lambda width: pl.BlockSpec((tm, width), lambda i: (i, 0))
        x_in, x_spec = x2, row(D_MODEL)
        prev_in = jnp.zeros((nb, tseq, D_MODEL), F32).at[:, 0].set(shift).reshape(n, D_MODEL)
        prev_spec = row(D_MODEL)
        h_shape = jax.ShapeDtypeStruct((n, D_MODEL), F32)
        h_spec = row(D_MODEL)
        scratch = []
        sem = ("arbitrary",)
    in_specs = [x_spec, prev_spec] + [_resident(c.shape, cmap) for c in consts]
    args = [x_in, prev_in] + consts
    if has_vres:
        in_specs.append(row(MIX_W))
        args.append(v_first)
    sd = lambda width, dt: jax.ShapeDtypeStruct((n, width), dt)
    outs = pl.pallas_call(
        functools.partial(_rwkv_proj_kernel, tseq=tseq, long_seq=long_seq, has_vres=has_vres),
        grid=grid,
        in_specs=in_specs,
        out_specs=[row(MIX_W)] * 5 + [row(CROSS_W), row(BRANCH_W), h_spec],
        out_shape=[sd(MIX_W, BF16)] * 3 + [sd(MIX_W, F32)] * 2
        + [sd(CROSS_W, _xq_dtype(tseq)), sd(BRANCH_W, BF16), h_shape],
        scratch_shapes=scratch,
        compiler_params=_params(*sem),
        name="rwkv_proj",
    )(*args)
    h_out = outs[7]
    new_shift = h_out.reshape(nb, D_MODEL) if long_seq else h_out.reshape(nb, tseq, D_MODEL)[:, -1]
    return list(outs[:7]) + [new_shift]


def _rwkv_scan_kernel(r_ref, k_ref, v_ref, lw_ref, as_ref, s0_ref, par_ref, *rest, rb, tseq, layer, aliased):
    y_ref, sout_all_ref, st_ref = rest[1:] if aliased else rest
    sout_ref = _own_layer(sout_all_ref, layer, aliased)
    tc = min(tseq, CHUNK)
    gseq = CHUNK // tc
    n_chunks = rb // CHUNK
    steps_per_seq = max(1, tseq // rb)
    groups, n_states, width = st_ref.shape[0], st_ref.shape[1], st_ref.shape[2]
    n_levels = _log2(tc)
    narrow = _narrow(tc)
    i = pl.program_id(1)

    units = [(g, p) for g in range(groups) for p in range(width)]
    pairs = range(len(units))
    lane = lax.broadcasted_iota(jnp.int32, (1, LANES), 1)

    def state_rows(head, parity):
        return pl.ds(head * RW_N + parity, RW_N // 2, stride=2)

    def unpack_states(s):
        for g, p in units:
            e, f = s0_ref[g, s, 2 * p], s0_ref[g, s, 2 * p + 1]
            st_ref[g, s, p, state_rows(0, 0), :] = jnp.where(lane < RW_N, e, 0.0)
            st_ref[g, s, p, state_rows(0, 1), :] = jnp.where(lane < RW_N, pltpu.roll(e, RW_N, 1), 0.0)
            st_ref[g, s, p, state_rows(1, 0), :] = jnp.where(lane >= RW_N, pltpu.roll(f, RW_N, 1), 0.0)
            st_ref[g, s, p, state_rows(1, 1), :] = jnp.where(lane >= RW_N, f, 0.0)

    def pack_states(s):
        for g, p in units:
            even0, odd0 = st_ref[g, s, p, state_rows(0, 0), :], st_ref[g, s, p, state_rows(0, 1), :]
            even1, odd1 = st_ref[g, s, p, state_rows(1, 0), :], st_ref[g, s, p, state_rows(1, 1), :]
            sout_ref[g, s, 2 * p] = jnp.where(lane < RW_N, even0, pltpu.roll(odd0, RW_N, 1))
            sout_ref[g, s, 2 * p + 1] = jnp.where(lane < RW_N, pltpu.roll(even1, RW_N, 1), odd1)
        _zero_other_layers(sout_all_ref, layer, aliased, s)

    single = tseq < CHUNK

    if not single:
        @pl.when(i % steps_per_seq == 0)
        def _():
            def load(s, c):
                unpack_states(s)
                return c
            lax.fori_loop(0, n_states, load, 0)

    t_i = lax.broadcasted_iota(jnp.int32, (CHUNK, CHUNK), 0)
    s_i = lax.broadcasted_iota(jnp.int32, (CHUNK, CHUNK), 1)
    same = (t_i >> _log2(tc)) == (s_i >> _log2(tc))
    tri = jnp.where(same & (s_i <= t_i), 1.0, 0.0).astype(BF16)
    ones_seq = jnp.where(same, 1.0, 0.0).astype(BF16)
    n_i = lax.broadcasted_iota(jnp.int32, (CHUNK, 2 * CHUNK), 0)
    m_i = lax.broadcasted_iota(jnp.int32, (CHUNK, 2 * CHUNK), 1) & (CHUNK - 1)
    same2 = (n_i >> _log2(tc)) == (m_i >> _log2(tc))
    strict2 = jnp.where(same2 & (m_i < n_i), 1.0, 0.0).astype(F32)
    incl2 = jnp.where(same2 & (m_i <= n_i), 1.0, 0.0).astype(F32)
    incl_cat = jnp.concatenate([incl2, incl2], axis=1)
    head0 =jnp.where(lane < RW_N, 1.0, 0.0).astype(F32)
    head1 = 1.0 - head0
    ones_head = jnp.where((lax.broadcasted_iota(jnp.int32, (LANES, LANES), 0) >> _log2(RW_N))
                          == (lax.broadcasted_iota(jnp.int32, (LANES, LANES), 1) >> _log2(RW_N)),
                          1.0, 0.0).astype(F32)
    head0_b, head1_b = head0.astype(BF16), head1.astype(BF16)

    def stack(a):
        ab = a.astype(BF16)
        return jnp.concatenate([ab * head0_b, ab * head1_b], axis=0)

    def head_sum(a):
        s0 = jnp.sum(a * head0, axis=-1, keepdims=True)
        s1 = jnp.sum(a * head1, axis=-1, keepdims=True)
        return jnp.where(lane < RW_N, s0, s1)

    def tokens(ref, rows):
        return [ref[g, rows, p * LANES:(p + 1) * LANES].astype(F32) for g, p in units]

    def par(u, row):
        return par_ref[units[u][1], row:row + 1, :]

    def chunk(j, carry):
        rows = pl.ds(pl.multiple_of(j * CHUNK, CHUNK), CHUNK)
        r, k, v = tokens(r_ref, rows), tokens(k_ref, rows), tokens(v_ref, rows)
        lw, a_gate = tokens(lw_ref, rows), tokens(as_ref, rows)

        c = [_sum_rows(tri, x) for x in lw]
        c_end = [x[CHUNK - 1:CHUNK, :] for x in c] if gseq == 1 else [_sum_rows(ones_seq, x) for x in lw]
        kk = [k[p] * par(p, 0) for p in pairs]
        kk = [x * lax.rsqrt(jnp.maximum(head_sum(x * x), 1e-24)) for x in kk]
        b_vec = [kk[p] * a_gate[p] for p in pairs]
        k_mod = [k[p] * (1.0 + (a_gate[p] - 1.0) * par(p, 1)) for p in pairs]
        bonus = [head_sum(r[p] * k_mod[p] * par(p, 2)) for p in pairs]
        e_neg = [jnp.exp(-x) for x in c]
        e_end = [jnp.exp(c_end[p] - c[p]) for p in pairs]
        b_hat = [narrow(b_vec[p] * e_end[p]) for p in pairs]
        k_hat = [narrow(k_mod[p] * e_end[p]) for p in pairs]
        a_t = [-kk[p] * jnp.exp(c[p] - lw[p]) for p in pairs]
        r_t = [r[p] * jnp.exp(c[p]) for p in pairs]

        lhs = [jnp.concatenate([a_t[p], r_t[p]], axis=0).astype(BF16) for p in pairs]
        rhs = [jnp.concatenate([stack(b_vec[p] * e_neg[p]), stack(k_mod[p] * e_neg[p])], axis=0) for p in pairs]
        big = [lax.dot_general(lhs[p], rhs[p], NT, preferred_element_type=F32) for p in pairs]
        l_ab = [x[0:CHUNK, 0:LANES] * strict2 for x in big]
        l_ak = [(x[0:CHUNK, LANES:2 * LANES] * strict2).astype(BF16) for x in big]
        m_cat = [(x[CHUNK:2 * CHUNK, :] * incl_cat).astype(BF16) for x in big]
        v_s = [stack(x) for x in v]

        ah0, rh0 = [], []
        for gi in range(gseq):
            sl = slice(gi * tc, (gi + 1) * tc)
            sidx = j * gseq + gi if n_states > 1 else 0
            if single:
                unpack_states(sidx)
            st = [st_ref[g, sidx, p] for g, p in units]
            ar =[jnp.concatenate([a_t[p][sl], r_t[p][sl]], axis=0) for p in pairs]
            prod = [lax.dot_general(narrow(ar[p]), narrow(st[p]), NT, preferred_element_type=F32) for p in pairs]
            ah0.append([x[0:tc] for x in prod])
            rh0.append([x[tc:2 * tc] for x in prod])
        ah0 = [ah0[0][p] if gseq == 1 else jnp.concatenate([part[p] for part in ah0], axis=0) for p in pairs]
        rh0 = [rh0[0][p] if gseq == 1 else jnp.concatenate([part[p] for part in rh0], axis=0) for p in pairs]

        u = [ah0[p] + jnp.dot(l_ak[p], v_s[p], preferred_element_type=F32) for p in pairs]
        pw = l_ab
        for lev in range(n_levels):
            pb = [m.astype(BF16) for m in pw]
            u = [u[p] + jnp.dot(pb[p], stack(u[p]), preferred_element_type=F32) for p in pairs]
            if lev < n_levels - 1:
                pw = [jnp.dot(pb[p], stack(pw[p]), preferred_element_type=F32) for p in pairs]
        y = [rh0[p] + jnp.dot(m_cat[p], jnp.concatenate([stack(u[p]), v_s[p]], axis=0),
                              preferred_element_type=F32) for p in pairs]

        v = tokens(v_ref, rows)
        for gi in range(gseq):
            sl = slice(gi * tc, (gi + 1) * tc)
            sidx = j * gseq + gi if n_states > 1 else 0
            uv = [jnp.concatenate([u[p][sl], v[p][sl]], axis=0) for p in pairs]
            bk = [jnp.concatenate([b_hat[p][sl], k_hat[p][sl]], axis=0) for p in pairs]
            upd = [lax.dot_general(narrow(uv[p]), bk[p], TN, preferred_element_type=F32) for p in pairs]
            for p, (g, pp) in enumerate(units):
                decay = c_end[p] if gseq == 1 else c_end[p][gi * tc:gi * tc + 1, :]
                st_ref[g, sidx, pp] = st_ref[g, sidx, pp] * jnp.exp(decay) + upd[p] * ones_head
            if single:
                pack_states(sidx)

        mean = [head_sum(m) * (1.0 / RW_N) for m in y]
        d = [y[p] - mean[p] for p in pairs]
        var = [head_sum(m * m) * (1.0 / RW_N) for m in d]
        for p in pairs:
            out = d[p] * lax.rsqrt(var[p] + RW_LNX_EPS) * par(p, 3) + par(p, 4)
            g, pp = units[p]
            y_ref[g, rows, pp * LANES:(pp + 1) * LANES] = (out + bonus[p] * v[p]).astype(y_ref.dtype)
        return carry

    lax.fori_loop(0, n_chunks, chunk, 0)

    if not single:
        @pl.when(i % steps_per_seq == steps_per_seq - 1)
        def _():
            def store(s, c):
                pack_states(s)
                return c
            lax.fori_loop(0, n_states, store, 0)


def _rwkv_scan(r, k, v, lw, a_gate, s_all, layer, s_new_all, par, tseq):
    return _layer_state_call(
        _rwkv_scan_kernel, "rwkv_scan", 2 * SCAN_WIDTH, (RW_N // 2, LANES), RW_PAIRS // SCAN_WIDTH, SCAN_WIDTH * LANES,
        [r, k, v, lw, a_gate], [par], [pl.BlockSpec((SCAN_WIDTH, 8, LANES), lambda p, i: (p, 0, 0))],
        s_all, layer, s_new_all, tseq)


def _cross_attention(q_ref, k_ref, v_ref, rows_per_seq, interleaved):
    n_seq = k_ref.shape[0]
    scale = X_DH ** -0.5
    units = [(s, hh) for s in range(n_seq) for hh in range(X_HEADS)]

    def memory(ref, s, hh):
        if interleaved:
            return ref[s, pl.ds(hh, MEM_LEN, stride=X_HEADS), :].astype(BF16)
        return ref[s, :, hh * X_DH:(hh + 1) * X_DH].astype(BF16)

    q = [q_ref[s * rows_per_seq:(s + 1) * rows_per_seq, hh * X_DH:(hh + 1) * X_DH].astype(BF16) for s, hh in units]
    sc = [lax.dot_general(q[u], memory(k_ref, s, hh), NT, preferred_element_type=F32) * scale
          for u, (s, hh) in enumerate(units)]
    p = [jnp.exp(x - jnp.max(x, axis=-1, keepdims=True)) for x in sc]
    denom = [jnp.sum(x, axis=-1, keepdims=True) for x in p]
    o = [jnp.dot(p[u].astype(BF16), memory(v_ref, s, hh), preferred_element_type=F32) / denom[u]
         for u, (s, hh) in enumerate(units)]
    return [jnp.concatenate(o[s * X_HEADS:(s + 1) * X_HEADS], axis=-1) for s in range(n_seq)]


def _out_kernel(x_ref, mix_ref, xq_ref, k_ref, v_ref, sg_ref, w_ref, fg_ref, o_ref, *, rows_per_seq, interleaved,
                final):
    xo = _cross_attention(xq_ref, k_ref, v_ref, rows_per_seq, interleaved)
    xo = xo[0] if len(xo) == 1 else jnp.concatenate(xo, axis=0)
    sg = sg_ref[...].astype(F32)
    left = (mix_ref[...].astype(F32) * sg[:, :MIX_W]).astype(BF16)
    right = (xo * sg[:, MIX_W:]).astype(BF16)
    x = x_ref[...] + jnp.dot(left, w_ref[0:MIX_W, :], preferred_element_type=F32) \
        + jnp.dot(right, w_ref[MIX_W:BRANCH_W, :], preferred_element_type=F32)
    o_ref[...] = _rms(x, fg_ref[...]) if final else x


def _out_proj(x2, mix, xq, mk_all, mv_all, layer, tseq, sg, w, final_g, final):
    n = x2.shape[0]
    interleaved = mk_all.shape[-1] == X_DH
    rb = min(1024, n, 8 * tseq)
    n_seq = max(1, rb // tseq)
    steps_per_seq = max(1, tseq // rb)
    row = lambda width: pl.BlockSpec((rb, width), lambda i: (i, 0))
    mem = pl.BlockSpec((None, n_seq) + mk_all.shape[2:], lambda i: (layer, i // steps_per_seq, 0, 0))
    return pl.pallas_call(
        functools.partial(_out_kernel, rows_per_seq=rb // n_seq, interleaved=interleaved, final=final),
        grid=(n // rb,),
        in_specs=[row(D_MODEL), row(MIX_W), row(CROSS_W), mem, mem, row(BRANCH_W),
                  _resident((BRANCH_W, D_MODEL), lambda i: (0, 0)),
                  _resident((1, D_MODEL), lambda i: (0, 0))],
        out_specs=row(D_MODEL),
        out_shape=jax.ShapeDtypeStruct((n, D_MODEL), F32),
        compiler_params=_params("arbitrary"),
        name="out_proj",
    )(x2, mix, xq, mk_all, mv_all, sg, w, final_g.reshape(1, D_MODEL))


def _trunk(x, mem_k, mem_v, s_hgrn, s_rwkv, s_shift, p):
    nb, tseq, _ = x.shape
    x2 = x.reshape(nb * tseq, D_MODEL)
    rwkv_shape = s_rwkv.shape
    s_rwkv = s_rwkv.reshape(rwkv_shape[:3] + (RW_N // 2, 2 * RW_N))
    new_h, new_r, new_s = None, None, []
    v_first = None
    for i in range(DEPTH):
        j = i // 2
        if i % 2 == 0:
            q, k, gl, v, xq, sg = _hgrn_proj(x2, p["norm_g"][i], p["w_in"][i], p["lbs"][j], tseq)
            mix, new_h = _hgrn_scan(q, k, v, gl, s_hgrn, j, new_h, p["hg_onorm_g"][j], tseq)
        else:
            vres = None if v_first is None else (p["rw_v0"][j - 1], p["rw_v1"][j - 1], p["rw_v2"][j - 1], v_first)
            r, k, v, lw, a_gate, xq, sg, shift = _rwkv_proj(
                x2, s_shift[j], tseq, p["norm_g"][i], p["rw_mu"][j], p["w_in"][i],
                p["rw_w0"][j], p["rw_w1"][j], p["rw_w2"][j], p["rw_a0"][j], p["rw_a1"][j], p["rw_a2"][j], vres)
            if v_first is None:
                v_first = v
            mix, new_r = _rwkv_scan(r, k, v, lw, a_gate, s_rwkv, j, new_r, p["rw_par"][j], tseq)
            new_s.append(shift)
        x2 = _out_proj(x2, mix, xq, mem_k, mem_v, i, tseq, sg, p["w_out"][i], p["final_g"], final=(i == DEPTH - 1))
    return x2.reshape(nb, tseq, D_MODEL), new_h, new_r.reshape(rwkv_shape), jnp.stack(new_s)


def kernel(x_prompt, x_sample, mem_prompt, state_hgrn, state_rwkv, state_shift, cache_mem_k, cache_mem_v,
           norm_g, w_in, w_out, mem_norm_g, w_mem_kv, hg_lb, hg_onorm_g, rw_mu, rw_w0, rw_w1, rw_w2,
           rw_a0, rw_a1, rw_a2, rw_v0, rw_v1, rw_v2, rw_kk, rw_ka, rw_rk, rw_lnx_g, rw_lnx_b, final_g):
    n_rwkv = rw_mu.shape[0]
    lbs = jax.nn.softmax(hg_lb.astype(F32), axis=0)
    lbs = jnp.cumsum(lbs, axis=0) - lbs[0]
    par = jnp.stack([rw_kk, rw_ka, rw_rk.reshape(n_rwkv, MIX_W), rw_lnx_g, rw_lnx_b], axis=1)
    par = jnp.concatenate([par, jnp.zeros((n_rwkv, 3, MIX_W), F32)], axis=1)
    par = par.reshape(n_rwkv, 8, RW_PAIRS, LANES).transpose(0, 2, 1, 3)
    p = dict(norm_g=norm_g, w_in=w_in.astype(BF16), w_out=w_out.astype(BF16), lbs=lbs, hg_onorm_g=hg_onorm_g,
             rw_mu=rw_mu, rw_w0=rw_w0, rw_w1=rw_w1.astype(BF16), rw_w2=rw_w2.astype(BF16),
             rw_a0=rw_a0, rw_a1=rw_a1.astype(BF16), rw_a2=rw_a2.astype(BF16),
             rw_v0=rw_v0, rw_v1=rw_v1.astype(BF16), rw_v2=rw_v2.astype(BF16), rw_par=par, final_g=final_g)

    nb, mem_len, _ = mem_prompt.shape
    mk, mv = _memory_kv(mem_prompt.reshape(nb * mem_len, D_MODEL), mem_norm_g, w_mem_kv.astype(BF16))
    mk = mk.reshape(DEPTH, nb, mem_len, CROSS_W)
    mv = mv.reshape(DEPTH, nb, mem_len, CROSS_W)
    z_h = jnp.zeros((state_hgrn.shape[0], nb) + state_hgrn.shape[2:], F32)
    z_r = jnp.zeros((state_rwkv.shape[0], nb) + state_rwkv.shape[2:], F32)
    z_s = jnp.zeros((state_shift.shape[0], nb, D_MODEL), F32)
    y_p, sh_p, sr_p, ss_p = _trunk(x_prompt, mk, mv, z_h, z_r, z_s, p)

    nbs = x_sample.shape[0]
    cmk = cache_mem_k.reshape(DEPTH, nbs, mem_len * X_HEADS, X_DH)
    cmv = cache_mem_v.reshape(DEPTH, nbs, mem_len * X_HEADS, X_DH)
    y_s, sh_s, sr_s, ss_s = _trunk(x_sample, cmk, cmv, state_hgrn, state_rwkv, state_shift, p)
    return (y_p, y_s, sh_p, sr_p, ss_p,
            mk.reshape(DEPTH, nb, mem_len, X_HEADS, X_DH), mv.reshape(DEPTH, nb, mem_len, X_HEADS, X_DH),
            sh_s, sr_s, ss_s)
```

```python
import functools
import math

import jax
import jax.numpy as jnp
from jax import lax
from jax.experimental import pallas as pl
from jax.experimental.pallas import tpu as pltpu

F32 = jnp.float32
BF16 = jnp.bfloat16

D_MODEL = 1024
DEPTH = 4
MIX_W = D_MODEL
HG_HEADS = 8
HG_D = MIX_W // HG_HEADS
RW_N = 64
RW_HEADS = MIX_W // RW_N
RW_PAIRS = RW_HEADS // 2
RW_LNX_EPS = 64e-5
MEM_LEN = 256
X_HEADS = 4
X_DH = 128
CROSS_W = X_HEADS * X_DH
BRANCH_W = MIX_W + CROSS_W
IN_COLS = 3 * MIX_W + CROSS_W + BRANCH_W
NORM_EPS = 1e-6
LOG_FLOOR = 1e-30
NEG_BIG = -1e30

LANES = 128
SUBLANES = 8
CHUNK = 64
COL_CHUNK = 256
VMEM_LIMIT = 56 * 1024 * 1024

NT = (((1,), (1,)), ((), ()))
TN = (((0,), (0,)), ((), ()))


def _params(*sem):
    return pltpu.CompilerParams(dimension_semantics=sem, vmem_limit_bytes=VMEM_LIMIT)


def _resident(shape, index_map):
    return pl.BlockSpec(shape, index_map, pipeline_mode=pl.Buffered(1))


def _rms(x, g):
    return x * lax.rsqrt(jnp.mean(x * x, axis=-1, keepdims=True) + NORM_EPS) * g


def _sigmoid(z):
    return 1.0 / (1.0 + jnp.exp(-z))


def _split3(x):
    hi = x.astype(BF16)
    rest = x - hi.astype(F32)
    mid = rest.astype(BF16)
    low = (rest - mid.astype(F32)).astype(BF16)
    return jnp.concatenate([hi, mid, low], axis=1)


def _sum_rows(sel, x):
    parts = jnp.dot(sel, _split3(x), preferred_element_type=F32)
    return parts[:, 0:LANES] + parts[:, LANES:2 * LANES] + parts[:, 2 * LANES:3 * LANES]


def _narrow(rows):
    if rows % 16 == 0:
        return lambda a: a.astype(BF16)
    return lambda a: a.astype(BF16).astype(F32)


def _log2(n):
    l = int(math.log2(n))
    assert 1 << l == n, n
    return l


def _memkv_kernel(x_ref, g_ref, w_ref, k_ref, v_ref):
    hb = _rms(x_ref[...], g_ref[...]).astype(BF16)
    kv = jnp.dot(hb, w_ref[...], preferred_element_type=F32)
    k_ref[...] = kv[:, :CROSS_W]
    v_ref[...] = kv[:, CROSS_W:]


def _memory_kv(mem2d, g, w):
    n = mem2d.shape[0]
    tm = min(512, n)
    out = jax.ShapeDtypeStruct((DEPTH, n, CROSS_W), F32)
    return pl.pallas_call(
        _memkv_kernel,
        grid=(DEPTH, n // tm),
        in_specs=[
            pl.BlockSpec((tm, D_MODEL), lambda l, i: (i, 0)),
            pl.BlockSpec((None, 1, D_MODEL), lambda l, i: (l, 0, 0)),
            pl.BlockSpec((None, D_MODEL, 2 * CROSS_W), lambda l, i: (l, 0, 0)),
        ],
        out_specs=[
            pl.BlockSpec((None, tm, CROSS_W), lambda l, i: (l, i, 0)),
            pl.BlockSpec((None, tm, CROSS_W), lambda l, i: (l, i, 0)),
        ],
        out_shape=[out, out],
        compiler_params=_params("arbitrary", "arbitrary"),
        name="memory_kv",
    )(mem2d, g.reshape(DEPTH, 1, D_MODEL), w)


def _hgrn_proj_kernel(x_ref, g_ref, w_ref, lb_ref, q_ref, k_ref, gl_ref, v_ref, xq_ref, sg_ref):
    hb = _rms(x_ref[...], g_ref[...]).astype(BF16)
    for j in range(IN_COLS // COL_CHUNK):
        c0 = j * COL_CHUNK
        z = jnp.dot(hb, w_ref[:, c0:c0 + COL_CHUNK], preferred_element_type=F32)
        if c0 < MIX_W:
            q_ref[:, c0:c0 + COL_CHUNK] = (z * _sigmoid(z)).astype(q_ref.dtype)
        elif c0 < 2 * MIX_W:
            d0 = c0 - MIX_W
            lb = lb_ref[:, d0:d0 + COL_CHUNK]
            f = lb + (1.0 - lb) * _sigmoid(z)
            gl_ref[:, d0:d0 + COL_CHUNK] = jnp.log(jnp.maximum(f, LOG_FLOOR))
            k_ref[:, d0:d0 + COL_CHUNK] = ((1.0 - lb) * _sigmoid(-z)).astype(k_ref.dtype)
        elif c0 < 3 * MIX_W:
            d0 = c0 - 2 * MIX_W
            v_ref[:, d0:d0 + COL_CHUNK] = z.astype(v_ref.dtype)
        elif c0 < 3 * MIX_W + CROSS_W:
            d0 = c0 - 3 * MIX_W
            xq_ref[:, d0:d0 + COL_CHUNK] = z.astype(xq_ref.dtype)
        else:
            d0 = c0 - 3 * MIX_W - CROSS_W
            sg_ref[:, d0:d0 + COL_CHUNK] = (z * _sigmoid(z)).astype(sg_ref.dtype)


def _xq_dtype(tseq):
    return BF16 if tseq % 16 == 0 else F32


def _hgrn_proj(x2, g, w, lb, tseq):
    n = x2.shape[0]
    tm = min(1024, n)
    row = lambda width: pl.BlockSpec((tm, width), lambda i: (i, 0))
    sd = lambda width, dt: jax.ShapeDtypeStruct((n, width), dt)
    return pl.pallas_call(
        _hgrn_proj_kernel,
        grid=(n // tm,),
        in_specs=[
            row(D_MODEL),
            _resident((1, D_MODEL), lambda i: (0, 0)),
            _resident((D_MODEL, IN_COLS), lambda i: (0, 0)),
            _resident((1, MIX_W), lambda i: (0, 0)),
        ],
        out_specs=[row(MIX_W), row(MIX_W), row(MIX_W), row(MIX_W), row(CROSS_W), row(BRANCH_W)],
        out_shape=[sd(MIX_W, BF16), sd(MIX_W, BF16), sd(MIX_W, F32), sd(MIX_W, BF16),
                   sd(CROSS_W, _xq_dtype(tseq)), sd(BRANCH_W, BF16)],
        compiler_params=_params("arbitrary"),
        name="hgrn_proj",
    )(x2, g.reshape(1, D_MODEL), w, lb.reshape(1, MIX_W))


def _own_layer(sout_all_ref, layer, aliased):
    return sout_all_ref if aliased else sout_all_ref.at[layer]


def _zero_other_layers(sout_all_ref, layer, aliased, s):
    if aliased:
        return
    for other in range(sout_all_ref.shape[0]):
        if other != layer:
            for g in range(sout_all_ref.shape[1]):
                sout_all_ref[other, g, s] = jnp.zeros(sout_all_ref.shape[3:], F32)


def _hgrn_scan_kernel(q_ref, k_ref, v_ref, g_ref, s0_ref, og_ref, *rest, rb, tseq, layer, aliased):
    o_ref, sout_all_ref, st_ref = rest[1:] if aliased else rest
    sout_ref = _own_layer(sout_all_ref, layer, aliased)
    tc = min(tseq, CHUNK)
    gseq = CHUNK // tc
    narrow = _narrow(tc)
    n_chunks = rb // CHUNK
    steps_per_seq = max(1, tseq // rb)
    groups, n_states, width = st_ref.shape[0], st_ref.shape[1], st_ref.shape[2]
    i = pl.program_id(1)
    units = [(g, hh) for g in range(groups) for hh in range(width)]
    single = tseq < CHUNK

    if not single:
        @pl.when(i % steps_per_seq == 0)
        def _():
            def load(s, c):
                for g, hh in units:
                    st_ref[g, s, hh] = s0_ref[g, s, hh].T
                return c
            lax.fori_loop(0, n_states, load, 0)

    t_i = lax.broadcasted_iota(jnp.int32, (CHUNK, CHUNK), 0)
    s_i = lax.broadcasted_iota(jnp.int32, (CHUNK, CHUNK), 1)
    same = (t_i >> _log2(tc)) == (s_i >> _log2(tc))
    tri = jnp.where(same & (s_i <= t_i), 1.0, 0.0).astype(BF16)
    ones_seq = jnp.where(same, 1.0, 0.0).astype(BF16)
    row = lax.broadcasted_iota(jnp.int32, (CHUNK, 1), 0)
    og = og_ref[...]
    halves = [1 << lev for lev in range(_log2(tc))]
    late = [(row & half) != 0 for half in halves]
    pair_mask = [jnp.where(((t_i >> _log2(2 * half)) == (s_i >> _log2(2 * half)))
                           & ((t_i & half) != 0) & ((s_i & half) == 0), 1.0, 0.0).astype(F32) for half in halves]
    small = [half for half in halves if 2 * half < SUBLANES]
    pick = jnp.concatenate([jnp.where(s_i == (t_i & ~(2 * half - 1)) + half - 1, 1.0, 0.0)
                            for half in small], axis=0).astype(BF16)

    if single:
        piece_row = lax.broadcasted_iota(jnp.int32, (3 * HG_D, HG_D), 0) & (HG_D - 1)
        spread = [jnp.where(piece_row == gi * tc, 1.0, 0.0).astype(BF16) for gi in range(gseq)]

    heads = range(len(units))

    def tokens(ref, rows):
        return [ref[g, rows, hh * HG_D:(hh + 1) * HG_D] for g, hh in units]

    def chunk(j, carry):
        rows = pl.ds(pl.multiple_of(j * CHUNK, CHUNK), CHUNK)
        q = [x.astype(F32) for x in tokens(q_ref, rows)]
        k = [x.astype(F32) for x in tokens(k_ref, rows)]
        v = [x.astype(F32) for x in tokens(v_ref, rows)]
        g = tokens(g_ref, rows)
        b = [_sum_rows(tri, x) for x in g]
        b_end = [x[CHUNK - 1:CHUNK, :] for x in b] if gseq == 1 else [_sum_rows(ones_seq, x) for x in g]
        q_in = [q[h] * jnp.exp(b[h]) for h in heads]
        k_dec = [k[h] * jnp.exp(b_end[h] - b[h]) for h in heads]
        vb = [x.astype(BF16) for x in v]

        o = []
        if single:
            decay_t = [_split3(jnp.concatenate([jnp.exp(b_end[h]), jnp.zeros((HG_D - CHUNK, HG_D), F32)], axis=0).T)
                       for h in heads]
        for gi in range(gseq):
            sl = slice(gi * tc, (gi + 1) * tc)
            sidx = j * gseq + gi if n_states > 1 else 0
            if single:
                st = [s0_ref[g, sidx, hh] for g, hh in units]
                o.append([jnp.dot(narrow(q_in[h][sl]), narrow(st[h]), preferred_element_type=F32) for h in heads])
                upd = [lax.dot_general(narrow(k_dec[h][sl]), narrow(v[h][sl]), TN, preferred_element_type=F32)
                       for h in heads]
                decay = [jnp.dot(decay_t[h], spread[gi], preferred_element_type=F32) for h in heads]
                for h, (g, hh) in enumerate(units):
                    sout_ref[g, sidx, hh] = st[h] * decay[h] + upd[h]
                _zero_other_layers(sout_all_ref, layer, aliased, sidx)
                continue
            st = [st_ref[g, sidx, hh] for g, hh in units]
            o.append([lax.dot_general(narrow(q_in[h][sl]), narrow(st[h]), NT, preferred_element_type=F32)
                      for h in heads])
            upd = [lax.dot_general(narrow(v[h][sl]), narrow(k_dec[h][sl]), TN, preferred_element_type=F32)
                   for h in heads]
            for h in heads:
                decay = b_end[h] if gseq == 1 else b_end[h][gi * tc:gi * tc + 1, :]
                st_ref[units[h][0], sidx, units[h][1]] = st[h] * jnp.exp(decay) + upd[h]
        o = [o[0][h] if gseq == 1 else jnp.concatenate([part[h] for part in o], axis=0) for h in heads]

        beta_small = [_sum_rows(pick, x) for x in b]
        att = [None for _ in heads]
        for lev, (half, is_late, mask) in enumerate(zip(halves, late, pair_mask)):
            for h in heads:
                if half in small:
                    beta = beta_small[h][lev * CHUNK:(lev + 1) * CHUNK]
                else:
                    blocks = b[h].reshape(CHUNK // (2 * half), 2 * half, HG_D)
                    beta = jnp.broadcast_to(blocks[:, half - 1:half, :], blocks.shape).reshape(CHUNK, HG_D)
                q_l = (q[h] * jnp.exp(jnp.where(is_late, b[h] - beta, NEG_BIG))).astype(BF16)
                k_l = (k[h] * jnp.exp(jnp.where(is_late, NEG_BIG, beta - b[h]))).astype(BF16)
                part = lax.dot_general(q_l, k_l, NT, preferred_element_type=F32) * mask
                att[h] = part if att[h] is None else att[h] + part
        o = [o[h] + jnp.dot(att[h].astype(BF16), vb[h], preferred_element_type=F32) for h in heads]
        o = [o[h] + jnp.sum(q[h] * k[h], axis=-1, keepdims=True) * v[h] for h in heads]

        for h in heads:
            out = o[h] * lax.rsqrt(jnp.mean(o[h] * o[h], axis=-1, keepdims=True) + NORM_EPS) * og
            g, hh = units[h]
            o_ref[g, rows, hh * HG_D:(hh + 1) * HG_D] = out.astype(o_ref.dtype)
        return carry

    lax.fori_loop(0, n_chunks, chunk, 0)

    if not single:
        @pl.when(i % steps_per_seq == steps_per_seq - 1)
        def _():
            def store(s, c):
                for g, hh in units:
                    sout_ref[g, s, hh] = st_ref[g, s, hh].T
                _zero_other_layers(sout_all_ref, layer, aliased, s)
                return c
            lax.fori_loop(0, n_states, store, 0)


SCAN_WIDTH = 8
SCAN_GROUPS = 2
SCAN_ROWS = 256
STATE_BLOCK_BYTES = 4 * 1024 * 1024


def _scan_geometry(n, tseq):
    rb = min(SCAN_ROWS, n)
    if tseq < rb:
        max_states = STATE_BLOCK_BYTES // (SCAN_WIDTH * LANES * LANES * 4)
        rb = min(rb, max(CHUNK, max_states * tseq))
    assert rb % CHUNK == 0 and n % rb == 0
    assert (tseq % rb == 0) or (rb % tseq == 0 and CHUNK % tseq == 0)
    n_states = max(1, rb // tseq)
    steps_per_seq = max(1, tseq // rb)
    return rb, n_states, steps_per_seq


def _layer_state_call(kernel_fn, name, heads, state_tail, grid0, tok_width, tok_inputs, consts, const_specs,
                      s_all, layer, s_new_all, tseq, short_groups=1):
    n = tok_inputs[0].shape[0]
    layers, nb = s_all.shape[0], s_all.shape[1]
    groups = SCAN_GROUPS if tseq >= CHUNK else short_groups
    assert n % groups == 0 and nb % groups == 0
    rb, n_states, steps_per_seq = _scan_geometry(n // groups, tseq)
    grouped = (layers, groups, nb // groups) + s_all.shape[2:]
    tok = pl.BlockSpec((groups, rb, tok_width), lambda h, i: (0, i, h))
    state = pl.BlockSpec((None, groups, n_states, heads) + state_tail,
                         lambda h, i: (layer, 0, i // steps_per_seq, h) + (0,) * len(state_tail))
    in_specs = [tok] * len(tok_inputs) + [state] + const_specs
    args = [t.reshape(groups, n // groups, t.shape[1]) for t in tok_inputs]
    args += [s_all.reshape(grouped)] + consts
    aliases = {}
    if s_new_all is not None:
        in_specs.append(pl.BlockSpec(memory_space=pl.ANY))
        args.append(s_new_all.reshape(grouped))
        aliases = {len(args) - 1: 1}
        state_out = state
    else:
        state_out = pl.BlockSpec((layers, groups, n_states, heads) + state_tail,
                                 lambda h, i: (0, 0, i // steps_per_seq, h) + (0,) * len(state_tail))
    out, s_new = pl.pallas_call(
        functools.partial(kernel_fn, rb=rb, tseq=tseq, layer=layer, aliased=s_new_all is not None),
        grid=(grid0, n // groups // rb),
        in_specs=in_specs,
        out_specs=[tok, state_out],
        out_shape=[jax.ShapeDtypeStruct((groups, n // groups, MIX_W), BF16),
                   jax.ShapeDtypeStruct(grouped, F32)],
        scratch_shapes=[pltpu.VMEM((groups, n_states, SCAN_WIDTH, LANES, LANES), F32)],
        input_output_aliases=aliases,
        compiler_params=_params("arbitrary", "arbitrary"),
        name=name,
    )(*args)
    return out.reshape(n, MIX_W), s_new.reshape(s_all.shape)


def _hgrn_scan(q, k, v, gl, s_all, layer, s_new_all, og, tseq):
    return _layer_state_call(
        _hgrn_scan_kernel, "hgrn_scan", SCAN_WIDTH, (HG_D, HG_D), HG_HEADS // SCAN_WIDTH, SCAN_WIDTH * HG_D,
        [q, k, v, gl], [og.reshape(1, HG_D)], [_resident((1, HG_D), lambda h, i: (0, 0))],
        s_all, layer, s_new_all, tseq)


def _rwkv_proj_kernel(*refs, tseq, long_seq, has_vres):
    it = iter(refs)
    x_ref, prev_ref, ng_ref, mu_ref, w_ref = (next(it) for _ in range(5))
    w0_ref, w1_ref, w2_ref, a0_ref, a1_ref, a2_ref = (next(it) for _ in range(6))
    if has_vres:
        v0_ref, v1_ref, v2_ref, vf_ref = (next(it) for _ in range(4))
    r_ref, k_ref, v_ref, lw_ref, as_ref, xq_ref, sg_ref, h_ref = (next(it) for _ in range(8))
    carry_ref = next(it) if long_seq else None

    x = x_ref[0] if long_seq else x_ref[...]
    tm = x.shape[0]
    h = _rms(x, ng_ref[...])
    shifted = pltpu.roll(h, 1, 0)
    row = lax.broadcasted_iota(jnp.int32, (tm, 1), 0)
    if long_seq:
        tb = pl.program_id(1)

        @pl.when(tb == 0)
        def _():
            carry_ref[0:1, :] = prev_ref[0]

        h_prev = jnp.where(row == 0, carry_ref[0:1, :], shifted)
        carry_ref[0:1, :] = h[tm - 1:tm, :]

        @pl.when(tb == pl.num_programs(1) - 1)
        def _():
            h_ref[0] = h[tm - 1:tm, :]
    else:
        h_prev = jnp.where((row & (tseq - 1)) == 0, prev_ref[...], shifted)
        h_ref[...] = h

    dx = h_prev - h
    hb = h.astype(BF16)
    xr = (h + dx * mu_ref[0:1, :]).astype(BF16)
    xw = (h + dx * mu_ref[1:2, :]).astype(BF16)
    xk = (h + dx * mu_ref[2:3, :]).astype(BF16)
    xv = (h + dx * mu_ref[3:4, :]).astype(BF16)
    xa = (h + dx * mu_ref[4:5, :]).astype(BF16)

    w_mid = jnp.tanh(jnp.dot(xw, w1_ref[...], preferred_element_type=F32)).astype(BF16)
    a_mid = jnp.dot(xa, a1_ref[...], preferred_element_type=F32).astype(BF16)
    if has_vres:
        v_mid = jnp.dot(xv, v1_ref[...], preferred_element_type=F32).astype(BF16)

    for j in range(MIX_W // COL_CHUNK):
        c0 = j * COL_CHUNK
        cols = slice(c0, c0 + COL_CHUNK)
        r_ref[:, cols] = jnp.dot(xr, w_ref[:, c0:c0 + COL_CHUNK], preferred_element_type=F32).astype(r_ref.dtype)
        k_ref[:, cols] = jnp.dot(xk, w_ref[:, MIX_W + c0:MIX_W + c0 + COL_CHUNK],
                                 preferred_element_type=F32).astype(k_ref.dtype)
        vz = jnp.dot(xv, w_ref[:, 2 * MIX_W + c0:2 * MIX_W + c0 + COL_CHUNK], preferred_element_type=F32)
        if has_vres:
            gate = _sigmoid(v0_ref[:, cols] + jnp.dot(v_mid, v2_ref[:, cols], preferred_element_type=F32))
            vz = vz + (vf_ref[:, cols].astype(F32) - vz) * gate
        v_ref[:, cols] = vz.astype(v_ref.dtype)
        u = w0_ref[:, cols] + jnp.dot(w_mid, w2_ref[:, cols], preferred_element_type=F32)
        softplus = jnp.maximum(-u, 0.0) + jnp.log(1.0 + jnp.exp(-jnp.abs(u)))
        lw_ref[:, cols] = -jnp.exp(-softplus - 0.5)
        as_ref[:, cols] = _sigmoid(a0_ref[:, cols] + jnp.dot(a_mid, a2_ref[:, cols], preferred_element_type=F32))

    base = 3 * MIX_W
    for j in range(CROSS_W // COL_CHUNK):
        c0 = j * COL_CHUNK
        xq_ref[:, c0:c0 + COL_CHUNK] = jnp.dot(hb, w_ref[:, base + c0:base + c0 + COL_CHUNK],
                                                preferred_element_type=F32).astype(xq_ref.dtype)
    base = 3 * MIX_W + CROSS_W
    for j in range(BRANCH_W // COL_CHUNK):
        c0 = j * COL_CHUNK
        z = jnp.dot(hb, w_ref[:, base + c0:base + c0 + COL_CHUNK], preferred_element_type=F32)
        sg_ref[:, c0:c0 + COL_CHUNK] = (z * _sigmoid(z)).astype(sg_ref.dtype)


def _rwkv_proj(x2, shift, tseq, ng, mu, w, w0, w1, w2, a0, a1, a2, vres):
    n = x2.shape[0]
    nb = n // tseq
    tm = min(512, n)
    long_seq = tseq >= tm
    has_vres = vres is not None
    vec = lambda a: a.reshape(1, -1)
    mu8 = jnp.zeros((8, D_MODEL), F32).at[:5].set(mu)
    consts = [vec(ng), mu8, w, vec(w0), w1, w2, vec(a0), a1, a2]
    if has_vres:
        v0, v1, v2, v_first = vres
        consts += [vec(v0), v1, v2]
    if long_seq:
        assert tseq % tm == 0
        steps = tseq // tm
        grid = (nb, steps)
        cmap = lambda b, t: (0, 0)
        row = lambda width: pl.BlockSpec((tm, width), lambda b, t: (b * steps + t, 0))
        x_in = x2.reshape(nb, tseq, D_MODEL)
        x_spec = pl.BlockSpec((1, tm, D_MODEL), lambda b, t: (b, t, 0))
        prev_in = shift.reshape(nb, 1, D_MODEL)
        prev_spec = pl.BlockSpec((1, 1, D_MODEL), lambda b, t: (b, 0, 0))
        h_shape = jax.ShapeDtypeStruct((nb, 1, D_MODEL), F32)
        h_spec = pl.BlockSpec((1, 1, D_MODEL), lambda b, t: (b, 0, 0))
        scratch = [pltpu.VMEM((8, D_MODEL), F32)]
        sem = ("arbitrary", "arbitrary")
    else:
        assert tm % tseq == 0 and tseq & (tseq - 1) == 0
        grid = (n // tm,)
        cmap = lambda i: (0, 0)
        row = lambda width: pl.BlockSpec((tm, width), lambda i: (i, 0))
        x_in, x_spec = x2, row(D_MODEL)
        prev_in = jnp.zeros((nb, tseq, D_MODEL), F32).at[:, 0].set(shift).reshape(n, D_MODEL)
        prev_spec = row(D_MODEL)
        h_shape = jax.ShapeDtypeStruct((n, D_MODEL), F32)
        h_spec = row(D_MODEL)
        scratch = []
        sem = ("arbitrary",)
    in_specs = [x_spec, prev_spec] + [_resident(c.shape, cmap) for c in consts]
    args = [x_in, prev_in] + consts
    if has_vres:
        in_specs.append(row(MIX_W))
        args.append(v_first)
    sd = lambda width, dt: jax.ShapeDtypeStruct((n, width), dt)
    outs = pl.pallas_call(
        functools.partial(_rwkv_proj_kernel, tseq=tseq, long_seq=long_seq, has_vres=has_vres),
        grid=grid,
        in_specs=in_specs,
        out_specs=[row(MIX_W)] * 5 + [row(CROSS_W), row(BRANCH_W), h_spec],
        out_shape=[sd(MIX_W, BF16)] * 3 + [sd(MIX_W, F32)] * 2
        + [sd(CROSS_W, _xq_dtype(tseq)), sd(BRANCH_W, BF16), h_shape],
        scratch_shapes=scratch,
        compiler_params=_params(*sem),
        name="rwkv_proj",
    )(*args)
    h_out = outs[7]
    new_shift = h_out.reshape(nb, D_MODEL) if long_seq else h_out.reshape(nb, tseq, D_MODEL)[:, -1]
    return list(outs[:7]) + [new_shift]


def _rwkv_scan_kernel(r_ref, k_ref, v_ref, lw_ref, as_ref, s0_ref, par_ref, *rest, rb, tseq, layer, aliased):
    y_ref, sout_all_ref, st_ref = rest[1:] if aliased else rest
    sout_ref = _own_layer(sout_all_ref, layer, aliased)
    tc = min(tseq, CHUNK)
    gseq = CHUNK // tc
    n_chunks = rb // CHUNK
    steps_per_seq = max(1, tseq // rb)
    groups, n_states, width = st_ref.shape[0], st_ref.shape[1], st_ref.shape[2]
    n_levels = _log2(tc)
    narrow = _narrow(tc)
    i = pl.program_id(1)

    units = [(g, p) for g in range(groups) for p in range(width)]
    pairs = range(len(units))
    lane = lax.broadcasted_iota(jnp.int32, (1, LANES), 1)

    def state_rows(head, parity):
        return pl.ds(head * RW_N + parity, RW_N // 2, stride=2)

    def unpack_states(s):
        for g, p in units:
            e, f = s0_ref[g, s, 2 * p], s0_ref[g, s, 2 * p + 1]
            st_ref[g, s, p, state_rows(0, 0), :] = jnp.where(lane < RW_N, e, 0.0)
            st_ref[g, s, p, state_rows(0, 1), :] = jnp.where(lane < RW_N, pltpu.roll(e, RW_N, 1), 0.0)
            st_ref[g, s, p, state_rows(1, 0), :] = jnp.where(lane >= RW_N, pltpu.roll(f, RW_N, 1), 0.0)
            st_ref[g, s, p, state_rows(1, 1), :] = jnp.where(lane >= RW_N, f, 0.0)

    def pack_states(s):
        for g, p in units:
            even0, odd0 = st_ref[g, s, p, state_rows(0, 0), :], st_ref[g, s, p, state_rows(0, 1), :]
            even1, odd1 = st_ref[g, s, p, state_rows(1, 0), :], st_ref[g, s, p, state_rows(1, 1), :]
            sout_ref[g, s, 2 * p] = jnp.where(lane < RW_N, even0, pltpu.roll(odd0, RW_N, 1))
            sout_ref[g, s, 2 * p + 1] = jnp.where(lane < RW_N, pltpu.roll(even1, RW_N, 1), odd1)
        _zero_other_layers(sout_all_ref, layer, aliased, s)

    single = tseq < CHUNK

    if not single:
        @pl.when(i % steps_per_seq == 0)
        def _():
            def load(s, c):
                unpack_states(s)
                return c
            lax.fori_loop(0, n_states, load, 0)

    t_i = lax.broadcasted_iota(jnp.int32, (CHUNK, CHUNK), 0)
    s_i = lax.broadcasted_iota(jnp.int32, (CHUNK, CHUNK), 1)
    same = (t_i >> _log2(tc)) == (s_i >> _log2(tc))
    tri = jnp.where(same & (s_i <= t_i), 1.0, 0.0).astype(BF16)
    ones_seq = jnp.where(same, 1.0, 0.0).astype(BF16)
    n_i = lax.broadcasted_iota(jnp.int32, (CHUNK, 2 * CHUNK), 0)
    m_i = lax.broadcasted_iota(jnp.int32, (CHUNK, 2 * CHUNK), 1) & (CHUNK - 1)
    same2 = (n_i >> _log2(tc)) == (m_i >> _log2(tc))
    strict2 = jnp.where(same2 & (m_i < n_i), 1.0, 0.0).astype(F32)
    incl2 = jnp.where(same2 & (m_i <= n_i), 1.0, 0.0).astype(F32)
    incl_cat = jnp.concatenate([incl2, incl2], axis=1)
    head0 =jnp.where(lane < RW_N, 1.0, 0.0).astype(F32)
    head1 = 1.0 - head0
    ones_head = jnp.where((lax.broadcasted_iota(jnp.int32, (LANES, LANES), 0) >> _log2(RW_N))
                          == (lax.broadcasted_iota(jnp.int32, (LANES, LANES), 1) >> _log2(RW_N)),
                          1.0, 0.0).astype(F32)
    head0_b, head1_b = head0.astype(BF16), head1.astype(BF16)

    def stack(a):
        ab = a.astype(BF16)
        return jnp.concatenate([ab * head0_b, ab * head1_b], axis=0)

    def head_sum(a):
        s0 = jnp.sum(a * head0, axis=-1, keepdims=True)
        s1 = jnp.sum(a * head1, axis=-1, keepdims=True)
        return jnp.where(lane < RW_N, s0, s1)

    def tokens(ref, rows):
        return [ref[g, rows, p * LANES:(p + 1) * LANES].astype(F32) for g, p in units]

    def par(u, row):
        return par_ref[units[u][1], row:row + 1, :]

    def chunk(j, carry):
        rows = pl.ds(pl.multiple_of(j * CHUNK, CHUNK), CHUNK)
        r, k, v = tokens(r_ref, rows), tokens(k_ref, rows), tokens(v_ref, rows)
        lw, a_gate = tokens(lw_ref, rows), tokens(as_ref, rows)

        c = [_sum_rows(tri, x) for x in lw]
        c_end = [x[CHUNK - 1:CHUNK, :] for x in c] if gseq == 1 else [_sum_rows(ones_seq, x) for x in lw]
        kk = [k[p] * par(p, 0) for p in pairs]
        kk = [x * lax.rsqrt(jnp.maximum(head_sum(x * x), 1e-24)) for x in kk]
        b_vec = [kk[p] * a_gate[p] for p in pairs]
        k_mod = [k[p] * (1.0 + (a_gate[p] - 1.0) * par(p, 1)) for p in pairs]
        bonus = [head_sum(r[p] * k_mod[p] * par(p, 2)) for p in pairs]
        e_neg = [jnp.exp(-x) for x in c]
        e_end = [jnp.exp(c_end[p] - c[p]) for p in pairs]
        b_hat = [narrow(b_vec[p] * e_end[p]) for p in pairs]
        k_hat = [narrow(k_mod[p] * e_end[p]) for p in pairs]
        a_t = [-kk[p] * jnp.exp(c[p] - lw[p]) for p in pairs]
        r_t = [r[p] * jnp.exp(c[p]) for p in pairs]

        lhs = [jnp.concatenate([a_t[p], r_t[p]], axis=0).astype(BF16) for p in pairs]
        rhs = [jnp.concatenate([stack(b_vec[p] * e_neg[p]), stack(k_mod[p] * e_neg[p])], axis=0) for p in pairs]
        big = [lax.dot_general(lhs[p], rhs[p], NT, preferred_element_type=F32) for p in pairs]
        l_ab = [x[0:CHUNK, 0:LANES] * strict2 for x in big]
        l_ak = [(x[0:CHUNK, LANES:2 * LANES] * strict2).astype(BF16) for x in big]
        m_cat = [(x[CHUNK:2 * CHUNK, :] * incl_cat).astype(BF16) for x in big]
        v_s = [stack(x) for x in v]

        ah0, rh0 = [], []
        for gi in range(gseq):
            sl = slice(gi * tc, (gi + 1) * tc)
            sidx = j * gseq + gi if n_states > 1 else 0
            if single:
                unpack_states(sidx)
            st = [st_ref[g, sidx, p] for g, p in units]
            ar =[jnp.concatenate([a_t[p][sl], r_t[p][sl]], axis=0) for p in pairs]
            prod = [lax.dot_general(narrow(ar[p]), narrow(st[p]), NT, preferred_element_type=F32) for p in pairs]
            ah0.append([x[0:tc] for x in prod])
            rh0.append([x[tc:2 * tc] for x in prod])
        ah0 = [ah0[0][p] if gseq == 1 else jnp.concatenate([part[p] for part in ah0], axis=0) for p in pairs]
        rh0 = [rh0[0][p] if gseq == 1 else jnp.concatenate([part[p] for part in rh0], axis=0) for p in pairs]

        u = [ah0[p] + jnp.dot(l_ak[p], v_s[p], preferred_element_type=F32) for p in pairs]
        pw = l_ab
        for lev in range(n_levels):
            pb = [m.astype(BF16) for m in pw]
            u = [u[p] + jnp.dot(pb[p], stack(u[p]), preferred_element_type=F32) for p in pairs]
            if lev < n_levels - 1:
                pw = [jnp.dot(pb[p], stack(pw[p]), preferred_element_type=F32) for p in pairs]
        y = [rh0[p] + jnp.dot(m_cat[p], jnp.concatenate([stack(u[p]), v_s[p]], axis=0),
                              preferred_element_type=F32) for p in pairs]

        v = tokens(v_ref, rows)
        for gi in range(gseq):
            sl = slice(gi * tc, (gi + 1) * tc)
            sidx = j * gseq + gi if n_states > 1 else 0
            uv = [jnp.concatenate([u[p][sl], v[p][sl]], axis=0) for p in pairs]
            bk = [jnp.concatenate([b_hat[p][sl], k_hat[p][sl]], axis=0) for p in pairs]
            upd = [lax.dot_general(narrow(uv[p]), bk[p], TN, preferred_element_type=F32) for p in pairs]
            for p, (g, pp) in enumerate(units):
                decay = c_end[p] if gseq == 1 else c_end[p][gi * tc:gi * tc + 1, :]
                st_ref[g, sidx, pp] = st_ref[g, sidx, pp] * jnp.exp(decay) + upd[p] * ones_head
            if single:
                pack_states(sidx)

        mean = [head_sum(m) * (1.0 / RW_N) for m in y]
        d = [y[p] - mean[p] for p in pairs]
        var = [head_sum(m * m) * (1.0 / RW_N) for m in d]
        for p in pairs:
            out = d[p] * lax.rsqrt(var[p] + RW_LNX_EPS) * par(p, 3) + par(p, 4)
            g, pp = units[p]
            y_ref[g, rows, pp * LANES:(pp + 1) * LANES] = (out + bonus[p] * v[p]).astype(y_ref.dtype)
        return carry

    lax.fori_loop(0, n_chunks, chunk, 0)

    if not single:
        @pl.when(i % steps_per_seq == steps_per_seq - 1)
        def _():
            def store(s, c):
                pack_states(s)
                return c
            lax.fori_loop(0, n_states, store, 0)


def _rwkv_scan(r, k, v, lw, a_gate, s_all, layer, s_new_all, par, tseq):
    return _layer_state_call(
        _rwkv_scan_kernel, "rwkv_scan", 2 * SCAN_WIDTH, (RW_N // 2, LANES), RW_PAIRS // SCAN_WIDTH, SCAN_WIDTH * LANES,
        [r, k, v, lw, a_gate], [par], [pl.BlockSpec((SCAN_WIDTH, 8, LANES), lambda p, i: (p, 0, 0))],
        s_all, layer, s_new_all, tseq, short_groups=SCAN_GROUPS)


def _cross_attention(q_ref, k_ref, v_ref, rows_per_seq, interleaved):
    n_seq = k_ref.shape[0]
    scale = X_DH ** -0.5
    units = [(s, hh) for s in range(n_seq) for hh in range(X_HEADS)]

    def memory(ref, s, hh):
        if interleaved:
            return ref[s, pl.ds(hh, MEM_LEN, stride=X_HEADS), :].astype(BF16)
        return ref[s, :, hh * X_DH:(hh + 1) * X_DH].astype(BF16)

    q = [q_ref[s * rows_per_seq:(s + 1) * rows_per_seq, hh * X_DH:(hh + 1) * X_DH].astype(BF16) for s, hh in units]
    sc = [lax.dot_general(q[u], memory(k_ref, s, hh), NT, preferred_element_type=F32) * scale
          for u, (s, hh) in enumerate(units)]
    p = [jnp.exp(x - jnp.max(x, axis=-1, keepdims=True)) for x in sc]
    denom = [jnp.sum(x, axis=-1, keepdims=True) for x in p]
    o = [jnp.dot(p[u].astype(BF16), memory(v_ref, s, hh), preferred_element_type=F32) / denom[u]
         for u, (s, hh) in enumerate(units)]
    return [jnp.concatenate(o[s * X_HEADS:(s + 1) * X_HEADS], axis=-1) for s in range(n_seq)]


def _out_kernel(x_ref, mix_ref, xq_ref, k_ref, v_ref, sg_ref, w_ref, fg_ref, o_ref, *, rows_per_seq, interleaved,
                final):
    xo = _cross_attention(xq_ref, k_ref, v_ref, rows_per_seq, interleaved)
    xo = xo[0] if len(xo) == 1 else jnp.concatenate(xo, axis=0)
    sg = sg_ref[...].astype(F32)
    left = (mix_ref[...].astype(F32) * sg[:, :MIX_W]).astype(BF16)
    right = (xo * sg[:, MIX_W:]).astype(BF16)
    x = x_ref[...] + jnp.dot(left, w_ref[0:MIX_W, :], preferred_element_type=F32) \
        + jnp.dot(right, w_ref[MIX_W:BRANCH_W, :], preferred_element_type=F32)
    o_ref[...] = _rms(x, fg_ref[...]) if final else x


def _out_proj(x2, mix, xq, mk_all, mv_all, layer, tseq, sg, w, final_g, final):
    n = x2.shape[0]
    interleaved = mk_all.shape[-1] == X_DH
    rb = min(1024, n, 8 * tseq)
    n_seq = max(1, rb // tseq)
    steps_per_seq = max(1, tseq // rb)
    row = lambda width: pl.BlockSpec((rb, width), lambda i: (i, 0))
    mem = pl.BlockSpec((None, n_seq) + mk_all.shape[2:], lambda i: (layer, i // steps_per_seq, 0, 0))
    return pl.pallas_call(
        functools.partial(_out_kernel, rows_per_seq=rb // n_seq, interleaved=interleaved, final=final),
        grid=(n // rb,),
        in_specs=[row(D_MODEL), row(MIX_W), row(CROSS_W), mem, mem, row(BRANCH_W),
                  _resident((BRANCH_W, D_MODEL), lambda i: (0, 0)),
                  _resident((1, D_MODEL), lambda i: (0, 0))],
        out_specs=row(D_MODEL),
        out_shape=jax.ShapeDtypeStruct((n, D_MODEL), F32),
        compiler_params=_params("arbitrary"),
        name="out_proj",
    )(x2, mix, xq, mk_all, mv_all, sg, w, final_g.reshape(1, D_MODEL))


def _trunk(x, mem_k, mem_v, s_hgrn, s_rwkv, s_shift, p):
    nb, tseq, _ = x.shape
    x2 = x.reshape(nb * tseq, D_MODEL)
    rwkv_shape = s_rwkv.shape
    s_rwkv = s_rwkv.reshape(rwkv_shape[:3] + (RW_N // 2, 2 * RW_N))
    new_h, new_r, new_s = None, None, []
    v_first = None
    for i in range(DEPTH):
        j = i // 2
        if i % 2 == 0:
            q, k, gl, v, xq, sg = _hgrn_proj(x2, p["norm_g"][i], p["w_in"][i], p["lbs"][j], tseq)
            mix, new_h = _hgrn_scan(q, k, v, gl, s_hgrn, j, new_h, p["hg_onorm_g"][j], tseq)
        else:
            vres = None if v_first is None else (p["rw_v0"][j - 1], p["rw_v1"][j - 1], p["rw_v2"][j - 1], v_first)
            r, k, v, lw, a_gate, xq, sg, shift = _rwkv_proj(
                x2, s_shift[j], tseq, p["norm_g"][i], p["rw_mu"][j], p["w_in"][i],
                p["rw_w0"][j], p["rw_w1"][j], p["rw_w2"][j], p["rw_a0"][j], p["rw_a1"][j], p["rw_a2"][j], vres)
            if v_first is None:
                v_first = v
            mix, new_r = _rwkv_scan(r, k, v, lw, a_gate, s_rwkv, j, new_r, p["rw_par"][j], tseq)
            new_s.append(shift)
        x2 = _out_proj(x2, mix, xq, mem_k, mem_v, i, tseq, sg, p["w_out"][i], p["final_g"], final=(i == DEPTH - 1))
    return x2.reshape(nb, tseq, D_MODEL), new_h, new_r.reshape(rwkv_shape), jnp.stack(new_s)


def kernel(x_prompt, x_sample, mem_prompt, state_hgrn, state_rwkv, state_shift, cache_mem_k, cache_mem_v,
           norm_g, w_in, w_out, mem_norm_g, w_mem_kv, hg_lb, hg_onorm_g, rw_mu, rw_w0, rw_w1, rw_w2,
           rw_a0, rw_a1, rw_a2, rw_v0, rw_v1, rw_v2, rw_kk, rw_ka, rw_rk, rw_lnx_g, rw_lnx_b, final_g):
    n_rwkv = rw_mu.shape[0]
    lbs = jax.nn.softmax(hg_lb.astype(F32), axis=0)
    lbs = jnp.cumsum(lbs, axis=0) - lbs[0]
    par = jnp.stack([rw_kk, rw_ka, rw_rk.reshape(n_rwkv, MIX_W), rw_lnx_g, rw_lnx_b], axis=1)
    par = jnp.concatenate([par, jnp.zeros((n_rwkv, 3, MIX_W), F32)], axis=1)
    par = par.reshape(n_rwkv, 8, RW_PAIRS, LANES).transpose(0, 2, 1, 3)
    p = dict(norm_g=norm_g, w_in=w_in.astype(BF16), w_out=w_out.astype(BF16), lbs=lbs, hg_onorm_g=hg_onorm_g,
             rw_mu=rw_mu, rw_w0=rw_w0, rw_w1=rw_w1.astype(BF16), rw_w2=rw_w2.astype(BF16),
             rw_a0=rw_a0, rw_a1=rw_a1.astype(BF16), rw_a2=rw_a2.astype(BF16),
             rw_v0=rw_v0, rw_v1=rw_v1.astype(BF16), rw_v2=rw_v2.astype(BF16), rw_par=par, final_g=final_g)

    nb, mem_len, _ = mem_prompt.shape
    mk, mv = _memory_kv(mem_prompt.reshape(nb * mem_len, D_MODEL), mem_norm_g, w_mem_kv.astype(BF16))
    mk = mk.reshape(DEPTH, nb, mem_len, CROSS_W)
    mv = mv.reshape(DEPTH, nb, mem_len, CROSS_W)
    z_h = jnp.zeros((state_hgrn.shape[0], nb) + state_hgrn.shape[2:], F32)
    z_r = jnp.zeros((state_rwkv.shape[0], nb) + state_rwkv.shape[2:], F32)
    z_s = jnp.zeros((state_shift.shape[0], nb, D_MODEL), F32)
    y_p, sh_p, sr_p, ss_p = _trunk(x_prompt, mk, mv, z_h, z_r, z_s, p)

    nbs = x_sample.shape[0]
    cmk = cache_mem_k.reshape(DEPTH, nbs, mem_len * X_HEADS, X_DH)
    cmv = cache_mem_v.reshape(DEPTH, nbs, mem_len * X_HEADS, X_DH)
    y_s, sh_s, sr_s, ss_s = _trunk(x_sample, cmk, cmv, state_hgrn, state_rwkv, state_shift, p)
    return (y_p, y_s, sh_p, sr_p, ss_p,
            mk.reshape(DEPTH, nb, mem_len, X_HEADS, X_DH), mv.reshape(DEPTH, nb, mem_len, X_HEADS, X_DH),
            sh_s, sr_s, ss_s)
```
